```python
import math
import jax, jax.numpy as jnp
from jax import lax
import numpy as np

D_MODEL = 2048
BATCH = 8
SEQ = 8192
DEPTH = 4

CHUNK = 64
GDN_HEAD_DIM = 128
GDN_WIDTH = D_MODEL // 2
GDN_HEADS = GDN_WIDTH // GDN_HEAD_DIM
SC_WIDTH = D_MODEL - GDN_WIDTH
GDN_CONV = 4
SC_CONV = 3
FFN_CONV = 3
N_MEM = 256
XATTN_HEADS = 4
XATTN_HEAD_DIM = D_MODEL // XATTN_HEADS
D_FF = ((8 * D_MODEL // 3 + 255) // 256) * 256
N_MIX_IN = 4 * GDN_WIDTH + 2 * GDN_HEADS + 3 * SC_WIDTH
EPS = 1e-6

kernel_name = 'hybrid_gdn_shortconv_memxattn_convffn'


def rmsnorm(x, w):
    xf = x.astype(jnp.float32)
    y = xf * lax.rsqrt(jnp.mean(xf * xf, axis=-1, keepdims=True) + EPS)
    return (y * w.astype(jnp.float32)).astype(x.dtype)


def l2norm(x):
    return x * lax.rsqrt(jnp.sum(x * x, axis=-1, keepdims=True) + EPS)


def causal_dwconv(x, w):
    K = w.shape[0]
    S = x.shape[1]
    w = w.astype(x.dtype)
    xp = jnp.pad(x, ((0, 0), (K - 1, 0), (0, 0)))
    y = xp[:, 0:S] * w[0]
    for j in range(1, K):
        y = y + xp[:, j:j + S] * w[j]
    return y


def gated_delta_rule(q, k, v, g, beta):
    Bsz, S, H, DK = q.shape
    DV = v.shape[-1]
    N = S // CHUNK

    def to_chunks(t):
        t = t.reshape((Bsz, N, CHUNK, H) + t.shape[3:])
        return jnp.moveaxis(t, 3, 1)

    q, k, v, g, beta = (to_chunks(t) for t in (q, k, v, g, beta))
    q = q * (DK ** -0.5)
    g = jnp.cumsum(g, axis=-1)
    causal = jnp.tril(jnp.ones((CHUNK, CHUNK), dtype=bool))
    strict = jnp.tril(jnp.ones((CHUNK, CHUNK), dtype=bool), k=-1)
    decay = jnp.exp(jnp.where(causal, g[..., :, None] - g[..., None, :], -jnp.inf))
    k_beta = k * beta[..., None]
    a_strict = jnp.where(strict, jnp.einsum('bhncd,bhnmd->bhncm', k_beta, k) * decay, 0.0)
    eye = jnp.eye(CHUNK, dtype=jnp.float32)
    rhs = jnp.concatenate([v * beta[..., None], k_beta * jnp.exp(g)[..., None]], axis=-1)
    sol = lax.linalg.triangular_solve(eye + a_strict, rhs, left_side=True, lower=True,
                                      unit_diagonal=True)
    u, w = sol[..., :DV], sol[..., DV:]
    attn = jnp.einsum('bhncd,bhnmd->bhncm', q, k) * decay
    q_dec = q * jnp.exp(g)[..., None]
    g_last = g[..., -1]
    k_dec = k * jnp.exp(g_last[..., None] - g)[..., None]

    def step(state, xs):
        q_i, k_i, u_i, w_i, attn_i, gl_i = xs
        v_new = u_i - jnp.einsum('bhcd,bhde->bhce', w_i, state)
        o_i = (jnp.einsum('bhcd,bhde->bhce', q_i, state)
               + jnp.einsum('bhcm,bhme->bhce', attn_i, v_new))
        state = (state * jnp.exp(gl_i)[..., None, None]
                 + jnp.einsum('bhcd,bhce->bhde', k_i, v_new))
        return state, o_i

    xs = tuple(jnp.moveaxis(t, 2, 0) for t in (q_dec, k_dec, u, w, attn, g_last))
    state0 = jnp.zeros((Bsz, H, DK, DV), jnp.float32)
    _, o = lax.scan(step, state0, xs)
    return jnp.transpose(o, (1, 0, 3, 2, 4)).reshape(Bsz, S, H, DV)


def gdn_group(proj, conv_w, a_log, dt_bias, out_gain):
    Bsz, S, _ = proj.shape
    W, H, Dh = GDN_WIDTH, GDN_HEADS, GDN_HEAD_DIM
    qkv = jax.nn.silu(causal_dwconv(proj[..., :3 * W], conv_w)).astype(jnp.float32)
    q = l2norm(qkv[..., :W].reshape(Bsz, S, H, Dh))
    k = l2norm(qkv[..., W:2 * W].reshape(Bsz, S, H, Dh))
    v = qkv[..., 2 * W:].reshape(Bsz, S, H, Dh)
    z = proj[..., 3 * W:4 * W].reshape(Bsz, S, H, Dh)
    b_raw = proj[..., 4 * W:4 * W + H].astype(jnp.float32)
    a_raw = proj[..., 4 * W + H:4 * W + 2 * H].astype(jnp.float32)
    beta = jax.nn.sigmoid(b_raw)
    g = -jnp.exp(a_log.astype(jnp.float32)) * jax.nn.softplus(a_raw + dt_bias.astype(jnp.float32))
    o = gated_delta_rule(q, k, v, g, beta)
    o = rmsnorm(o, out_gain).astype(proj.dtype) * jax.nn.silu(z)
    return o.reshape(Bsz, S, W)


def shortconv_group(proj, conv_w):
    off = 4 * GDN_WIDTH + 2 * GDN_HEADS
    b_gate = proj[..., off:off + SC_WIDTH]
    c_gate = proj[..., off + SC_WIDTH:off + 2 * SC_WIDTH]
    h = proj[..., off + 2 * SC_WIDTH:off + 3 * SC_WIDTH]
    return b_gate * causal_dwconv(c_gate * h, conv_w)


def memory_xattn(h, mem_n, w_q, w_k, w_v, w_o):
    Bsz, S, _ = h.shape
    q = (h @ w_q).reshape(Bsz, S, XATTN_HEADS, XATTN_HEAD_DIM)
    k = (mem_n @ w_k).reshape(Bsz, N_MEM, XATTN_HEADS, XATTN_HEAD_DIM)
    v = (mem_n @ w_v).reshape(Bsz, N_MEM, XATTN_HEADS, XATTN_HEAD_DIM)
    s = jnp.einsum('bshd,bmhd->bhsm', q, k).astype(jnp.float32) * (XATTN_HEAD_DIM ** -0.5)
    p = jax.nn.softmax(s, axis=-1).astype(v.dtype)
    o = jnp.einsum('bhsm,bmhd->bshd', p, v).reshape(Bsz, S, D_MODEL)
    return o @ w_o


def conv_ffn(h, w_up, conv_w, w_down):
    u = causal_dwconv(h @ w_up, conv_w)
    gate, up = u[..., :D_FF], u[..., D_FF:]
    return (jax.nn.silu(gate) * up) @ w_down


def _fwd_setup_inputs(seed: int = 0) -> dict:
    key = jax.random.key(seed)
    ks = jax.random.split(key, 24)
    L, D = DEPTH, D_MODEL
    out_scale = (3 * DEPTH) ** -0.5

    def normal(k, shape, std):
        return jax.random.normal(k, shape, jnp.float32) * std

    def gain(k, shape):
        return 1.0 + normal(k, shape, 0.02)

    dt = jnp.exp(jax.random.uniform(ks[6], (L, GDN_HEADS), jnp.float32,
                                    math.log(1e-3), math.log(1e-1)))
    return {
        'x': normal(ks[0], (BATCH, SEQ, D), 1.0),
        'mem': normal(ks[1], (BATCH, N_MEM, D), 1.0),
        'mix_norm': gain(ks[2], (L, D)),
        'w_mix_in': normal(ks[3], (L, D, N_MIX_IN), D ** -0.5),
        'gdn_conv': normal(ks[4], (L, GDN_CONV, 3 * GDN_WIDTH), GDN_CONV ** -0.5),
        'gdn_a_log': jnp.log(jax.random.uniform(ks[5], (L, GDN_HEADS), jnp.float32, 1.0, 16.0)),
        'gdn_dt_bias': dt + jnp.log(-jnp.expm1(-dt)),
        'gdn_out_norm': gain(ks[7], (L, GDN_HEAD_DIM)),
        'sc_conv': normal(ks[8], (L, SC_CONV, SC_WIDTH), SC_CONV ** -0.5),
        'w_mix_out': normal(ks[9], (L, D, D), D ** -0.5 * out_scale),
        'xattn_norm': gain(ks[10], (L, D)),
        'mem_norm': gain(ks[11], (L, D)),
        'w_xq': normal(ks[12], (L, D, D), D ** -0.5),
        'w_xk': normal(ks[13], (L, D, D), D ** -0.5),
        'w_xv': normal(ks[14], (L, D, D), D ** -0.5),
        'w_xo': normal(ks[15], (L, D, D), D ** -0.5 * out_scale),
        'ffn_norm': gain(ks[16], (L, D)),
        'w_ffn_up': normal(ks[17], (L, D, 2 * D_FF), D ** -0.5),
        'ffn_conv': normal(ks[18], (L, FFN_CONV, 2 * D_FF), FFN_CONV ** -0.5),
        'w_ffn_down': normal(ks[19], (L, D_FF, D), D_FF ** -0.5 * out_scale),
        'final_norm': gain(ks[20], (D,)),
    }


def _fwd_reference(x, mem, mix_norm, w_mix_in, gdn_conv, gdn_a_log, gdn_dt_bias, gdn_out_norm,
              sc_conv, w_mix_out, xattn_norm, mem_norm, w_xq, w_xk, w_xv, w_xo,
              ffn_norm, w_ffn_up, ffn_conv, w_ffn_down, final_norm):
    for l in range(DEPTH):
        h = rmsnorm(x, mix_norm[l])
        proj = h @ w_mix_in[l]
        y_gdn = gdn_group(proj, gdn_conv[l], gdn_a_log[l], gdn_dt_bias[l], gdn_out_norm[l])
        y_sc = shortconv_group(proj, sc_conv[l])
        x = x + jnp.concatenate([y_gdn, y_sc], axis=-1) @ w_mix_out[l]
        h = rmsnorm(x, xattn_norm[l])
        mem_n = rmsnorm(mem, mem_norm[l])
        x = x + memory_xattn(h, mem_n, w_xq[l], w_xk[l], w_xv[l], w_xo[l])
        h = rmsnorm(x, ffn_norm[l])
        x = x + conv_ffn(h, w_ffn_up[l], ffn_conv[l], w_ffn_down[l])
    return rmsnorm(x, final_norm)


import jax as _jax
import jax.numpy as _jnp

TWIN_FORMAT = 'train_step'
FWD_PARAMS = ['x', 'mem', 'mix_norm', 'w_mix_in', 'gdn_conv', 'gdn_a_log', 'gdn_dt_bias', 'gdn_out_norm', 'sc_conv', 'w_mix_out', 'xattn_norm', 'mem_norm', 'w_xq', 'w_xk', 'w_xv', 'w_xo', 'ffn_norm', 'w_ffn_up', 'ffn_conv', 'w_ffn_down', 'final_norm']
TWIN_WEIGHTS = ['mix_norm', 'w_mix_in', 'gdn_conv', 'gdn_a_log', 'gdn_dt_bias', 'gdn_out_norm', 'sc_conv', 'w_mix_out', 'xattn_norm', 'mem_norm', 'w_xq', 'w_xk', 'w_xv', 'w_xo', 'ffn_norm', 'w_ffn_up', 'ffn_conv', 'w_ffn_down', 'final_norm']
TWIN_DIFF_INPUT = 'x'
TWIN_INPUTS = ['x', 'mem', 'mix_norm', 'w_mix_in', 'gdn_conv', 'gdn_a_log', 'gdn_dt_bias', 'gdn_out_norm', 'sc_conv', 'w_mix_out', 'xattn_norm', 'mem_norm', 'w_xq', 'w_xk', 'w_xv', 'w_xo', 'ffn_norm', 'w_ffn_up', 'ffn_conv', 'w_ffn_down', 'final_norm', 'loss_target', 'm_mix_norm', 'm_w_mix_in', 'm_gdn_conv', 'm_gdn_a_log', 'm_gdn_dt_bias', 'm_gdn_out_norm', 'm_sc_conv', 'm_w_mix_out', 'm_xattn_norm', 'm_mem_norm', 'm_w_xq', 'm_w_xk', 'm_w_xv', 'm_w_xo', 'm_ffn_norm', 'm_w_ffn_up', 'm_ffn_conv', 'm_w_ffn_down', 'm_final_norm', 'v_mix_norm', 'v_w_mix_in', 'v_gdn_conv', 'v_gdn_a_log', 'v_gdn_dt_bias', 'v_gdn_out_norm', 'v_sc_conv', 'v_w_mix_out', 'v_xattn_norm', 'v_mem_norm', 'v_w_xq', 'v_w_xk', 'v_w_xv', 'v_w_xo', 'v_ffn_norm', 'v_w_ffn_up', 'v_ffn_conv', 'v_w_ffn_down', 'v_final_norm']
TWIN_OUTPUTS = ['loss', 'grad_x', 'grad_mix_norm', 'grad_w_mix_in', 'grad_gdn_conv', 'grad_gdn_a_log', 'grad_gdn_dt_bias', 'grad_gdn_out_norm', 'grad_sc_conv', 'grad_w_mix_out', 'grad_xattn_norm', 'grad_mem_norm', 'grad_w_xq', 'grad_w_xk', 'grad_w_xv', 'grad_w_xo', 'grad_ffn_norm', 'grad_w_ffn_up', 'grad_ffn_conv', 'grad_w_ffn_down', 'grad_final_norm', 'delta_mix_norm', 'delta_w_mix_in', 'delta_gdn_conv', 'delta_gdn_a_log', 'delta_gdn_dt_bias', 'delta_gdn_out_norm', 'delta_sc_conv', 'delta_w_mix_out', 'delta_xattn_norm', 'delta_mem_norm', 'delta_w_xq', 'delta_w_xk', 'delta_w_xv', 'delta_w_xo', 'delta_ffn_norm', 'delta_w_ffn_up', 'delta_ffn_conv', 'delta_w_ffn_down', 'delta_final_norm', 'new_m_mix_norm', 'new_m_w_mix_in', 'new_m_gdn_conv', 'new_m_gdn_a_log', 'new_m_gdn_dt_bias', 'new_m_gdn_out_norm', 'new_m_sc_conv', 'new_m_w_mix_out', 'new_m_xattn_norm', 'new_m_mem_norm', 'new_m_w_xq', 'new_m_w_xk', 'new_m_w_xv', 'new_m_w_xo', 'new_m_ffn_norm', 'new_m_w_ffn_up', 'new_m_ffn_conv', 'new_m_w_ffn_down', 'new_m_final_norm', 'new_v_mix_norm', 'new_v_w_mix_in', 'new_v_gdn_conv', 'new_v_gdn_a_log', 'new_v_gdn_dt_bias', 'new_v_gdn_out_norm', 'new_v_sc_conv', 'new_v_w_mix_out', 'new_v_xattn_norm', 'new_v_mem_norm', 'new_v_w_xq', 'new_v_w_xk', 'new_v_w_xv', 'new_v_w_xo', 'new_v_ffn_norm', 'new_v_w_ffn_up', 'new_v_ffn_conv', 'new_v_w_ffn_down', 'new_v_final_norm']
TWIN_LEAF_KINDS = {'loss': 'loss', 'grad_x': 'grad_x', 'grad_mix_norm': 'grad_w', 'grad_w_mix_in': 'grad_w', 'grad_gdn_conv': 'grad_w', 'grad_gdn_a_log': 'grad_w', 'grad_gdn_dt_bias': 'grad_w', 'grad_gdn_out_norm': 'grad_w', 'grad_sc_conv': 'grad_w', 'grad_w_mix_out': 'grad_w', 'grad_xattn_norm': 'grad_w', 'grad_mem_norm': 'grad_w', 'grad_w_xq': 'grad_w', 'grad_w_xk': 'grad_w', 'grad_w_xv': 'grad_w', 'grad_w_xo': 'grad_w', 'grad_ffn_norm': 'grad_w', 'grad_w_ffn_up': 'grad_w', 'grad_ffn_conv': 'grad_w', 'grad_w_ffn_down': 'grad_w', 'grad_final_norm': 'grad_w', 'delta_mix_norm': 'delta_w', 'delta_w_mix_in': 'delta_w', 'delta_gdn_conv': 'delta_w', 'delta_gdn_a_log': 'delta_w', 'delta_gdn_dt_bias': 'delta_w', 'delta_gdn_out_norm': 'delta_w', 'delta_sc_conv': 'delta_w', 'delta_w_mix_out': 'delta_w', 'delta_xattn_norm': 'delta_w', 'delta_mem_norm': 'delta_w', 'delta_w_xq': 'delta_w', 'delta_w_xk': 'delta_w', 'delta_w_xv': 'delta_w', 'delta_w_xo': 'delta_w', 'delta_ffn_norm': 'delta_w', 'delta_w_ffn_up': 'delta_w', 'delta_ffn_conv': 'delta_w', 'delta_w_ffn_down': 'delta_w', 'delta_final_norm': 'delta_w', 'new_m_mix_norm': 'new_m', 'new_m_w_mix_in': 'new_m', 'new_m_gdn_conv': 'new_m', 'new_m_gdn_a_log': 'new_m', 'new_m_gdn_dt_bias': 'new_m', 'new_m_gdn_out_norm': 'new_m', 'new_m_sc_conv': 'new_m', 'new_m_w_mix_out': 'new_m', 'new_m_xattn_norm': 'new_m', 'new_m_mem_norm': 'new_m', 'new_m_w_xq': 'new_m', 'new_m_w_xk': 'new_m', 'new_m_w_xv': 'new_m', 'new_m_w_xo': 'new_m', 'new_m_ffn_norm': 'new_m', 'new_m_w_ffn_up': 'new_m', 'new_m_ffn_conv': 'new_m', 'new_m_w_ffn_down': 'new_m', 'new_m_final_norm': 'new_m', 'new_v_mix_norm': 'new_v', 'new_v_w_mix_in': 'new_v', 'new_v_gdn_conv': 'new_v', 'new_v_gdn_a_log': 'new_v', 'new_v_gdn_dt_bias': 'new_v', 'new_v_gdn_out_norm': 'new_v', 'new_v_sc_conv': 'new_v', 'new_v_w_mix_out': 'new_v', 'new_v_xattn_norm': 'new_v', 'new_v_mem_norm': 'new_v', 'new_v_w_xq': 'new_v', 'new_v_w_xk': 'new_v', 'new_v_w_xv': 'new_v', 'new_v_w_xo': 'new_v', 'new_v_ffn_norm': 'new_v', 'new_v_w_ffn_up': 'new_v', 'new_v_ffn_conv': 'new_v', 'new_v_w_ffn_down': 'new_v', 'new_v_final_norm': 'new_v'}


def _forward(args):
    return _fwd_reference(*[args[k] for k in FWD_PARAMS])


def _output_shape():
    def fwd():
        inp = _fwd_setup_inputs(0)
        return _fwd_reference(*[inp[k] for k in FWD_PARAMS])
    out = _jax.eval_shape(fwd)
    return out.shape, out.dtype

N_MICROBATCH = 1
ADAM_LR = 0.001
ADAM_B1 = 0.9
ADAM_B2 = 0.999
ADAM_EPS = 1e-08
ADAM_WD = 0.01
ADAM_STEP = 10
PER_EXAMPLE_BATCH_AXIS = {'x': 0, 'mem': 0, 'loss_target': 0}
SHARED_INPUTS = []
_WEIGHT_DTYPES = {'mix_norm': _jnp.float32, 'w_mix_in': _jnp.float32, 'gdn_conv': _jnp.float32, 'gdn_a_log': _jnp.float32, 'gdn_dt_bias': _jnp.float32, 'gdn_out_norm': _jnp.float32, 'sc_conv': _jnp.float32, 'w_mix_out': _jnp.float32, 'xattn_norm': _jnp.float32, 'mem_norm': _jnp.float32, 'w_xq': _jnp.float32, 'w_xk': _jnp.float32, 'w_xv': _jnp.float32, 'w_xo': _jnp.float32, 'ffn_norm': _jnp.float32, 'w_ffn_up': _jnp.float32, 'ffn_conv': _jnp.float32, 'w_ffn_down': _jnp.float32, 'final_norm': _jnp.float32}
MOMENT_SCALE = {'mix_norm': 5.214796e-02, 'w_mix_in': 2.747831e-02, 'gdn_conv': 1.673802e-02, 'gdn_a_log': 9.260995e-02, 'gdn_dt_bias': 8.871435e-02, 'gdn_out_norm': 5.934462e-02, 'sc_conv': 3.769390e-02, 'w_mix_out': 1.035817e-01, 'xattn_norm': 4.018066e-03, 'mem_norm': 5.506168e-03, 'w_xq': 3.781861e-03, 'w_xk': 3.786156e-03, 'w_xv': 3.869744e-03, 'w_xo': 1.337100e-02, 'ffn_norm': 3.162926e-02, 'w_ffn_up': 1.340712e-02, 'ffn_conv': 1.360465e-02, 'w_ffn_down': 7.576547e-02, 'final_norm': 3.198566e+01}


def _to_microbatches(a, axis):
    t = _jnp.moveaxis(a, axis, 0)
    t = t.reshape((N_MICROBATCH, t.shape[0] // N_MICROBATCH) + t.shape[1:])
    return _jnp.moveaxis(t, 1, axis + 1)


def setup_inputs(seed: int = 0) -> dict:
    inp = _fwd_setup_inputs(seed)
    key = _jax.random.fold_in(_jax.random.key(seed), 7919)
    shape, _ = _output_shape()
    out = dict(inp)
    out["loss_target"] = _jax.random.normal(_jax.random.fold_in(key, 0), shape, _jnp.float32)
    for i, name in enumerate(TWIN_WEIGHTS):
        w = inp[name].astype(_jnp.float32)
        if MOMENT_SCALE is None:
            s = _jnp.sqrt(_jnp.mean(_jnp.square(w)) + 1e-30)
        else:
            s = MOMENT_SCALE[name]
        km, kv = _jax.random.split(_jax.random.fold_in(key, i + 1))
        out[name] = w
        out["m_" + name] = s * _jax.random.normal(km, w.shape, _jnp.float32)
        out["v_" + name] = (s * s) * _jax.random.uniform(kv, w.shape, _jnp.float32, 0.5, 1.5)
    if N_MICROBATCH > 1:
        for name, axis in PER_EXAMPLE_BATCH_AXIS.items():
            out[name] = _to_microbatches(out[name], axis)
    return {'x': out['x'], 'mem': out['mem'], 'mix_norm': out['mix_norm'], 'w_mix_in': out['w_mix_in'], 'gdn_conv': out['gdn_conv'], 'gdn_a_log': out['gdn_a_log'], 'gdn_dt_bias': out['gdn_dt_bias'], 'gdn_out_norm': out['gdn_out_norm'], 'sc_conv': out['sc_conv'], 'w_mix_out': out['w_mix_out'], 'xattn_norm': out['xattn_norm'], 'mem_norm': out['mem_norm'], 'w_xq': out['w_xq'], 'w_xk': out['w_xk'], 'w_xv': out['w_xv'], 'w_xo': out['w_xo'], 'ffn_norm': out['ffn_norm'], 'w_ffn_up': out['w_ffn_up'], 'ffn_conv': out['ffn_conv'], 'w_ffn_down': out['w_ffn_down'], 'final_norm': out['final_norm'], 'loss_target': out['loss_target'], 'm_mix_norm': out['m_mix_norm'], 'm_w_mix_in': out['m_w_mix_in'], 'm_gdn_conv': out['m_gdn_conv'], 'm_gdn_a_log': out['m_gdn_a_log'], 'm_gdn_dt_bias': out['m_gdn_dt_bias'], 'm_gdn_out_norm': out['m_gdn_out_norm'], 'm_sc_conv': out['m_sc_conv'], 'm_w_mix_out': out['m_w_mix_out'], 'm_xattn_norm': out['m_xattn_norm'], 'm_mem_norm': out['m_mem_norm'], 'm_w_xq': out['m_w_xq'], 'm_w_xk': out['m_w_xk'], 'm_w_xv': out['m_w_xv'], 'm_w_xo': out['m_w_xo'], 'm_ffn_norm': out['m_ffn_norm'], 'm_w_ffn_up': out['m_w_ffn_up'], 'm_ffn_conv': out['m_ffn_conv'], 'm_w_ffn_down': out['m_w_ffn_down'], 'm_final_norm': out['m_final_norm'], 'v_mix_norm': out['v_mix_norm'], 'v_w_mix_in': out['v_w_mix_in'], 'v_gdn_conv': out['v_gdn_conv'], 'v_gdn_a_log': out['v_gdn_a_log'], 'v_gdn_dt_bias': out['v_gdn_dt_bias'], 'v_gdn_out_norm': out['v_gdn_out_norm'], 'v_sc_conv': out['v_sc_conv'], 'v_w_mix_out': out['v_w_mix_out'], 'v_xattn_norm': out['v_xattn_norm'], 'v_mem_norm': out['v_mem_norm'], 'v_w_xq': out['v_w_xq'], 'v_w_xk': out['v_w_xk'], 'v_w_xv': out['v_w_xv'], 'v_w_xo': out['v_w_xo'], 'v_ffn_norm': out['v_ffn_norm'], 'v_w_ffn_up': out['v_w_ffn_up'], 'v_ffn_conv': out['v_ffn_conv'], 'v_w_ffn_down': out['v_w_ffn_down'], 'v_final_norm': out['v_final_norm']}


def _loss(weights, diff, rest, loss_target):
    with _jax.named_scope("forward"):
        args = {**rest, TWIN_DIFF_INPUT: diff, **{k: w.astype(_WEIGHT_DTYPES[k]) for k, w in weights.items()}}
        y = _forward(args)
    with _jax.named_scope("loss_head"):
        err = _jnp.square(y.astype(_jnp.float32) - loss_target)
        return 0.5 * _jnp.sum(_jnp.mean(err, axis=-1)) if err.ndim else 0.5 * err


def _adamw(w, g, m, v):
    m = ADAM_B1 * m + (1.0 - ADAM_B1) * g
    v = ADAM_B2 * v + (1.0 - ADAM_B2) * _jnp.square(g)
    m_hat = m / (1.0 - ADAM_B1 ** ADAM_STEP)
    v_hat = v / (1.0 - ADAM_B2 ** ADAM_STEP)
    delta = -ADAM_LR * (m_hat / (_jnp.sqrt(v_hat) + ADAM_EPS) + ADAM_WD * w)
    return delta, m, v


def reference(x, mem, mix_norm, w_mix_in, gdn_conv, gdn_a_log, gdn_dt_bias, gdn_out_norm, sc_conv, w_mix_out, xattn_norm, mem_norm, w_xq, w_xk, w_xv, w_xo, ffn_norm, w_ffn_up, ffn_conv, w_ffn_down, final_norm, loss_target, m_mix_norm, m_w_mix_in, m_gdn_conv, m_gdn_a_log, m_gdn_dt_bias, m_gdn_out_norm, m_sc_conv, m_w_mix_out, m_xattn_norm, m_mem_norm, m_w_xq, m_w_xk, m_w_xv, m_w_xo, m_ffn_norm, m_w_ffn_up, m_ffn_conv, m_w_ffn_down, m_final_norm, v_mix_norm, v_w_mix_in, v_gdn_conv, v_gdn_a_log, v_gdn_dt_bias, v_gdn_out_norm, v_sc_conv, v_w_mix_out, v_xattn_norm, v_mem_norm, v_w_xq, v_w_xk, v_w_xv, v_w_xo, v_ffn_norm, v_w_ffn_up, v_ffn_conv, v_w_ffn_down, v_final_norm):
    given = dict(x=x, mem=mem, mix_norm=mix_norm, w_mix_in=w_mix_in, gdn_conv=gdn_conv, gdn_a_log=gdn_a_log, gdn_dt_bias=gdn_dt_bias, gdn_out_norm=gdn_out_norm, sc_conv=sc_conv, w_mix_out=w_mix_out, xattn_norm=xattn_norm, mem_norm=mem_norm, w_xq=w_xq, w_xk=w_xk, w_xv=w_xv, w_xo=w_xo, ffn_norm=ffn_norm, w_ffn_up=w_ffn_up, ffn_conv=ffn_conv, w_ffn_down=w_ffn_down, final_norm=final_norm, loss_target=loss_target, m_mix_norm=m_mix_norm, m_w_mix_in=m_w_mix_in, m_gdn_conv=m_gdn_conv, m_gdn_a_log=m_gdn_a_log, m_gdn_dt_bias=m_gdn_dt_bias, m_gdn_out_norm=m_gdn_out_norm, m_sc_conv=m_sc_conv, m_w_mix_out=m_w_mix_out, m_xattn_norm=m_xattn_norm, m_mem_norm=m_mem_norm, m_w_xq=m_w_xq, m_w_xk=m_w_xk, m_w_xv=m_w_xv, m_w_xo=m_w_xo, m_ffn_norm=m_ffn_norm, m_w_ffn_up=m_w_ffn_up, m_ffn_conv=m_ffn_conv, m_w_ffn_down=m_w_ffn_down, m_final_norm=m_final_norm, v_mix_norm=v_mix_norm, v_w_mix_in=v_w_mix_in, v_gdn_conv=v_gdn_conv, v_gdn_a_log=v_gdn_a_log, v_gdn_dt_bias=v_gdn_dt_bias, v_gdn_out_norm=v_gdn_out_norm, v_sc_conv=v_sc_conv, v_w_mix_out=v_w_mix_out, v_xattn_norm=v_xattn_norm, v_mem_norm=v_mem_norm, v_w_xq=v_w_xq, v_w_xk=v_w_xk, v_w_xv=v_w_xv, v_w_xo=v_w_xo, v_ffn_norm=v_ffn_norm, v_w_ffn_up=v_w_ffn_up, v_ffn_conv=v_ffn_conv, v_w_ffn_down=v_w_ffn_down, v_final_norm=v_final_norm)
    weights = {n: given[n] for n in TWIN_WEIGHTS}
    shared = {n: given[n] for n in SHARED_INPUTS}
    per_example = {n: given[n] for n in ['x', 'mem']}
    grad_fn = _jax.value_and_grad(_loss, argnums=(0, 1))

    def one_microbatch(ex, loss_target):
        ex = dict(ex)
        diff = ex.pop(TWIN_DIFF_INPUT)
        return grad_fn(weights, diff, {**shared, **ex}, loss_target)

    if N_MICROBATCH == 1:
        loss, (grad_w, grad_x) = one_microbatch(per_example, given["loss_target"])
    else:
        def body(carry, xs):
            loss_sum, grad_sum = carry
            l_k, (gw_k, gx_k) = one_microbatch(xs[0], xs[1])
            with _jax.named_scope("update"):
                return (loss_sum + l_k, _jax.tree.map(_jnp.add, grad_sum, gw_k)), gx_k

        init = (_jnp.zeros((), _jnp.float32), _jax.tree.map(_jnp.zeros_like, weights))
        (loss, grad_w), grad_x = _jax.lax.scan(body, init, (per_example, given["loss_target"]))
    with _jax.named_scope("update"):
        delta_w, new_m, new_v = {}, {}, {}
        for n in TWIN_WEIGHTS:
            delta_w[n], new_m[n], new_v[n] = _adamw(weights[n], grad_w[n], given["m_" + n], given["v_" + n])
    return (loss, grad_x, *[grad_w[n] for n in TWIN_WEIGHTS], *[delta_w[n] for n in TWIN_WEIGHTS],
            *[new_m[n] for n in TWIN_WEIGHTS], *[new_v[n] for n in TWIN_WEIGHTS])
```

```python
import functools

import jax
import jax.numpy as jnp
from jax import lax
from jax.experimental import pallas as pl
from jax.experimental.pallas import tpu as pltpu

F32 = jnp.float32
BF16 = jnp.bfloat16
CHUNK = 64
HEAD = 128
XHEADS = 4
GDN_K = 4
EPS = 1e-6
HALO = 16
N_DEV = 8
VMEM_LIMIT = 56 * 1024 * 1024
ADAM_LR, ADAM_B1, ADAM_B2, ADAM_EPS, ADAM_WD, ADAM_STEP = 0.001, 0.9, 0.999, 1e-08, 0.01, 10
MESH = pl.DeviceIdType.MESH


def _pc(body, **kw):
    return pl.pallas_call(body, **kw)


def _cp(sem=None, **kw):
    if sem is not None:
        kw["dimension_semantics"] = sem
    return pltpu.CompilerParams(vmem_limit_bytes=VMEM_LIMIT, **kw)


def _tile(n, cands):
    for c in cands:
        if n % c == 0:
            return c
    return n


def _sigmoid(x):
    return 1.0 / (1.0 + jnp.exp(-x))


def _softplus(x):
    return jnp.maximum(x, 0.0) + jnp.log(1.0 + jnp.exp(-jnp.abs(x)))


def _mmb(a, b):
    return jnp.dot(a.astype(BF16), b.astype(BF16), preferred_element_type=F32)


def _mmb_nt(a, b):
    return lax.dot_general(a.astype(BF16), b.astype(BF16), (((1,), (1,)), ((), ())), preferred_element_type=F32)


def _mmb_tn(a, b):
    return lax.dot_general(a.astype(BF16), b.astype(BF16), (((0,), (0,)), ((), ())), preferred_element_type=F32)


def _mmh(a, b):
    return jnp.dot(a, b, preferred_element_type=F32, precision=lax.Precision.HIGHEST)


def _shift_down(cur, prev, s):
    if s == 0:
        return cur
    r = pltpu.roll(cur, s, 0)
    p = pltpu.roll(prev, s, 0)
    rows = lax.broadcasted_iota(jnp.int32, prev.shape, 0)
    first = jnp.where(rows < s, p, r[:HALO])
    return jnp.concatenate([first, r[HALO:]], axis=0)


def _shift_up(ext, s, tm):
    if s == 0:
        return ext[:tm]
    return pltpu.roll(ext, ext.shape[0] - s, 0)[:tm]


def _prev_map(tm, col):
    return lambda i, j: (jnp.maximum(i * (tm // HALO) - 1, 0), col(j))


def _next_map(tm, nrows, col):
    return lambda i, j: (jnp.minimum((i + 1) * (tm // HALO), nrows // HALO - 1), col(j))


def _mm(a, b, *, ta=False, tb=False, out_dtype=F32, res=None, name):
    m, k = (a.shape[1], a.shape[0]) if ta else a.shape
    n = b.shape[0] if tb else b.shape[1]
    assert (b.shape[1] if tb else b.shape[0]) == k, (a.shape, b.shape, ta, tb)
    tm = _tile(m, (1024, 512, 256, 128))
    tn = _tile(n, (1024, 512, 256, 128))
    tk = _tile(k, (512, 256, 128))
    nk = k // tk
    dn = (((0 if ta else 1,), (1 if tb else 0,)), ((), ()))

    def body(a_ref, b_ref, *rest):
        if res is None:
            o_ref, acc = rest
        else:
            r_ref, o_ref, acc = rest
        kk = pl.program_id(2)

        @pl.when(kk == 0)
        def _():
            acc[...] = jnp.zeros_like(acc)

        acc[...] += lax.dot_general(a_ref[...].astype(BF16), b_ref[...].astype(BF16), dn,
                                    preferred_element_type=F32)

        @pl.when(kk == nk - 1)
        def _():
            if res is None:
                o_ref[...] = acc[...].astype(out_dtype)
            else:
                o_ref[...] = (acc[...] + r_ref[...].astype(F32)).astype(out_dtype)

    a_spec = pl.BlockSpec((tk, tm), lambda i, j, kk: (kk, i)) if ta else pl.BlockSpec((tm, tk), lambda i, j, kk: (i, kk))
    b_spec = pl.BlockSpec((tn, tk), lambda i, j, kk: (j, kk)) if tb else pl.BlockSpec((tk, tn), lambda i, j, kk: (kk, j))
    o_spec = pl.BlockSpec((tm, tn), lambda i, j, kk: (i, j))
    in_specs = [a_spec, b_spec] + ([o_spec] if res is not None else [])
    args = (a, b) + ((res,) if res is not None else ())
    return _pc(body, name=name, grid=(m // tm, n // tn, nk), in_specs=in_specs, out_specs=o_spec,
               out_shape=jax.ShapeDtypeStruct((m, n), out_dtype), scratch_shapes=[pltpu.VMEM((tm, tn), F32)],
               compiler_params=_cp(("parallel", "parallel", "arbitrary")))(*args)


def _rms_fwd(x, w, name):
    s, d = x.shape
    tm = _tile(s, (512, 256, 128, 64))

    def body(x_ref, w_ref, o_ref):
        xv = x_ref[...]
        r = lax.rsqrt(jnp.mean(xv * xv, axis=-1, keepdims=True) + EPS)
        o_ref[...] = (xv * r * w_ref[...]).astype(BF16)

    return _pc(body, name=name, grid=(s // tm,),
               in_specs=[pl.BlockSpec((tm, d), lambda i: (i, 0)), pl.BlockSpec((1, d), lambda i: (0, 0))],
               out_specs=pl.BlockSpec((tm, d), lambda i: (i, 0)), out_shape=jax.ShapeDtypeStruct((s, d), BF16),
               compiler_params=_cp(("parallel",)))(x, w.reshape(1, d))


def _rms_bwd(x, dh, w, dx_in, name):
    s, d = x.shape
    tm = _tile(s, (256, 128, 64))

    def body(x_ref, dh_ref, w_ref, dxi_ref, dx_ref, dg_ref):
        @pl.when(pl.program_id(0) == 0)
        def _():
            dg_ref[...] = jnp.zeros_like(dg_ref)

        xv = x_ref[...]
        dy = dh_ref[...].astype(F32)
        r = lax.rsqrt(jnp.mean(xv * xv, axis=-1, keepdims=True) + EPS)
        xh = xv * r
        dxh = dy * w_ref[...]
        dx_ref[...] = dxi_ref[...] + r * (dxh - xh * jnp.mean(dxh * xh, axis=-1, keepdims=True))
        dg_ref[0:1, :] += jnp.sum(dy * xh, axis=0, keepdims=True)

    row = pl.BlockSpec((tm, d), lambda i: (i, 0))
    dx, dg = _pc(body, name=name, grid=(s // tm,),
                 in_specs=[row, row, pl.BlockSpec((1, d), lambda i: (0, 0)), row],
                 out_specs=[row, pl.BlockSpec((8, d), lambda i: (0, 0))],
                 out_shape=[jax.ShapeDtypeStruct((s, d), F32), jax.ShapeDtypeStruct((8, d), F32)],
                 compiler_params=_cp(("arbitrary",)))(x, dh, w.reshape(1, d), dx_in)
    return dx, dg[0]


def _loss_head(x, w, target):
    s, d = x.shape
    tm = _tile(s, (256, 128, 64))

    def body(x_ref, w_ref, t_ref, dx_ref, dg_ref, l_ref):
        @pl.when(pl.program_id(0) == 0)
        def _():
            dg_ref[...] = jnp.zeros_like(dg_ref)
            l_ref[...] = jnp.zeros_like(l_ref)

        xv = x_ref[...]
        r = lax.rsqrt(jnp.mean(xv * xv, axis=-1, keepdims=True) + EPS)
        xh = xv * r
        err = xh * w_ref[...] - t_ref[...]
        l_ref[...] += 0.5 * jnp.sum(jnp.mean(err * err, axis=-1, keepdims=True), axis=0, keepdims=True)
        dy = err * (1.0 / d)
        dxh = dy * w_ref[...]
        dx_ref[...] = r * (dxh - xh * jnp.mean(dxh * xh, axis=-1, keepdims=True))
        dg_ref[0:1, :] += jnp.sum(dy * xh, axis=0, keepdims=True)

    row = pl.BlockSpec((tm, d), lambda i: (i, 0))
    dx, dg, ls = _pc(body, name="loss_head", grid=(s // tm,),
                     in_specs=[row, pl.BlockSpec((1, d), lambda i: (0, 0)), row],
                     out_specs=[row, pl.BlockSpec((8, d), lambda i: (0, 0)), pl.BlockSpec((8, 128), lambda i: (0, 0))],
                     out_shape=[jax.ShapeDtypeStruct((s, d), F32), jax.ShapeDtypeStruct((8, d), F32),
                                jax.ShapeDtypeStruct((8, 128), F32)],
                     compiler_params=_cp(("arbitrary",)))(x, w.reshape(1, d), target)
    return ls[0, 0], dx, dg[0]


def _gdn_prep_fwd(proj, conv_w, width):
    s = proj.shape[0]
    tm = _tile(s, (256, 128, 64))
    nh = width // HEAD

    def body(c_ref, p_ref, w_ref, o_ref):
        i, seg = pl.program_id(0), pl.program_id(1)
        cur = c_ref[...].astype(F32)
        prev = jnp.where(i > 0, p_ref[...].astype(F32), 0.0)
        pre = cur * w_ref[GDN_K - 1:GDN_K, :]
        for j in range(GDN_K - 1):
            pre = pre + _shift_down(cur, prev, GDN_K - 1 - j) * w_ref[j:j + 1, :]
        act = pre * _sigmoid(pre)
        scale = jnp.where(seg == 0, HEAD ** -0.5, 1.0)
        for h in range(nh):
            a = act[:, h * HEAD:(h + 1) * HEAD]
            rs = lax.rsqrt(jnp.sum(a * a, axis=-1, keepdims=True) + EPS) * scale
            o_ref[:, h * HEAD:(h + 1) * HEAD] = a * jnp.where(seg < 2, rs, 1.0)

    return _pc(body, name="gdn_prep_fwd", grid=(s // tm, 3),
               in_specs=[pl.BlockSpec((tm, width), lambda i, j: (i, j)),
                         pl.BlockSpec((HALO, width), _prev_map(tm, lambda j: j)),
                         pl.BlockSpec((GDN_K, width), lambda i, j: (0, j))],
               out_specs=pl.BlockSpec((tm, width), lambda i, j: (i, j)),
               out_shape=jax.ShapeDtypeStruct((s, 3 * width), F32),
               compiler_params=_cp(("parallel", "parallel")))(proj, proj, conv_w)


def _gdn_prep_bwd(dqkv, proj, conv_w, width):
    s = proj.shape[0]
    tm = _tile(s, (256, 128, 64))
    nh = width // HEAD
    nt = s // tm

    def body(c_ref, p_ref, n_ref, d_ref, dn_ref, w_ref, o_ref, dw_ref):
        seg, i = pl.program_id(0), pl.program_id(1)

        @pl.when(i == 0)
        def _():
            dw_ref[...] = jnp.zeros_like(dw_ref)

        ext = jnp.concatenate([c_ref[...].astype(F32), n_ref[...].astype(F32)], axis=0)
        prev = jnp.where(i > 0, p_ref[...].astype(F32), 0.0)
        sh = [_shift_down(ext, prev, GDN_K - 1 - j) for j in range(GDN_K)]
        pre = sh[0] * w_ref[0:1, :]
        for j in range(1, GDN_K):
            pre = pre + sh[j] * w_ref[j:j + 1, :]
        sg = _sigmoid(pre)
        act = pre * sg
        dout = jnp.concatenate([d_ref[...], dn_ref[...]], axis=0)
        rows = lax.broadcasted_iota(jnp.int32, (tm + HALO, 1), 0)
        dout = jnp.where((rows < tm) | (i < nt - 1), dout, 0.0)
        scale = jnp.where(seg == 0, HEAD ** -0.5, 1.0)
        parts = []
        for h in range(nh):
            a = act[:, h * HEAD:(h + 1) * HEAD]
            dq = dout[:, h * HEAD:(h + 1) * HEAD]
            rs = lax.rsqrt(jnp.sum(a * a, axis=-1, keepdims=True) + EPS)
            nrm = a * rs
            dn = dq * scale
            da_norm = rs * (dn - nrm * jnp.sum(dn * nrm, axis=-1, keepdims=True))
            parts.append(jnp.where(seg < 2, da_norm, dq))
        dact = jnp.concatenate(parts, axis=1)
        dpre = dact * (sg * (1.0 + pre * (1.0 - sg)))
        dp = _shift_up(dpre, 0, tm) * w_ref[GDN_K - 1:GDN_K, :]
        for j in range(GDN_K - 1):
            dp = dp + _shift_up(dpre, GDN_K - 1 - j, tm) * w_ref[j:j + 1, :]
        o_ref[...] = dp.astype(BF16)
        for j in range(GDN_K):
            dw_ref[j:j + 1, :] += jnp.sum(dpre[:tm] * sh[j][:tm], axis=0, keepdims=True)

    dproj, dw = _pc(body, name="gdn_prep_bwd", grid=(3, nt),
                    in_specs=[pl.BlockSpec((tm, width), lambda j, i: (i, j)),
                              pl.BlockSpec((HALO, width), lambda j, i: _prev_map(tm, lambda c: c)(i, j)),
                              pl.BlockSpec((HALO, width), lambda j, i: _next_map(tm, s, lambda c: c)(i, j)),
                              pl.BlockSpec((tm, width), lambda j, i: (i, j)),
                              pl.BlockSpec((HALO, width), lambda j, i: _next_map(tm, s, lambda c: c)(i, j)),
                              pl.BlockSpec((GDN_K, width), lambda j, i: (0, j))],
                    out_specs=[pl.BlockSpec((tm, width), lambda j, i: (i, j)),
                               pl.BlockSpec((8, width), lambda j, i: (0, j))],
                    out_shape=[jax.ShapeDtypeStruct((s, 3 * width), BF16), jax.ShapeDtypeStruct((8, 3 * width), F32)],
                    compiler_params=_cp(("parallel", "arbitrary")))(proj, proj, proj, dqkv, dqkv, conv_w)
    return dproj, dw[:GDN_K]


def _chunk_common(bav, gp_ref, nh):
    g_full = -jnp.exp(gp_ref[0:1, :]) * _softplus(bav + gp_ref[1:2, :])
    beta_full = _sigmoid(bav)
    ri = lax.broadcasted_iota(jnp.int32, (CHUNK, CHUNK), 0)
    ci = lax.broadcasted_iota(jnp.int32, (CHUNK, CHUNK), 1)
    gc_full = _mmh((ri >= ci).astype(F32), g_full)
    gc_t = gc_full.T
    return beta_full, gc_full, gc_t, ri, ci


def _head_gates(h, nh, beta_full, gc_full, gc_t, ri, ci):
    bcol = beta_full[:, h:h + 1]
    gcol = gc_full[:, nh + h:nh + h + 1]
    grow = gc_t[nh + h:nh + h + 1, :]
    dec = jnp.exp(jnp.where(ri >= ci, gcol - grow, -1e30))
    ecol = jnp.exp(gcol)
    gl = gcol[CHUNK - 1:CHUNK, :]
    return bcol, gcol, dec, ecol, gl


def _gdn_fwd(qkv, ba, gp, width):
    s = qkv.shape[0]
    nh = width // HEAD
    nc = s // CHUNK

    def body(q_ref, k_ref, v_ref, ba_ref, gp_ref, o_ref, st_ref, t_ref, state):
        @pl.when(pl.program_id(0) == 0)
        def _():
            state[...] = jnp.zeros_like(state)

        beta_full, gc_full, gc_t, ri, ci = _chunk_common(ba_ref[...], gp_ref, nh)
        eye = (ri == ci).astype(F32)
        for h in range(nh):
            sl = slice(h * HEAD, (h + 1) * HEAD)
            q, k, v = q_ref[:, sl], k_ref[:, sl], v_ref[:, sl]
            bcol, gcol, dec, ecol, gl = _head_gates(h, nh, beta_full, gc_full, gc_t, ri, ci)
            kb = k * bcol
            a = jnp.where(ri > ci, _mmb_nt(kb, k) * dec, 0.0)
            t = eye - a
            pw = _mmh(a, a)
            for it in range(5):
                t = t + _mmh(t, pw)
                if it < 4:
                    pw = _mmh(pw, pw)
            uw = _mmb(t, jnp.concatenate([v * bcol, kb * ecol], axis=1))
            u, w = uw[:, :HEAD], uw[:, HEAD:]
            attn = jnp.where(ri >= ci, _mmb_nt(q, k) * dec, 0.0)
            st = state[h]
            vn = u - _mmb(w, st)
            o_ref[:, sl] = _mmb(q * ecol, st) + _mmb(attn, vn)
            st_ref[0, h] = st
            t_ref[0, h] = t
            state[h] = st * jnp.exp(gl) + _mmb_tn(k * jnp.exp(gl - gcol), vn)

    blk = lambda c: pl.BlockSpec((CHUNK, width), lambda n, c=c: (n, c))
    return _pc(body, name="gdn_fwd", grid=(nc,),
               in_specs=[blk(0), blk(1), blk(2), pl.BlockSpec((CHUNK, 128), lambda n: (n, 0)),
                         pl.BlockSpec((8, 128), lambda n: (0, 0))],
               out_specs=[blk(0), pl.BlockSpec((1, nh, HEAD, HEAD), lambda n: (n, 0, 0, 0)),
                          pl.BlockSpec((1, nh, CHUNK, CHUNK), lambda n: (n, 0, 0, 0))],
               out_shape=[jax.ShapeDtypeStruct((s, width), F32), jax.ShapeDtypeStruct((nc, nh, HEAD, HEAD), F32),
                          jax.ShapeDtypeStruct((nc, nh, CHUNK, CHUNK), F32)],
               scratch_shapes=[pltpu.VMEM((nh, HEAD, HEAD), F32)],
               compiler_params=_cp(("arbitrary",)))(qkv, qkv, qkv, ba, gp)


def _gdn_bwd(qkv, ba, gp, do, states, tinv, width):
    s = qkv.shape[0]
    nh = width // HEAD
    nc = s // CHUNK

    def body(q_ref, k_ref, v_ref, ba_ref, gp_ref, do_ref, st_ref, t_ref, dqkv_ref, dgb_ref, dstate):
        @pl.when(pl.program_id(0) == 0)
        def _():
            dstate[...] = jnp.zeros_like(dstate)

        beta_full, gc_full, gc_t, ri, ci = _chunk_common(ba_ref[...], gp_ref, nh)
        lane = lax.broadcasted_iota(jnp.int32, (CHUNK, 128), 1)
        rowi = lax.broadcasted_iota(jnp.int32, (CHUNK, 1), 0)
        dbeta_full = jnp.zeros((CHUNK, 128), F32)
        dgc_full = jnp.zeros((CHUNK, 128), F32)
        for h in range(nh):
            sl = slice(h * HEAD, (h + 1) * HEAD)
            q, k, v, dout = q_ref[:, sl], k_ref[:, sl], v_ref[:, sl], do_ref[:, sl]
            bcol, gcol, dec, ecol, gl = _head_gates(h, nh, beta_full, gc_full, gc_t, ri, ci)
            st, t, dsp = st_ref[0, h], t_ref[0, h], dstate[h]
            el = jnp.exp(gl)
            kdsc = jnp.exp(gl - gcol)
            kb = k * bcol
            a = jnp.where(ri > ci, _mmb_nt(kb, k) * dec, 0.0)
            attn = jnp.where(ri >= ci, _mmb_nt(q, k) * dec, 0.0)
            uw = _mmb(t, jnp.concatenate([v * bcol, kb * ecol], axis=1))
            u, w = uw[:, :HEAD], uw[:, HEAD:]
            qd = q * ecol
            kd = k * kdsc
            vn = u - _mmb(w, st)
            d_attn = jnp.where(ri >= ci, _mmb_nt(dout, vn), 0.0)
            d_vn = _mmb_tn(attn, dout) + _mmb(kd, dsp)
            d_qd = _mmb_nt(dout, st)
            d_kd = _mmb_nt(vn, dsp)
            d_el = jnp.sum(jnp.sum(st * dsp, axis=1, keepdims=True), axis=0, keepdims=True)
            dstate[h] = _mmb_tn(qd, dout) + el * dsp - _mmb_tn(w, d_vn)
            d_w = -_mmb_nt(d_vn, st)
            dr = _mmb_tn(t, jnp.concatenate([d_vn, d_w], axis=1))
            dru, drw = dr[:, :HEAD], dr[:, HEAD:]
            d_a = -jnp.where(ri > ci, _mmb_nt(dr, uw), 0.0)
            d_kk = d_a * dec
            d_qk = d_attn * dec
            d_kb = _mmb(d_kk, k) + drw * ecol
            dk = _mmb_tn(d_kk, kb) + _mmb_tn(d_qk, q) + d_kb * bcol + d_kd * kdsc
            dq = _mmb(d_qk, k) + d_qd * ecol
            dv = dru * bcol
            dbeta = jnp.sum(dru * v + d_kb * k, axis=1, keepdims=True)
            de = jnp.sum(drw * kb + d_qd * q, axis=1, keepdims=True)
            r = jnp.sum(d_kd * k, axis=1, keepdims=True) * kdsc
            mm = d_a * a + d_attn * attn
            d_gl = jnp.sum(r, axis=0, keepdims=True) + d_el * el
            d_gc = (de * ecol - r + jnp.sum(mm, axis=1, keepdims=True) - jnp.sum(mm.T, axis=1, keepdims=True)
                    + jnp.where(rowi == CHUNK - 1, d_gl, 0.0))
            dqkv_ref[:, sl] = dq
            dqkv_ref[:, width + h * HEAD:width + (h + 1) * HEAD] = dk
            dqkv_ref[:, 2 * width + h * HEAD:2 * width + (h + 1) * HEAD] = dv
            dbeta_full = dbeta_full + jnp.where(lane == h, dbeta, 0.0)
            dgc_full = dgc_full + jnp.where(lane == nh + h, d_gc, 0.0)
        dgb_ref[...] = dbeta_full + _mmh((ri <= ci).astype(F32), dgc_full)

    rev = lambda c: pl.BlockSpec((CHUNK, width), lambda n, c=c: (nc - 1 - n, c))
    return _pc(body, name="gdn_bwd", grid=(nc,),
               in_specs=[rev(0), rev(1), rev(2), pl.BlockSpec((CHUNK, 128), lambda n: (nc - 1 - n, 0)),
                         pl.BlockSpec((8, 128), lambda n: (0, 0)), rev(0),
                         pl.BlockSpec((1, nh, HEAD, HEAD), lambda n: (nc - 1 - n, 0, 0, 0)),
                         pl.BlockSpec((1, nh, CHUNK, CHUNK), lambda n: (nc - 1 - n, 0, 0, 0))],
               out_specs=[pl.BlockSpec((CHUNK, 3 * width), lambda n: (nc - 1 - n, 0)),
                          pl.BlockSpec((CHUNK, 128), lambda n: (nc - 1 - n, 0))],
               out_shape=[jax.ShapeDtypeStruct((s, 3 * width), F32), jax.ShapeDtypeStruct((s, 128), F32)],
               scratch_shapes=[pltpu.VMEM((nh, HEAD, HEAD), F32)],
               compiler_params=_cp(("arbitrary",)))(qkv, qkv, qkv, ba, gp, do, states, tinv)


def _gates_bwd(ba, dgb, gp, nh):
    s = ba.shape[0]
    tm = _tile(s, (512, 256, 128, 64))

    def body(ba_ref, d_ref, gp_ref, o_ref, dp_ref):
        @pl.when(pl.program_id(0) == 0)
        def _():
            dp_ref[...] = jnp.zeros_like(dp_ref)

        bav, dv = ba_ref[...], d_ref[...]
        lane = lax.broadcasted_iota(jnp.int32, bav.shape, 1)
        beta = _sigmoid(bav)
        amp = jnp.exp(gp_ref[0:1, :])
        z = bav + gp_ref[1:2, :]
        d_a = dv * (-amp) * _sigmoid(z)
        d_b = dv * beta * (1.0 - beta)
        is_a = (lane >= nh) & (lane < 2 * nh)
        o_ref[...] = jnp.where(lane < nh, d_b, jnp.where(is_a, d_a, 0.0))
        dp_ref[0:1, :] += jnp.sum(jnp.where(is_a, dv * (-amp) * _softplus(z), 0.0), axis=0, keepdims=True)
        dp_ref[1:2, :] += jnp.sum(jnp.where(is_a, d_a, 0.0), axis=0, keepdims=True)

    row = pl.BlockSpec((tm, 128), lambda i: (i, 0))
    par = pl.BlockSpec((8, 128), lambda i: (0, 0))
    return _pc(body, name="gates_bwd", grid=(s // tm,), in_specs=[row, row, par], out_specs=[row, par],
               out_shape=[jax.ShapeDtypeStruct((s, 128), F32), jax.ShapeDtypeStruct((8, 128), F32)],
               compiler_params=_cp(("arbitrary",)))(ba, dgb, gp)


def _mix_post_fwd(o_raw, proj, gain, sc_w, width):
    s = o_raw.shape[0]
    tm = _tile(s, (256, 128, 64))
    nh = width // HEAD
    ksc = sc_w.shape[0]

    def body(o_ref, z_ref, b_ref, c_ref, h_ref, cp_ref, hp_ref, g_ref, w_ref, y_ref):
        i = pl.program_id(0)
        z = z_ref[...].astype(F32)
        sz = z * _sigmoid(z)
        for h in range(nh):
            sl = slice(h * HEAD, (h + 1) * HEAD)
            o = o_ref[:, sl]
            r = lax.rsqrt(jnp.mean(o * o, axis=-1, keepdims=True) + EPS)
            y_ref[:, sl] = (o * r * g_ref[...] * sz[:, sl]).astype(BF16)
        prod = c_ref[...].astype(F32) * h_ref[...].astype(F32)
        pprev = jnp.where(i > 0, cp_ref[...].astype(F32) * hp_ref[...].astype(F32), 0.0)
        cv = prod * w_ref[ksc - 1:ksc, :]
        for j in range(ksc - 1):
            cv = cv + _shift_down(prod, pprev, ksc - 1 - j) * w_ref[j:j + 1, :]
        y_ref[:, width:] = (b_ref[...].astype(F32) * cv).astype(BF16)

    col = lambda c: pl.BlockSpec((tm, width), lambda i, c=c: (i, c))
    prv = lambda c: pl.BlockSpec((HALO, width), lambda i, c=c: (jnp.maximum(i * (tm // HALO) - 1, 0), c))
    return _pc(body, name="mix_post_fwd", grid=(s // tm,),
               in_specs=[col(0), col(3), col(4), col(5), col(6), prv(5), prv(6),
                         pl.BlockSpec((1, HEAD), lambda i: (0, 0)), pl.BlockSpec((ksc, width), lambda i: (0, 0))],
               out_specs=pl.BlockSpec((tm, 2 * width), lambda i: (i, 0)),
               out_shape=jax.ShapeDtypeStruct((s, 2 * width), BF16),
               compiler_params=_cp(("parallel",)))(o_raw, proj, proj, proj, proj, proj, proj,
                                                    gain.reshape(1, HEAD), sc_w)


def _mix_post_bwd(dy, o_raw, proj, gain, sc_w, width):
    s = o_raw.shape[0]
    tm = _tile(s, (256, 128, 64))
    nt = s // tm
    nh = width // HEAD
    ksc = sc_w.shape[0]

    def body(dyg_ref, dys_ref, dysn_ref, o_ref, z_ref, b_ref, bn_ref, c_ref, h_ref, cp_ref, hp_ref, g_ref, w_ref,
             do_ref, dp_ref, dg_ref, dw_ref):
        i = pl.program_id(0)

        @pl.when(i == 0)
        def _():
            dg_ref[...] = jnp.zeros_like(dg_ref)
            dw_ref[...] = jnp.zeros_like(dw_ref)

        z = z_ref[...].astype(F32)
        sg = _sigmoid(z)
        sz = z * sg
        dsz = sg * (1.0 + z * (1.0 - sg))
        dyg = dyg_ref[...].astype(F32)
        dgain = jnp.zeros((1, HEAD), F32)
        for h in range(nh):
            sl = slice(h * HEAD, (h + 1) * HEAD)
            o = o_ref[:, sl]
            r = lax.rsqrt(jnp.mean(o * o, axis=-1, keepdims=True) + EPS)
            oh = o * r
            d_yn = dyg[:, sl] * sz[:, sl]
            dp_ref[:, sl] = (dyg[:, sl] * oh * g_ref[...] * dsz[:, sl]).astype(BF16)
            dgain = dgain + jnp.sum(d_yn * oh, axis=0, keepdims=True)
            doh = d_yn * g_ref[...]
            do_ref[:, sl] = r * (doh - oh * jnp.mean(doh * oh, axis=-1, keepdims=True))
        dg_ref[0:1, :] += dgain
        cc, hh = c_ref[...].astype(F32), h_ref[...].astype(F32)
        prod = cc * hh
        pprev = jnp.where(i > 0, cp_ref[...].astype(F32) * hp_ref[...].astype(F32), 0.0)
        sh = [_shift_down(prod, pprev, ksc - 1 - j) for j in range(ksc)]
        cv = sh[0] * w_ref[0:1, :]
        for j in range(1, ksc):
            cv = cv + sh[j] * w_ref[j:j + 1, :]
        dys = dys_ref[...].astype(F32)
        dp_ref[:, width:2 * width] = (dys * cv).astype(BF16)
        dcv_n = jnp.where(i < nt - 1, dysn_ref[...].astype(F32) * bn_ref[...].astype(F32), 0.0)
        dcv = jnp.concatenate([dys * b_ref[...].astype(F32), dcv_n], axis=0)
        dprod = dcv[:tm] * w_ref[ksc - 1:ksc, :]
        for j in range(ksc - 1):
            dprod = dprod + _shift_up(dcv, ksc - 1 - j, tm) * w_ref[j:j + 1, :]
        dp_ref[:, 2 * width:3 * width] = (dprod * hh).astype(BF16)
        dp_ref[:, 3 * width:] = (dprod * cc).astype(BF16)
        for j in range(ksc):
            dw_ref[j:j + 1, :] += jnp.sum(dcv[:tm] * sh[j], axis=0, keepdims=True)

    col = lambda c: pl.BlockSpec((tm, width), lambda i, c=c: (i, c))
    prv = lambda c: pl.BlockSpec((HALO, width), lambda i, c=c: (jnp.maximum(i * (tm // HALO) - 1, 0), c))
    nxt = lambda c: pl.BlockSpec((HALO, width), lambda i, c=c: (jnp.minimum((i + 1) * (tm // HALO), s // HALO - 1), c))
    do, dp, dg, dw = _pc(
        body, name="mix_post_bwd", grid=(nt,),
        in_specs=[col(0), col(1), nxt(1), col(0), col(3), col(4), nxt(4), col(5), col(6), prv(5), prv(6),
                  pl.BlockSpec((1, HEAD), lambda i: (0, 0)), pl.BlockSpec((ksc, width), lambda i: (0, 0))],
        out_specs=[col(0), pl.BlockSpec((tm, 4 * width), lambda i: (i, 0)),
                   pl.BlockSpec((8, HEAD), lambda i: (0, 0)), pl.BlockSpec((8, width), lambda i: (0, 0))],
        out_shape=[jax.ShapeDtypeStruct((s, width), F32), jax.ShapeDtypeStruct((s, 4 * width), BF16),
                   jax.ShapeDtypeStruct((8, HEAD), F32), jax.ShapeDtypeStruct((8, width), F32)],
        compiler_params=_cp(("arbitrary",)))(dy, dy, dy, o_raw, proj, proj, proj, proj, proj, proj, proj,
                                             gain.reshape(1, HEAD), sc_w)
    return do, dp, dg[0], dw[:ksc]


def _xattn_fwd(q, k, v):
    s, d = q.shape
    nm = k.shape[0]
    dh = d // XHEADS
    tm = _tile(s, (512, 256, 128, 64))

    def body(q_ref, k_ref, v_ref, o_ref):
        sc = _mmb_nt(q_ref[...], k_ref[...]) * (dh ** -0.5)
        p = jnp.exp(sc - jnp.max(sc, axis=-1, keepdims=True))
        p = p / jnp.sum(p, axis=-1, keepdims=True)
        o_ref[...] = _mmb(p, v_ref[...]).astype(BF16)

    return _pc(body, name="xattn_fwd", grid=(s // tm, XHEADS),
               in_specs=[pl.BlockSpec((tm, dh), lambda i, h: (i, h)), pl.BlockSpec((nm, dh), lambda i, h: (0, h)),
                         pl.BlockSpec((nm, dh), lambda i, h: (0, h))],
               out_specs=pl.BlockSpec((tm, dh), lambda i, h: (i, h)), out_shape=jax.ShapeDtypeStruct((s, d), BF16),
               compiler_params=_cp(("parallel", "parallel")))(q, k, v)


def _xattn_bwd(q, k, v, do):
    s, d = q.shape
    nm = k.shape[0]
    dh = d // XHEADS
    tm = _tile(s, (512, 256, 128, 64))

    def body(q_ref, k_ref, v_ref, do_ref, dq_ref, dk_ref, dv_ref):
        @pl.when(pl.program_id(1) == 0)
        def _():
            dk_ref[...] = jnp.zeros_like(dk_ref)
            dv_ref[...] = jnp.zeros_like(dv_ref)

        scale = dh ** -0.5
        sc = _mmb_nt(q_ref[...], k_ref[...]) * scale
        p = jnp.exp(sc - jnp.max(sc, axis=-1, keepdims=True))
        p = p / jnp.sum(p, axis=-1, keepdims=True)
        dp = _mmb_nt(do_ref[...], v_ref[...])
        ds = p * (dp - jnp.sum(dp * p, axis=-1, keepdims=True)) * scale
        dq_ref[...] = _mmb(ds, k_ref[...]).astype(BF16)
        dk_ref[...] += _mmb_tn(ds, q_ref[...])
        dv_ref[...] += _mmb_tn(p, do_ref[...])

    rowb = pl.BlockSpec((tm, dh), lambda h, i: (i, h))
    memb = pl.BlockSpec((nm, dh), lambda h, i: (0, h))
    return _pc(body, name="xattn_bwd", grid=(XHEADS, s // tm), in_specs=[rowb, memb, memb, rowb],
               out_specs=[rowb, memb, memb],
               out_shape=[jax.ShapeDtypeStruct((s, d), BF16), jax.ShapeDtypeStruct((nm, d), F32),
                          jax.ShapeDtypeStruct((nm, d), F32)],
               compiler_params=_cp(("parallel", "arbitrary")))(q, k, v, do)


def _ffn_act_fwd(u_pre, conv_w):
    s, f2 = u_pre.shape
    f = f2 // 2
    tm = _tile(s, (256, 128, 64))
    cb = _tile(f, (512, 256, 128))
    nf = f // cb
    kf = conv_w.shape[0]

    def body(g_ref, u_ref, gp_ref, up_ref, wg_ref, wu_ref, a_ref):
        i = pl.program_id(0)

        def conv(c_ref, p_ref, w_ref):
            cur = c_ref[...].astype(F32)
            prev = jnp.where(i > 0, p_ref[...].astype(F32), 0.0)
            out = cur * w_ref[kf - 1:kf, :]
            for j in range(kf - 1):
                out = out + _shift_down(cur, prev, kf - 1 - j) * w_ref[j:j + 1, :]
            return out

        gate, up = conv(g_ref, gp_ref, wg_ref), conv(u_ref, up_ref, wu_ref)
        a_ref[...] = (gate * _sigmoid(gate) * up).astype(BF16)

    return _pc(body, name="ffn_act_fwd", grid=(s // tm, nf),
               in_specs=[pl.BlockSpec((tm, cb), lambda i, j: (i, j)), pl.BlockSpec((tm, cb), lambda i, j: (i, j + nf)),
                         pl.BlockSpec((HALO, cb), _prev_map(tm, lambda j: j)),
                         pl.BlockSpec((HALO, cb), _prev_map(tm, lambda j: j + nf)),
                         pl.BlockSpec((kf, cb), lambda i, j: (0, j)), pl.BlockSpec((kf, cb), lambda i, j: (0, j + nf))],
               out_specs=pl.BlockSpec((tm, cb), lambda i, j: (i, j)), out_shape=jax.ShapeDtypeStruct((s, f), BF16),
               compiler_params=_cp(("parallel", "parallel")))(u_pre, u_pre, u_pre, u_pre, conv_w, conv_w)


def _ffn_act_bwd(da, u_pre, conv_w):
    s, f2 = u_pre.shape
    f = f2 // 2
    tm = _tile(s, (256, 128, 64))
    nt = s // tm
    cb = _tile(f, (512, 256, 128))
    nf = f // cb
    kf = conv_w.shape[0]

    def body(da_ref, dan_ref, g_ref, gp_ref, gn_ref, u_ref, up_ref, un_ref, wg_ref, wu_ref, d_ref, dw_ref):
        i = pl.program_id(1)

        @pl.when(i == 0)
        def _():
            dw_ref[...] = jnp.zeros_like(dw_ref)

        def conv(c_ref, p_ref, n_ref, w_ref):
            ext = jnp.concatenate([c_ref[...].astype(F32), n_ref[...].astype(F32)], axis=0)
            prev = jnp.where(i > 0, p_ref[...].astype(F32), 0.0)
            sh = [_shift_down(ext, prev, kf - 1 - j) for j in range(kf)]
            out = sh[0] * w_ref[0:1, :]
            for j in range(1, kf):
                out = out + sh[j] * w_ref[j:j + 1, :]
            return out, sh

        gate, gsh = conv(g_ref, gp_ref, gn_ref, wg_ref)
        up, ush = conv(u_ref, up_ref, un_ref, wu_ref)
        dav = jnp.concatenate([da_ref[...].astype(F32), dan_ref[...].astype(F32)], axis=0)
        rows = lax.broadcasted_iota(jnp.int32, (tm + HALO, 1), 0)
        dav = jnp.where((rows < tm) | (i < nt - 1), dav, 0.0)
        sg = _sigmoid(gate)
        dgate = dav * up * (sg * (1.0 + gate * (1.0 - sg)))
        dup = dav * (gate * sg)

        def conv_t(dv, w_ref):
            out = dv[:tm] * w_ref[kf - 1:kf, :]
            for j in range(kf - 1):
                out = out + _shift_up(dv, kf - 1 - j, tm) * w_ref[j:j + 1, :]
            return out

        d_ref[0] = conv_t(dgate, wg_ref).astype(BF16)
        d_ref[1] = conv_t(dup, wu_ref).astype(BF16)
        for j in range(kf):
            dw_ref[0, j:j + 1, :] += jnp.sum(dgate[:tm] * gsh[j][:tm], axis=0, keepdims=True)
            dw_ref[1, j:j + 1, :] += jnp.sum(dup[:tm] * ush[j][:tm], axis=0, keepdims=True)

    pm = lambda off: (lambda j, i: _prev_map(tm, lambda c: c + off)(i, j))
    nm = lambda off: (lambda j, i: _next_map(tm, s, lambda c: c + off)(i, j))
    du, dw = _pc(
        body, name="ffn_act_bwd", grid=(nf, nt),
        in_specs=[pl.BlockSpec((tm, cb), lambda j, i: (i, j)), pl.BlockSpec((HALO, cb), nm(0)),
                  pl.BlockSpec((tm, cb), lambda j, i: (i, j)), pl.BlockSpec((HALO, cb), pm(0)),
                  pl.BlockSpec((HALO, cb), nm(0)),
                  pl.BlockSpec((tm, cb), lambda j, i: (i, j + nf)), pl.BlockSpec((HALO, cb), pm(nf)),
                  pl.BlockSpec((HALO, cb), nm(nf)),
                  pl.BlockSpec((kf, cb), lambda j, i: (0, j)), pl.BlockSpec((kf, cb), lambda j, i: (0, j + nf))],
        out_specs=[pl.BlockSpec((2, tm, cb), lambda j, i: (0, i, j)), pl.BlockSpec((2, 8, cb), lambda j, i: (0, 0, j))],
        out_shape=[jax.ShapeDtypeStruct((2, s, f), BF16), jax.ShapeDtypeStruct((2, 8, f), F32)],
        compiler_params=_cp(("parallel", "arbitrary")))(da, da, u_pre, u_pre, u_pre, u_pre, u_pre, u_pre, conv_w, conv_w)
    return du, dw[:, :kf]


def _adamw(w, g, m, v, name):
    shape = w.shape
    c = shape[-1]
    r = w.size // c
    tr = r if r * c <= 262144 else _tile(r, tuple(t for t in (512, 256, 128, 64, 32, 16, 8) if t * c <= 262144))
    bc1 = 1.0 - ADAM_B1 ** ADAM_STEP
    bc2 = 1.0 - ADAM_B2 ** ADAM_STEP

    def body(w_ref, g_ref, m_ref, v_ref, d_ref, nm_ref, nv_ref):
        gv = g_ref[...]
        mn = ADAM_B1 * m_ref[...] + (1.0 - ADAM_B1) * gv
        vn = ADAM_B2 * v_ref[...] + (1.0 - ADAM_B2) * (gv * gv)
        nm_ref[...] = mn
        nv_ref[...] = vn
        d_ref[...] = -ADAM_LR * ((mn / bc1) / (jnp.sqrt(vn / bc2) + ADAM_EPS) + ADAM_WD * w_ref[...])

    blk = pl.BlockSpec((tr, c), lambda i: (i, 0))
    outs = _pc(body, name=name, grid=(r // tr,), in_specs=[blk] * 4, out_specs=[blk] * 3,
               out_shape=[jax.ShapeDtypeStruct((r, c), F32)] * 3,
               compiler_params=_cp(("parallel",)))(*(t.reshape(r, c) for t in (w, g, m, v)))
    return tuple(o.reshape(shape) for o in outs)


def _slot_sum(x, name):
    _, r, c = x.shape
    tr = _tile(r, (512, 256, 128, 64, 32, 16, 8))

    def body(x_ref, o_ref):
        acc = x_ref[0].astype(F32)
        for d in range(1, N_DEV):
            acc = acc + x_ref[d].astype(F32)
        o_ref[...] = acc

    return _pc(body, name=name, grid=(r // tr,), in_specs=[pl.BlockSpec((N_DEV, tr, c), lambda i: (0, i, 0))],
               out_specs=pl.BlockSpec((tr, c), lambda i: (i, 0)), out_shape=jax.ShapeDtypeStruct((r, c), F32),
               compiler_params=_cp(("parallel",)))(x)


def _all_gather(x, name):
    nl, r, c = x.shape

    def body(x_ref, out_ref, send_sems, recv_sems, local_sems):
        mx, my, mc = lax.axis_index("x"), lax.axis_index("y"), lax.axis_index("c")
        me, sibling = (mx, my, mc), (mx, my, 1 - mc)
        chips = [(1 - mx, my), (mx, 1 - my), (1 - mx, 1 - my)]

        def slot(l, dev):
            return out_ref.at[l, 4 * dev[0] + 2 * dev[1] + dev[2]]

        def copy(l, k, block, to, src=None):
            return pltpu.make_async_remote_copy(
                src_ref=slot(l, block) if src is None else src, dst_ref=slot(l, block),
                send_sem=send_sems.at[l, k], recv_sem=recv_sems.at[l, k], device_id=to, device_id_type=MESH)

        mine = [pltpu.make_async_copy(x_ref.at[l], slot(l, me), local_sems.at[l]) for l in range(nl)]
        first = []
        for l in range(nl):
            mine[l].start()
            first.append(copy(l, 0, me, sibling, src=x_ref.at[l]))
            first += [copy(l, 1 + j, me, (*chip, mc), src=x_ref.at[l]) for j, chip in enumerate(chips)]
        for cp in first:
            cp.start()
        passed = []
        for l in range(nl):
            for j, chip in enumerate(chips):
                copy(l, 1 + j, (*chip, mc), me).wait_recv()
                fw = copy(l, 4 + j, (*chip, mc), sibling)
                fw.start()
                passed.append(fw)
        for l in range(nl):
            copy(l, 0, sibling, me).wait_recv()
            for j, chip in enumerate(chips):
                copy(l, 4 + j, (*chip, 1 - mc), me).wait_recv()
        for cp in first + passed:
            cp.wait_send()
        for l in range(nl):
            mine[l].wait()

    return _pc(body, name=name, in_specs=[pl.BlockSpec(memory_space=pl.ANY)],
               out_specs=pl.BlockSpec(memory_space=pl.ANY),
               out_shape=jax.ShapeDtypeStruct((nl, N_DEV, r, c), x.dtype),
               scratch_shapes=[pltpu.SemaphoreType.DMA((nl, 7)), pltpu.SemaphoreType.DMA((nl, 7)),
                               pltpu.SemaphoreType.DMA((nl,))],
               compiler_params=_cp())(x)


def _scatter_blocks(x, name):
    _, r, c = x.shape

    def body(x_ref, out_ref, send_sems, recv_sems, local_sem):
        mx, my, mc = lax.axis_index("x"), lax.axis_index("y"), lax.axis_index("c")
        me = 4 * mx + 2 * my + mc
        own = pltpu.make_async_copy(x_ref.at[me], out_ref.at[me], local_sem)
        own.start()
        sends = []
        for k in range(1, N_DEV):
            px, py, pc = mx ^ (k >> 2), my ^ ((k >> 1) & 1), mc ^ (k & 1)
            peer = 4 * px + 2 * py + pc
            cp = pltpu.make_async_remote_copy(
                src_ref=x_ref.at[peer], dst_ref=out_ref.at[me], send_sem=send_sems.at[k], recv_sem=recv_sems.at[k],
                device_id=(px, py, pc), device_id_type=MESH)
            cp.start()
            sends.append((cp, peer))
        for k, (cp, peer) in enumerate(sends, start=1):
            pltpu.make_async_remote_copy(
                src_ref=x_ref.at[peer], dst_ref=out_ref.at[peer], send_sem=send_sems.at[k], recv_sem=recv_sems.at[k],
                device_id=(mx, my, mc), device_id_type=MESH).wait_recv()
        for cp, _ in sends:
            cp.wait_send()
        own.wait()

    return _pc(body, name=name, in_specs=[pl.BlockSpec(memory_space=pl.ANY)],
               out_specs=pl.BlockSpec(memory_space=pl.ANY), out_shape=jax.ShapeDtypeStruct(x.shape, x.dtype),
               scratch_shapes=[pltpu.SemaphoreType.DMA((N_DEV,)), pltpu.SemaphoreType.DMA((N_DEV,)),
                               pltpu.SemaphoreType.DMA],
               compiler_params=_cp())(x)


PACK_COLS = 1024
BIG = ("w_mix_in", "w_mix_out", "w_xq", "w_xk", "w_xv", "w_xo", "w_ffn_up", "w_ffn_down")
COL_SHARDED = ("w_mix_in", "w_ffn_up")


def _pack(blocks, lead):
    flat = jnp.concatenate(blocks, axis=-1)
    n = flat.shape[-1]
    per = 16 * PACK_COLS
    pad = (-n) % per
    flat = jnp.pad(flat, [(0, 0)] * len(lead) + [(0, pad)])
    return flat.reshape(*lead, (n + pad) // PACK_COLS, PACK_COLS)


def _unpack(packed, sizes):
    flat = packed.reshape(*packed.shape[:-2], -1)
    out, off = [], 0
    for n in sizes:
        out.append(flat[..., off:off + n])
        off += n
    return out


def _full_from_shards(name, shards, shape):
    rows, cols = shape
    if name in COL_SHARDED:
        return jnp.moveaxis(shards.reshape(N_DEV, rows, cols // N_DEV), 0, 1).reshape(rows, cols)
    return shards.reshape(rows, cols)


def _shards_from_full(name, full):
    rows, cols = full.shape
    if name in COL_SHARDED:
        return jnp.moveaxis(full.reshape(rows, N_DEV, cols // N_DEV), 1, 0).reshape(N_DEV, -1)
    return full.reshape(N_DEV, -1)


def _mix_in_split(w, width, nh):
    main = jnp.concatenate([w[:, :4 * width], w[:, 4 * width + 2 * nh:]], axis=1)
    gates = jnp.pad(w[:, 4 * width:4 * width + 2 * nh], ((0, 0), (0, 128 - 2 * nh)))
    return main, gates


def _mix_in_join(main, gates, width, nh):
    return jnp.concatenate([main[:, :4 * width], gates[:, :2 * nh], main[:, 4 * width:]], axis=1)


def kernel(x, mem, mix_norm, w_mix_in, gdn_conv, gdn_a_log, gdn_dt_bias, gdn_out_norm, sc_conv, w_mix_out, xattn_norm, mem_norm, w_xq, w_xk, w_xv, w_xo, ffn_norm, w_ffn_up, ffn_conv, w_ffn_down, final_norm, loss_target, m_mix_norm, m_w_mix_in, m_gdn_conv, m_gdn_a_log, m_gdn_dt_bias, m_gdn_out_norm, m_sc_conv, m_w_mix_out, m_xattn_norm, m_mem_norm, m_w_xq, m_w_xk, m_w_xv, m_w_xo, m_ffn_norm, m_w_ffn_up, m_ffn_conv, m_w_ffn_down, m_final_norm, v_mix_norm, v_w_mix_in, v_gdn_conv, v_gdn_a_log, v_gdn_dt_bias, v_gdn_out_norm, v_sc_conv, v_w_mix_out, v_xattn_norm, v_mem_norm, v_w_xq, v_w_xk, v_w_xv, v_w_xo, v_ffn_norm, v_w_ffn_up, v_ffn_conv, v_w_ffn_down, v_final_norm):
    names = ["mix_norm", "w_mix_in", "gdn_conv", "gdn_a_log", "gdn_dt_bias", "gdn_out_norm", "sc_conv", "w_mix_out",
             "xattn_norm", "mem_norm", "w_xq", "w_xk", "w_xv", "w_xo", "ffn_norm", "w_ffn_up", "ffn_conv",
             "w_ffn_down", "final_norm"]
    wts = dict(zip(names, (mix_norm, w_mix_in, gdn_conv, gdn_a_log, gdn_dt_bias, gdn_out_norm, sc_conv, w_mix_out,
                           xattn_norm, mem_norm, w_xq, w_xk, w_xv, w_xo, ffn_norm, w_ffn_up, ffn_conv, w_ffn_down,
                           final_norm)))
    mom1 = dict(zip(names, (m_mix_norm, m_w_mix_in, m_gdn_conv, m_gdn_a_log, m_gdn_dt_bias, m_gdn_out_norm, m_sc_conv,
                            m_w_mix_out, m_xattn_norm, m_mem_norm, m_w_xq, m_w_xk, m_w_xv, m_w_xo, m_ffn_norm,
                            m_w_ffn_up, m_ffn_conv, m_w_ffn_down, m_final_norm)))
    mom2 = dict(zip(names, (v_mix_norm, v_w_mix_in, v_gdn_conv, v_gdn_a_log, v_gdn_dt_bias, v_gdn_out_norm, v_sc_conv,
                            v_w_mix_out, v_xattn_norm, v_mem_norm, v_w_xq, v_w_xk, v_w_xv, v_w_xo, v_ffn_norm,
                            v_w_ffn_up, v_ffn_conv, v_w_ffn_down, v_final_norm)))

    x0 = x[0]
    memv = mem[0]
    target = loss_target[0]
    s, d = x0.shape
    depth = mix_norm.shape[0]
    width = d // 2
    nh = width // HEAD
    f = w_ffn_down.shape[1] * N_DEV
    me = 4 * lax.axis_index("x") + 2 * lax.axis_index("y") + lax.axis_index("c")
    full_shape = {"w_mix_in": (d, 7 * width + 2 * nh), "w_mix_out": (d, d), "w_xq": (d, d), "w_xk": (d, d),
                  "w_xv": (d, d), "w_xo": (d, d), "w_ffn_up": (d, 2 * f), "w_ffn_down": (f, d)}
    shard_sizes = [wts[n][0].size for n in BIG]

    packed = _pack([wts[n].astype(BF16).reshape(depth, -1) for n in BIG], (depth,))
    gathered = _all_gather(packed, "all_gather_weights")
    conv_names = ("gdn_conv", "sc_conv", "ffn_conv")
    conv_local = _pack([wts[n].reshape(1, -1) for n in conv_names], (1,))
    conv_all = _all_gather(conv_local, "all_gather_conv")[0]
    conv_parts = _unpack(conv_all, [wts[n].size for n in conv_names])
    conv_full = {}
    for n, part in zip(conv_names, conv_parts):
        _, kt, cs = wts[n].shape
        conv_full[n] = jnp.moveaxis(part.reshape(N_DEV, depth, kt, cs), 0, 2).reshape(depth, kt, N_DEV * cs)

    def layer_weights(l):
        parts = _unpack(gathered[l], shard_sizes)
        return {n: _full_from_shards(n, p, full_shape[n]) for n, p in zip(BIG, parts)}

    def gate_params(l):
        gp = jnp.zeros((8, 128), F32)
        gp = gp.at[0, nh:2 * nh].set(gdn_a_log[l])
        return gp.at[1, nh:2 * nh].set(gdn_dt_bias[l])

    saved = []
    xc = x0
    for l in range(depth):
        wl = layer_weights(l)
        w_main, w_gate = _mix_in_split(wl["w_mix_in"], width, nh)
        gp = gate_params(l)
        h1 = _rms_fwd(xc, mix_norm[l], "rms_mix")
        proj = _mm(h1, w_main, out_dtype=BF16, name="mm_mix_in")
        ba = _mm(h1, w_gate, out_dtype=F32, name="mm_mix_gates")
        qkv = _gdn_prep_fwd(proj, conv_full["gdn_conv"][l], width)
        o_raw, states, tinv = _gdn_fwd(qkv, ba, gp, width)
        y = _mix_post_fwd(o_raw, proj, gdn_out_norm[l], conv_full["sc_conv"][l], width)
        x1 = _mm(y, wl["w_mix_out"], res=xc, name="mm_mix_out")
        h2 = _rms_fwd(x1, xattn_norm[l], "rms_xattn")
        mem_n = _rms_fwd(memv, mem_norm[l], "rms_mem")
        qx = _mm(h2, wl["w_xq"], out_dtype=BF16, name="mm_xq")
        kx = _mm(mem_n, wl["w_xk"], out_dtype=BF16, name="mm_xk")
        vx = _mm(mem_n, wl["w_xv"], out_dtype=BF16, name="mm_xv")
        ox = _xattn_fwd(qx, kx, vx)
        x2 = _mm(ox, wl["w_xo"], res=x1, name="mm_xo")
        h3 = _rms_fwd(x2, ffn_norm[l], "rms_ffn")
        u_pre = _mm(h3, wl["w_ffn_up"], out_dtype=BF16, name="mm_ffn_up")
        act = _ffn_act_fwd(u_pre, conv_full["ffn_conv"][l])
        x3 = _mm(act, wl["w_ffn_down"], res=x2, name="mm_ffn_down")
        saved.append(dict(x0=xc, x1=x1, x2=x2, h1=h1, h2=h2, h3=h3, proj=proj, ba=ba, qkv=qkv, o_raw=o_raw,
                          states=states, tinv=tinv, y=y, mem_n=mem_n, qx=qx, kx=kx, vx=vx, ox=ox, u_pre=u_pre, act=act))
        xc = x3

    loss_part, dx, g_final = _loss_head(xc, final_norm, target)

    small = {n: [None] * depth for n in ("mix_norm", "xattn_norm", "mem_norm", "ffn_norm", "gdn_a_log", "gdn_dt_bias",
                                          "gdn_out_norm", "gdn_conv", "sc_conv", "ffn_conv")}
    grad_packed = [None] * depth
    for l in reversed(range(depth)):
        sv = saved[l]
        wl = layer_weights(l)
        w_main, w_gate = _mix_in_split(wl["w_mix_in"], width, nh)
        gp = gate_params(l)
        big = {}
        d_act = _mm(dx, wl["w_ffn_down"], tb=True, out_dtype=BF16, name="mm_d_act")
        big["w_ffn_down"] = _mm(sv["act"], dx, ta=True, name="mm_dw_ffn_down")
        du, dcw = _ffn_act_bwd(d_act, sv["u_pre"], conv_full["ffn_conv"][l])
        small["ffn_conv"][l] = jnp.concatenate([dcw[0], dcw[1]], axis=1)
        big["w_ffn_up"] = jnp.concatenate([_mm(sv["h3"], du[0], ta=True, name="mm_dw_ffn_gate"),
                                           _mm(sv["h3"], du[1], ta=True, name="mm_dw_ffn_up")], axis=1)
        dh = _mm(du[0], wl["w_ffn_up"][:, :f], tb=True, out_dtype=F32, name="mm_dh3_gate")
        dh = _mm(du[1], wl["w_ffn_up"][:, f:], tb=True, out_dtype=BF16, res=dh, name="mm_dh3_up")
        dx, small["ffn_norm"][l] = _rms_bwd(sv["x2"], dh, ffn_norm[l], dx, "rms_bwd_ffn")
        d_ox = _mm(dx, wl["w_xo"], tb=True, out_dtype=BF16, name="mm_d_ox")
        big["w_xo"] = _mm(sv["ox"], dx, ta=True, name="mm_dw_xo")
        d_qx, d_kx, d_vx = _xattn_bwd(sv["qx"], sv["kx"], sv["vx"], d_ox)
        big["w_xq"] = _mm(sv["h2"], d_qx, ta=True, name="mm_dw_xq")
        big["w_xk"] = _mm(sv["mem_n"], d_kx, ta=True, name="mm_dw_xk")
        big["w_xv"] = _mm(sv["mem_n"], d_vx, ta=True, name="mm_dw_xv")
        dh = _mm(d_qx, wl["w_xq"], tb=True, out_dtype=BF16, name="mm_dh2")
        d_mem = _mm(d_kx, wl["w_xk"], tb=True, out_dtype=F32, name="mm_dmem_k")
        d_mem = _mm(d_vx, wl["w_xv"], tb=True, out_dtype=F32, res=d_mem, name="mm_dmem_v")
        _, small["mem_norm"][l] = _rms_bwd(memv, d_mem, mem_norm[l], jnp.zeros_like(memv), "rms_bwd_mem")
        dx, small["xattn_norm"][l] = _rms_bwd(sv["x1"], dh, xattn_norm[l], dx, "rms_bwd_xattn")
        d_y = _mm(dx, wl["w_mix_out"], tb=True, out_dtype=BF16, name="mm_d_y")
        big["w_mix_out"] = _mm(sv["y"], dx, ta=True, name="mm_dw_mix_out")
        d_o, d_zbch, small["gdn_out_norm"][l], small["sc_conv"][l] = _mix_post_bwd(
            d_y, sv["o_raw"], sv["proj"], gdn_out_norm[l], conv_full["sc_conv"][l], width)
        d_qkv, d_gb = _gdn_bwd(sv["qkv"], sv["ba"], gp, d_o, sv["states"], sv["tinv"], width)
        d_ba, d_gp = _gates_bwd(sv["ba"], d_gb, gp, nh)
        small["gdn_a_log"][l] = d_gp[0, nh:2 * nh]
        small["gdn_dt_bias"][l] = d_gp[1, nh:2 * nh]
        d_pqkv, small["gdn_conv"][l] = _gdn_prep_bwd(d_qkv, sv["proj"], conv_full["gdn_conv"][l], width)
        d_proj = jnp.concatenate([d_pqkv, d_zbch], axis=1)
        dw_main = _mm(sv["h1"], d_proj, ta=True, name="mm_dw_mix_in")
        dw_gate = _mm(sv["h1"], d_ba, ta=True, name="mm_dw_mix_gates")
        big["w_mix_in"] = _mix_in_join(dw_main, dw_gate, width, nh)
        dh = _mm(d_proj, w_main, tb=True, out_dtype=F32, name="mm_dh1_main")
        dh = _mm(d_ba, w_gate, tb=True, out_dtype=BF16, res=dh, name="mm_dh1_gates")
        dx, small["mix_norm"][l] = _rms_bwd(sv["x0"], dh, mix_norm[l], dx, "rms_bwd_mix")
        outgoing = _pack([_shards_from_full(n, big[n]).astype(BF16) for n in BIG], (N_DEV,))
        grad_packed[l] = _slot_sum(_scatter_blocks(outgoing, "scatter_grads"), "sum_grads")

    small_names = ("mix_norm", "xattn_norm", "mem_norm", "ffn_norm", "gdn_a_log", "gdn_dt_bias", "gdn_out_norm",
                   "gdn_conv", "sc_conv", "ffn_conv")
    small_parts = [jnp.stack(small[n]).reshape(1, -1) for n in small_names]
    small_parts += [g_final.reshape(1, -1), loss_part.reshape(1, 1)]
    small_sizes = [p.shape[1] for p in small_parts]
    small_local = _pack(small_parts, (1,))
    small_sum = _slot_sum(_all_gather(small_local, "all_gather_small")[0], "sum_small")
    small_tot = _unpack(small_sum, small_sizes)
    grads = {}
    for n, g in zip(small_names, small_tot[:len(small_names)]):
        if n in conv_names:
            _, kt, cs = wts[n].shape
            g = lax.dynamic_slice_in_dim(g.reshape(depth, kt, N_DEV * cs), me * cs, cs, axis=2)
        grads[n] = g.reshape(wts[n].shape)
    grads["final_norm"] = small_tot[-2].reshape(final_norm.shape)
    loss = small_tot[-1].reshape(())
    per_layer = [_unpack(grad_packed[l], shard_sizes) for l in range(depth)]
    for i, n in enumerate(BIG):
        grads[n] = jnp.stack([per_layer[l][i] for l in range(depth)]).reshape(wts[n].shape)

    delta, new_m, new_v = {}, {}, {}
    for n in names:
        delta[n], new_m[n], new_v[n] = _adamw(wts[n], grads[n], mom1[n], mom2[n], "adamw_" + n)
    return (loss, dx[None], *[grads[n] for n in names], *[delta[n] for n in names],
            *[new_m[n] for n in names], *[new_v[n] for n in names])
```

```python
import functools

import jax
import jax.numpy as jnp
from jax import lax
from jax.experimental import pallas as pl
from jax.experimental.pallas import tpu as pltpu

F32 = jnp.float32
BF16 = jnp.bfloat16
CHUNK = 64
HEAD = 128
XHEADS = 4
GDN_K = 4
EPS = 1e-6
HALO = 16
QKV_COL = 4
N_DEV = 8
VMEM_LIMIT = 56 * 1024 * 1024
ADAM_LR, ADAM_B1, ADAM_B2, ADAM_EPS, ADAM_WD, ADAM_STEP = 0.001, 0.9, 0.999, 1e-08, 0.01, 10
MESH = pl.DeviceIdType.MESH


def _pc(body, **kw):
    return pl.pallas_call(body, **kw)


def _cp(sem=None, **kw):
    if sem is not None:
        kw["dimension_semantics"] = sem
    return pltpu.CompilerParams(vmem_limit_bytes=VMEM_LIMIT, **kw)


def _tile(n, cands):
    for c in cands:
        if n % c == 0:
            return c
    return n


def _sigmoid(x):
    return 1.0 / (1.0 + jnp.exp(-x))


def _softplus(x):
    return jnp.maximum(x, 0.0) + jnp.log(1.0 + jnp.exp(-jnp.abs(x)))


def _mmb(a, b):
    return jnp.dot(a.astype(BF16), b.astype(BF16), preferred_element_type=F32)


def _mmb_nt(a, b):
    return lax.dot_general(a.astype(BF16), b.astype(BF16), (((1,), (1,)), ((), ())), preferred_element_type=F32)


def _mmb_tn(a, b):
    return lax.dot_general(a.astype(BF16), b.astype(BF16), (((0,), (0,)), ((), ())), preferred_element_type=F32)


def _split(x):
    hi = x.astype(BF16)
    return hi, x - hi.astype(F32)


def _mm3(a, b):
    ah, ar = _split(a)
    bh, br = _split(b)
    al, bl = ar.astype(BF16), br.astype(BF16)
    return (jnp.dot(ah, bh, preferred_element_type=F32)
            + (jnp.dot(ah, bl, preferred_element_type=F32) + jnp.dot(al, bh, preferred_element_type=F32)))


def _mm_exact_lhs(a, b):
    ab = a.astype(BF16)
    b1, r1 = _split(b)
    b2, r2 = _split(r1)
    return (jnp.dot(ab, b1, preferred_element_type=F32)
            + (jnp.dot(ab, b2, preferred_element_type=F32) + jnp.dot(ab, r2.astype(BF16), preferred_element_type=F32)))


def _shift_down(cur, prev, s):
    if s == 0:
        return cur
    r = pltpu.roll(cur, s, 0)
    p = pltpu.roll(prev, s, 0)
    rows = lax.broadcasted_iota(jnp.int32, prev.shape, 0)
    first = jnp.where(rows < s, p, r[:HALO])
    return jnp.concatenate([first, r[HALO:]], axis=0)


def _shift_up(ext, s, tm):
    if s == 0:
        return ext[:tm]
    return pltpu.roll(ext, ext.shape[0] - s, 0)[:tm]


def _prev_map(tm, col):
    return lambda i, j: (jnp.maximum(i * (tm // HALO) - 1, 0), col(j))


def _next_map(tm, nrows, col):
    return lambda i, j: (jnp.minimum((i + 1) * (tm // HALO), nrows // HALO - 1), col(j))


def _mm(a, b, *, ta=False, tb=False, out_dtype=F32, res=None, name, a_halves=False, b_shards=False,
        b_halves=False, out_shards=False):
    if a_halves:
        m, k = a.shape[1], 2 * a.shape[2]
    else:
        m, k = (a.shape[1], a.shape[0]) if ta else a.shape
    if b_shards:
        cs = b.shape[2]
        n = b.shape[1] if tb else N_DEV * cs
    elif b_halves:
        n = 2 * b.shape[2]
        cs = n // N_DEV
    else:
        n = b.shape[0] if tb else b.shape[1]
        cs = None
    tm = _tile(m, (1024, 512, 256, 128))
    tn = cs if (cs is not None and not tb) else _tile(n, (1024, 512, 256, 128))
    tk = cs if (b_shards and tb) else _tile(k, (512, 256, 128))
    nk = k // tk
    dn = (((0 if ta else 1,), (1 if tb else 0,)), ((), ()))

    def body(a_ref, b_ref, *rest):
        if res is None:
            o_ref, acc = rest
        else:
            r_ref, o_ref, acc = rest
        kk = pl.program_id(2)

        @pl.when(kk == 0)
        def _():
            acc[...] = jnp.zeros_like(acc)

        acc[...] += lax.dot_general(a_ref[...].astype(BF16), b_ref[...].astype(BF16), dn,
                                    preferred_element_type=F32)

        @pl.when(kk == nk - 1)
        def _():
            if res is None:
                o_ref[...] = acc[...].astype(out_dtype)
            else:
                o_ref[...] = (acc[...] + r_ref[...].astype(F32)).astype(out_dtype)

    if a_halves:
        per = (k // 2) // tk
        a_spec = pl.BlockSpec((None, tm, tk), lambda i, j, kk: (kk // per, i, kk % per))
    elif ta:
        a_spec = pl.BlockSpec((tk, tm), lambda i, j, kk: (kk, i))
    else:
        a_spec = pl.BlockSpec((tm, tk), lambda i, j, kk: (i, kk))
    if b_shards and tb:
        b_spec = pl.BlockSpec((None, tn, tk), lambda i, j, kk: (kk, j, 0))
    elif b_shards:
        b_spec = pl.BlockSpec((None, tk, tn), lambda i, j, kk: (j, kk, 0))
    elif b_halves:
        perb = (n // 2) // tn
        b_spec = pl.BlockSpec((None, tk, tn), lambda i, j, kk: (j // perb, kk, j % perb))
    elif tb:
        b_spec = pl.BlockSpec((tn, tk), lambda i, j, kk: (j, kk))
    else:
        b_spec = pl.BlockSpec((tk, tn), lambda i, j, kk: (kk, j))
    if out_shards:
        o_spec = pl.BlockSpec((None, tm, tn), lambda i, j, kk: (j, i, 0))
        o_shape = jax.ShapeDtypeStruct((N_DEV, m, tn), out_dtype)
    else:
        o_spec = pl.BlockSpec((tm, tn), lambda i, j, kk: (i, j))
        o_shape = jax.ShapeDtypeStruct((m, n), out_dtype)
    in_specs = [a_spec, b_spec] + ([o_spec] if res is not None else [])
    args = (a, b) + ((res,) if res is not None else ())
    return _pc(body, name=name, grid=(m // tm, n // tn, nk), in_specs=in_specs, out_specs=o_spec,
               out_shape=o_shape, scratch_shapes=[pltpu.VMEM((tm, tn), F32)],
               compiler_params=_cp(("parallel", "parallel", "arbitrary")))(*args)


def _rms_fwd(x, w, name):
    s, d = x.shape
    tm = _tile(s, (512, 256, 128, 64))

    def body(x_ref, w_ref, o_ref):
        xv = x_ref[...]
        r = lax.rsqrt(jnp.mean(xv * xv, axis=-1, keepdims=True) + EPS)
        o_ref[...] = (xv * r * w_ref[...]).astype(BF16)

    return _pc(body, name=name, grid=(s // tm,),
               in_specs=[pl.BlockSpec((tm, d), lambda i: (i, 0)), pl.BlockSpec((1, d), lambda i: (0, 0))],
               out_specs=pl.BlockSpec((tm, d), lambda i: (i, 0)), out_shape=jax.ShapeDtypeStruct((s, d), BF16),
               compiler_params=_cp(("parallel",)))(x, w.reshape(1, d))


def _rms_bwd(x, dh, w, dx_in, name):
    s, d = x.shape
    tm = _tile(s, (256, 128, 64))

    def body(x_ref, dh_ref, w_ref, dxi_ref, dx_ref, dg_ref):
        @pl.when(pl.program_id(0) == 0)
        def _():
            dg_ref[...] = jnp.zeros_like(dg_ref)

        xv = x_ref[...]
        dy = dh_ref[...].astype(F32)
        r = lax.rsqrt(jnp.mean(xv * xv, axis=-1, keepdims=True) + EPS)
        xh = xv * r
        dxh = dy * w_ref[...]
        dx_ref[...] = dxi_ref[...] + r * (dxh - xh * jnp.mean(dxh * xh, axis=-1, keepdims=True))
        dg_ref[0:1, :] += jnp.sum(dy * xh, axis=0, keepdims=True)

    row = pl.BlockSpec((tm, d), lambda i: (i, 0))
    dx, dg = _pc(body, name=name, grid=(s // tm,),
                 in_specs=[row, row, pl.BlockSpec((1, d), lambda i: (0, 0)), row],
                 out_specs=[row, pl.BlockSpec((8, d), lambda i: (0, 0))],
                 out_shape=[jax.ShapeDtypeStruct((s, d), F32), jax.ShapeDtypeStruct((8, d), F32)],
                 compiler_params=_cp(("arbitrary",)))(x, dh, w.reshape(1, d), dx_in)
    return dx, dg[0]


def _loss_head(x, w, target):
    s, d = x.shape
    tm = _tile(s, (256, 128, 64))

    def body(x_ref, w_ref, t_ref, dx_ref, dg_ref, l_ref):
        @pl.when(pl.program_id(0) == 0)
        def _():
            dg_ref[...] = jnp.zeros_like(dg_ref)
            l_ref[...] = jnp.zeros_like(l_ref)

        xv = x_ref[...]
        r = lax.rsqrt(jnp.mean(xv * xv, axis=-1, keepdims=True) + EPS)
        xh = xv * r
        err = xh * w_ref[...] - t_ref[...]
        l_ref[...] += 0.5 * jnp.sum(jnp.mean(err * err, axis=-1, keepdims=True), axis=0, keepdims=True)
        dy = err * (1.0 / d)
        dxh = dy * w_ref[...]
        dx_ref[...] = r * (dxh - xh * jnp.mean(dxh * xh, axis=-1, keepdims=True))
        dg_ref[0:1, :] += jnp.sum(dy * xh, axis=0, keepdims=True)

    row = pl.BlockSpec((tm, d), lambda i: (i, 0))
    dx, dg, ls = _pc(body, name="loss_head", grid=(s // tm,),
                     in_specs=[row, pl.BlockSpec((1, d), lambda i: (0, 0)), row],
                     out_specs=[row, pl.BlockSpec((8, d), lambda i: (0, 0)), pl.BlockSpec((8, 128), lambda i: (0, 0))],
                     out_shape=[jax.ShapeDtypeStruct((s, d), F32), jax.ShapeDtypeStruct((8, d), F32),
                                jax.ShapeDtypeStruct((8, 128), F32)],
                     compiler_params=_cp(("arbitrary",)))(x, w.reshape(1, d), target)
    return ls[0, 0], dx, dg[0]


def _gdn_prep_fwd(proj, conv_w, width):
    s = proj.shape[0]
    tm = _tile(s, (256, 128, 64))
    nh = width // HEAD

    def body(c_ref, p_ref, w_ref, o_ref):
        i, seg = pl.program_id(0), pl.program_id(1)
        cur = c_ref[...].astype(F32)
        prev = jnp.where(i > 0, p_ref[...].astype(F32), 0.0)
        pre = cur * w_ref[GDN_K - 1:GDN_K, :]
        for j in range(GDN_K - 1):
            pre = pre + _shift_down(cur, prev, GDN_K - 1 - j) * w_ref[j:j + 1, :]
        act = pre * _sigmoid(pre)
        scale = jnp.where(seg == 0, HEAD ** -0.5, 1.0)
        for h in range(nh):
            a = act[:, h * HEAD:(h + 1) * HEAD]
            rs = lax.rsqrt(jnp.sum(a * a, axis=-1, keepdims=True) + EPS) * scale
            o_ref[:, h * HEAD:(h + 1) * HEAD] = a * jnp.where(seg < 2, rs, 1.0)

    return _pc(body, name="gdn_prep_fwd", grid=(s // tm, 3),
               in_specs=[pl.BlockSpec((tm, width), lambda i, j: (i, j + QKV_COL)),
                         pl.BlockSpec((HALO, width), _prev_map(tm, lambda j: j + QKV_COL)),
                         pl.BlockSpec((GDN_K, width), lambda i, j: (0, j))],
               out_specs=pl.BlockSpec((tm, width), lambda i, j: (i, j)),
               out_shape=jax.ShapeDtypeStruct((s, 3 * width), F32),
               compiler_params=_cp(("parallel", "parallel")))(proj, proj, conv_w)


def _gdn_prep_bwd(dqkv, proj, conv_w, dproj_in, width):
    s = proj.shape[0]
    tm = _tile(s, (256, 128, 64))
    nh = width // HEAD
    nt = s // tm

    def body(c_ref, p_ref, n_ref, d_ref, dn_ref, w_ref, _, o_ref, dw_ref):
        seg, i = pl.program_id(0), pl.program_id(1)

        @pl.when(i == 0)
        def _():
            dw_ref[...] = jnp.zeros_like(dw_ref)

        ext = jnp.concatenate([c_ref[...].astype(F32), n_ref[...].astype(F32)], axis=0)
        prev = jnp.where(i > 0, p_ref[...].astype(F32), 0.0)
        sh = [_shift_down(ext, prev, GDN_K - 1 - j) for j in range(GDN_K)]
        pre = sh[0] * w_ref[0:1, :]
        for j in range(1, GDN_K):
            pre = pre + sh[j] * w_ref[j:j + 1, :]
        sg = _sigmoid(pre)
        act = pre * sg
        dout = jnp.concatenate([d_ref[...], dn_ref[...]], axis=0)
        rows = lax.broadcasted_iota(jnp.int32, (tm + HALO, 1), 0)
        dout = jnp.where((rows < tm) | (i < nt - 1), dout, 0.0)
        scale = jnp.where(seg == 0, HEAD ** -0.5, 1.0)
        parts = []
        for h in range(nh):
            a = act[:, h * HEAD:(h + 1) * HEAD]
            dq = dout[:, h * HEAD:(h + 1) * HEAD]
            rs = lax.rsqrt(jnp.sum(a * a, axis=-1, keepdims=True) + EPS)
            nrm = a * rs
            dn = dq * scale
            da_norm = rs * (dn - nrm * jnp.sum(dn * nrm, axis=-1, keepdims=True))
            parts.append(jnp.where(seg < 2, da_norm, dq))
        dact = jnp.concatenate(parts, axis=1)
        dpre = dact * (sg * (1.0 + pre * (1.0 - sg)))
        dp = _shift_up(dpre, 0, tm) * w_ref[GDN_K - 1:GDN_K, :]
        for j in range(GDN_K - 1):
            dp = dp + _shift_up(dpre, GDN_K - 1 - j, tm) * w_ref[j:j + 1, :]
        o_ref[...] = dp.astype(BF16)
        for j in range(GDN_K):
            dw_ref[j:j + 1, :] += jnp.sum(dpre[:tm] * sh[j][:tm], axis=0, keepdims=True)

    dproj, dw = _pc(body, name="gdn_prep_bwd", grid=(3, nt),
                    in_specs=[pl.BlockSpec((tm, width), lambda j, i: (i, j + QKV_COL)),
                              pl.BlockSpec((HALO, width), lambda j, i: _prev_map(tm, lambda c: c + QKV_COL)(i, j)),
                              pl.BlockSpec((HALO, width), lambda j, i: _next_map(tm, s, lambda c: c + QKV_COL)(i, j)),
                              pl.BlockSpec((tm, width), lambda j, i: (i, j)),
                              pl.BlockSpec((HALO, width), lambda j, i: _next_map(tm, s, lambda c: c)(i, j)),
                              pl.BlockSpec((GDN_K, width), lambda j, i: (0, j)),
                              pl.BlockSpec(memory_space=pl.ANY)],
                    out_specs=[pl.BlockSpec((tm, width), lambda j, i: (i, j + QKV_COL)),
                               pl.BlockSpec((8, width), lambda j, i: (0, j))],
                    out_shape=[jax.ShapeDtypeStruct(dproj_in.shape, BF16), jax.ShapeDtypeStruct((8, 3 * width), F32)],
                    input_output_aliases={6: 0},
                    compiler_params=_cp(("parallel", "arbitrary")))(proj, proj, proj, dqkv, dqkv, conv_w, dproj_in)
    return dproj, dw[:GDN_K]


def _chunk_common(bav, gp_ref, nh):
    g_full = -jnp.exp(gp_ref[0:1, :]) * _softplus(bav + gp_ref[1:2, :])
    beta_full = _sigmoid(bav)
    ri = lax.broadcasted_iota(jnp.int32, (CHUNK, CHUNK), 0)
    ci = lax.broadcasted_iota(jnp.int32, (CHUNK, CHUNK), 1)
    gc_full = _mm_exact_lhs(ri >= ci, g_full)
    gc_t = gc_full.T
    return beta_full, gc_full, gc_t, ri, ci


def _head_gates(h, nh, beta_full, gc_full, gc_t, ri, ci):
    bcol = beta_full[:, h:h + 1]
    gcol = gc_full[:, nh + h:nh + h + 1]
    grow = gc_t[nh + h:nh + h + 1, :]
    dec = jnp.exp(jnp.where(ri >= ci, gcol - grow, -1e30))
    ecol = jnp.exp(gcol)
    gl = gcol[CHUNK - 1:CHUNK, :]
    return bcol, gcol, dec, ecol, gl


def _gdn_fwd(qkv, ba, gp, width):
    s = qkv.shape[0]
    nh = width // HEAD
    nc = s // CHUNK

    def body(q_ref, k_ref, v_ref, ba_ref, gp_ref, o_ref, st_ref, t_ref, state):
        @pl.when(pl.program_id(0) == 0)
        def _():
            state[...] = jnp.zeros_like(state)

        beta_full, gc_full, gc_t, ri, ci = _chunk_common(ba_ref[...], gp_ref, nh)
        eye = (ri == ci).astype(F32)
        for h in range(nh):
            sl = slice(h * HEAD, (h + 1) * HEAD)
            q, k, v = q_ref[:, sl], k_ref[:, sl], v_ref[:, sl]
            bcol, gcol, dec, ecol, gl = _head_gates(h, nh, beta_full, gc_full, gc_t, ri, ci)
            kb = k * bcol
            a = jnp.where(ri > ci, _mmb_nt(kb, k) * dec, 0.0)
            t = eye - a
            pw = _mm3(a, a)
            for _ in range(4):
                both = _mm3(jnp.concatenate([t, pw], axis=0), pw)
                t, pw = t + both[:CHUNK], both[CHUNK:]
            t = t + _mm3(t, pw)
            uw = _mmb(t, jnp.concatenate([v * bcol, kb * ecol], axis=1))
            u, w = uw[:, :HEAD], uw[:, HEAD:]
            attn = jnp.where(ri >= ci, _mmb_nt(q, k) * dec, 0.0)
            st = state[h]
            vn = u - _mmb(w, st)
            o_ref[:, sl] = _mmb(q * ecol, st) + _mmb(attn, vn)
            st_ref[0, h] = st
            t_ref[0, h] = t
            state[h] = st * jnp.exp(gl) + _mmb_tn(k * jnp.exp(gl - gcol), vn)

    blk = lambda c: pl.BlockSpec((CHUNK, width), lambda n, c=c: (n, c))
    return _pc(body, name="gdn_fwd", grid=(nc,),
               in_specs=[blk(0), blk(1), blk(2), pl.BlockSpec((CHUNK, 128), lambda n: (n, 0)),
                         pl.BlockSpec((8, 128), lambda n: (0, 0))],
               out_specs=[blk(0), pl.BlockSpec((1, nh, HEAD, HEAD), lambda n: (n, 0, 0, 0)),
                          pl.BlockSpec((1, nh, CHUNK, CHUNK), lambda n: (n, 0, 0, 0))],
               out_shape=[jax.ShapeDtypeStruct((s, width), F32), jax.ShapeDtypeStruct((nc, nh, HEAD, HEAD), F32),
                          jax.ShapeDtypeStruct((nc, nh, CHUNK, CHUNK), F32)],
               scratch_shapes=[pltpu.VMEM((nh, HEAD, HEAD), F32)],
               compiler_params=_cp(("arbitrary",)))(qkv, qkv, qkv, ba, gp)


def _gdn_bwd(qkv, ba, gp, do, states, tinv, width):
    s = qkv.shape[0]
    nh = width // HEAD
    nc = s // CHUNK

    def body(q_ref, k_ref, v_ref, ba_ref, gp_ref, do_ref, st_ref, t_ref, dqkv_ref, dgb_ref, dstate):
        @pl.when(pl.program_id(0) == 0)
        def _():
            dstate[...] = jnp.zeros_like(dstate)

        beta_full, gc_full, gc_t, ri, ci = _chunk_common(ba_ref[...], gp_ref, nh)
        lane = lax.broadcasted_iota(jnp.int32, (CHUNK, 128), 1)
        rowi = lax.broadcasted_iota(jnp.int32, (CHUNK, 1), 0)
        dbeta_full = jnp.zeros((CHUNK, 128), F32)
        dgc_full = jnp.zeros((CHUNK, 128), F32)
        for h in range(nh):
            sl = slice(h * HEAD, (h + 1) * HEAD)
            q, k, v, dout = q_ref[:, sl], k_ref[:, sl], v_ref[:, sl], do_ref[:, sl]
            bcol, gcol, dec, ecol, gl = _head_gates(h, nh, beta_full, gc_full, gc_t, ri, ci)
            st, t, dsp = st_ref[0, h], t_ref[0, h], dstate[h]
            el = jnp.exp(gl)
            kdsc = jnp.exp(gl - gcol)
            kb = k * bcol
            a = jnp.where(ri > ci, _mmb_nt(kb, k) * dec, 0.0)
            attn = jnp.where(ri >= ci, _mmb_nt(q, k) * dec, 0.0)
            uw = _mmb(t, jnp.concatenate([v * bcol, kb * ecol], axis=1))
            u, w = uw[:, :HEAD], uw[:, HEAD:]
            qd = q * ecol
            kd = k * kdsc
            vn = u - _mmb(w, st)
            d_attn = jnp.where(ri >= ci, _mmb_nt(dout, vn), 0.0)
            d_vn = _mmb_tn(attn, dout) + _mmb(kd, dsp)
            d_qd = _mmb_nt(dout, st)
            d_kd = _mmb_nt(vn, dsp)
            d_el = jnp.sum(jnp.sum(st * dsp, axis=1, keepdims=True), axis=0, keepdims=True)
            dstate[h] = _mmb_tn(qd, dout) + el * dsp - _mmb_tn(w, d_vn)
            d_w = -_mmb_nt(d_vn, st)
            dr = _mmb_tn(t, jnp.concatenate([d_vn, d_w], axis=1))
            dru, drw = dr[:, :HEAD], dr[:, HEAD:]
            d_a = -jnp.where(ri > ci, _mmb_nt(dr, uw), 0.0)
            d_kk = d_a * dec
            d_qk = d_attn * dec
            d_kb = _mmb(d_kk, k) + drw * ecol
            dk = _mmb_tn(d_kk, kb) + _mmb_tn(d_qk, q) + d_kb * bcol + d_kd * kdsc
            dq = _mmb(d_qk, k) + d_qd * ecol
            dv = dru * bcol
            dbeta = jnp.sum(dru * v + d_kb * k, axis=1, keepdims=True)
            de = jnp.sum(drw * kb + d_qd * q, axis=1, keepdims=True)
            r = jnp.sum(d_kd * k, axis=1, keepdims=True) * kdsc
            mm = d_a * a + d_attn * attn
            d_gl = jnp.sum(r, axis=0, keepdims=True) + d_el * el
            d_gc = (de * ecol - r + jnp.sum(mm, axis=1, keepdims=True) - jnp.sum(mm.T, axis=1, keepdims=True)
                    + jnp.where(rowi == CHUNK - 1, d_gl, 0.0))
            dqkv_ref[:, sl] = dq
            dqkv_ref[:, width + h * HEAD:width + (h + 1) * HEAD] = dk
            dqkv_ref[:, 2 * width + h * HEAD:2 * width + (h + 1) * HEAD] = dv
            dbeta_full = dbeta_full + jnp.where(lane == h, dbeta, 0.0)
            dgc_full = dgc_full + jnp.where(lane == nh + h, d_gc, 0.0)
        dgb_ref[...] = dbeta_full + _mm_exact_lhs(ri <= ci, dgc_full)

    rev = lambda c: pl.BlockSpec((CHUNK, width), lambda n, c=c: (nc - 1 - n, c))
    return _pc(body, name="gdn_bwd", grid=(nc,),
               in_specs=[rev(0), rev(1), rev(2), pl.BlockSpec((CHUNK, 128), lambda n: (nc - 1 - n, 0)),
                         pl.BlockSpec((8, 128), lambda n: (0, 0)), rev(0),
                         pl.BlockSpec((1, nh, HEAD, HEAD), lambda n: (nc - 1 - n, 0, 0, 0)),
                         pl.BlockSpec((1, nh, CHUNK, CHUNK), lambda n: (nc - 1 - n, 0, 0, 0))],
               out_specs=[pl.BlockSpec((CHUNK, 3 * width), lambda n: (nc - 1 - n, 0)),
                          pl.BlockSpec((CHUNK, 128), lambda n: (nc - 1 - n, 0))],
               out_shape=[jax.ShapeDtypeStruct((s, 3 * width), F32), jax.ShapeDtypeStruct((s, 128), F32)],
               scratch_shapes=[pltpu.VMEM((nh, HEAD, HEAD), F32)],
               compiler_params=_cp(("arbitrary",)))(qkv, qkv, qkv, ba, gp, do, states, tinv)


def _gates_bwd(ba, dgb, gp, nh):
    s = ba.shape[0]
    tm = _tile(s, (512, 256, 128, 64))

    def body(ba_ref, d_ref, gp_ref, o_ref, dp_ref):
        @pl.when(pl.program_id(0) == 0)
        def _():
            dp_ref[...] = jnp.zeros_like(dp_ref)

        bav, dv = ba_ref[...], d_ref[...]
        lane = lax.broadcasted_iota(jnp.int32, bav.shape, 1)
        beta = _sigmoid(bav)
        amp = jnp.exp(gp_ref[0:1, :])
        z = bav + gp_ref[1:2, :]
        d_a = dv * (-amp) * _sigmoid(z)
        d_b = dv * beta * (1.0 - beta)
        is_a = (lane >= nh) & (lane < 2 * nh)
        o_ref[...] = jnp.where(lane < nh, d_b, jnp.where(is_a, d_a, 0.0))
        dp_ref[0:1, :] += jnp.sum(jnp.where(is_a, dv * (-amp) * _softplus(z), 0.0), axis=0, keepdims=True)
        dp_ref[1:2, :] += jnp.sum(jnp.where(is_a, d_a, 0.0), axis=0, keepdims=True)

    row = pl.BlockSpec((tm, 128), lambda i: (i, 0))
    par = pl.BlockSpec((8, 128), lambda i: (0, 0))
    return _pc(body, name="gates_bwd", grid=(s // tm,), in_specs=[row, row, par], out_specs=[row, par],
               out_shape=[jax.ShapeDtypeStruct((s, 128), F32), jax.ShapeDtypeStruct((8, 128), F32)],
               compiler_params=_cp(("arbitrary",)))(ba, dgb, gp)


def _mix_post_fwd(o_raw, proj, gain, sc_w, width):
    s = o_raw.shape[0]
    tm = _tile(s, (256, 128, 64))
    nh = width // HEAD
    ksc = sc_w.shape[0]

    def body(o_ref, z_ref, b_ref, c_ref, h_ref, cp_ref, hp_ref, g_ref, w_ref, y_ref):
        i = pl.program_id(0)
        z = z_ref[...].astype(F32)
        sz = z * _sigmoid(z)
        for h in range(nh):
            sl = slice(h * HEAD, (h + 1) * HEAD)
            o = o_ref[:, sl]
            r = lax.rsqrt(jnp.mean(o * o, axis=-1, keepdims=True) + EPS)
            y_ref[:, sl] = (o * r * g_ref[...] * sz[:, sl]).astype(BF16)
        prod = c_ref[...].astype(F32) * h_ref[...].astype(F32)
        pprev = jnp.where(i > 0, cp_ref[...].astype(F32) * hp_ref[...].astype(F32), 0.0)
        cv = prod * w_ref[ksc - 1:ksc, :]
        for j in range(ksc - 1):
            cv = cv + _shift_down(prod, pprev, ksc - 1 - j) * w_ref[j:j + 1, :]
        y_ref[:, width:] = (b_ref[...].astype(F32) * cv).astype(BF16)

    col = lambda c: pl.BlockSpec((tm, width), lambda i, c=c: (i, c))
    prv = lambda c: pl.BlockSpec((HALO, width), lambda i, c=c: (jnp.maximum(i * (tm // HALO) - 1, 0), c))
    return _pc(body, name="mix_post_fwd", grid=(s // tm,),
               in_specs=[col(0), col(0), col(1), col(2), col(3), prv(2), prv(3),
                         pl.BlockSpec((1, HEAD), lambda i: (0, 0)), pl.BlockSpec((ksc, width), lambda i: (0, 0))],
               out_specs=pl.BlockSpec((tm, 2 * width), lambda i: (i, 0)),
               out_shape=jax.ShapeDtypeStruct((s, 2 * width), BF16),
               compiler_params=_cp(("parallel",)))(o_raw, proj, proj, proj, proj, proj, proj,
                                                    gain.reshape(1, HEAD), sc_w)


def _mix_post_bwd(dy, o_raw, proj, gain, sc_w, width):
    s = o_raw.shape[0]
    tm = _tile(s, (256, 128, 64))
    nt = s // tm
    nh = width // HEAD
    ksc = sc_w.shape[0]

    def body(dyg_ref, dys_ref, dysn_ref, o_ref, z_ref, b_ref, bn_ref, c_ref, h_ref, cp_ref, hp_ref, g_ref, w_ref,
             do_ref, dp_ref, dg_ref, dw_ref):
        i = pl.program_id(0)

        @pl.when(i == 0)
        def _():
            dg_ref[...] = jnp.zeros_like(dg_ref)
            dw_ref[...] = jnp.zeros_like(dw_ref)

        z = z_ref[...].astype(F32)
        sg = _sigmoid(z)
        sz = z * sg
        dsz = sg * (1.0 + z * (1.0 - sg))
        dyg = dyg_ref[...].astype(F32)
        dgain = jnp.zeros((1, HEAD), F32)
        for h in range(nh):
            sl = slice(h * HEAD, (h + 1) * HEAD)
            o = o_ref[:, sl]
            r = lax.rsqrt(jnp.mean(o * o, axis=-1, keepdims=True) + EPS)
            oh = o * r
            d_yn = dyg[:, sl] * sz[:, sl]
            dp_ref[:, sl] = (dyg[:, sl] * oh * g_ref[...] * dsz[:, sl]).astype(BF16)
            dgain = dgain + jnp.sum(d_yn * oh, axis=0, keepdims=True)
            doh = d_yn * g_ref[...]
            do_ref[:, sl] = r * (doh - oh * jnp.mean(doh * oh, axis=-1, keepdims=True))
        dg_ref[0:1, :] += dgain
        cc, hh = c_ref[...].astype(F32), h_ref[...].astype(F32)
        prod = cc * hh
        pprev = jnp.where(i > 0, cp_ref[...].astype(F32) * hp_ref[...].astype(F32), 0.0)
        sh = [_shift_down(prod, pprev, ksc - 1 - j) for j in range(ksc)]
        cv = sh[0] * w_ref[0:1, :]
        for j in range(1, ksc):
            cv = cv + sh[j] * w_ref[j:j + 1, :]
        dys = dys_ref[...].astype(F32)
        dp_ref[:, width:2 * width] = (dys * cv).astype(BF16)
        dcv_n = jnp.where(i < nt - 1, dysn_ref[...].astype(F32) * bn_ref[...].astype(F32), 0.0)
        dcv = jnp.concatenate([dys * b_ref[...].astype(F32), dcv_n], axis=0)
        dprod = dcv[:tm] * w_ref[ksc - 1:ksc, :]
        for j in range(ksc - 1):
            dprod = dprod + _shift_up(dcv, ksc - 1 - j, tm) * w_ref[j:j + 1, :]
        dp_ref[:, 2 * width:3 * width] = (dprod * hh).astype(BF16)
        dp_ref[:, 3 * width:] = (dprod * cc).astype(BF16)
        for j in range(ksc):
            dw_ref[j:j + 1, :] += jnp.sum(dcv[:tm] * sh[j], axis=0, keepdims=True)

    col = lambda c: pl.BlockSpec((tm, width), lambda i, c=c: (i, c))
    prv = lambda c: pl.BlockSpec((HALO, width), lambda i, c=c: (jnp.maximum(i * (tm // HALO) - 1, 0), c))
    nxt = lambda c: pl.BlockSpec((HALO, width), lambda i, c=c: (jnp.minimum((i + 1) * (tm // HALO), s // HALO - 1), c))
    do, dp, dg, dw = _pc(
        body, name="mix_post_bwd", grid=(nt,),
        in_specs=[col(0), col(1), nxt(1), col(0), col(0), col(1), nxt(1), col(2), col(3), prv(2), prv(3),
                  pl.BlockSpec((1, HEAD), lambda i: (0, 0)), pl.BlockSpec((ksc, width), lambda i: (0, 0))],
        out_specs=[col(0), pl.BlockSpec((tm, 4 * width), lambda i: (i, 0)),
                   pl.BlockSpec((8, HEAD), lambda i: (0, 0)), pl.BlockSpec((8, width), lambda i: (0, 0))],
        out_shape=[jax.ShapeDtypeStruct((s, width), F32), jax.ShapeDtypeStruct((s, 7 * width), BF16),
                   jax.ShapeDtypeStruct((8, HEAD), F32), jax.ShapeDtypeStruct((8, width), F32)],
        compiler_params=_cp(("arbitrary",)))(dy, dy, dy, o_raw, proj, proj, proj, proj, proj, proj, proj,
                                             gain.reshape(1, HEAD), sc_w)
    return do, dp, dg[0], dw[:ksc]


def _xattn_fwd(q, k, v):
    s, d = q.shape
    nm = k.shape[0]
    dh = d // XHEADS
    tm = _tile(s, (512, 256, 128, 64))

    def body(q_ref, k_ref, v_ref, o_ref):
        sc = _mmb_nt(q_ref[...], k_ref[...]) * (dh ** -0.5)
        p = jnp.exp(sc - jnp.max(sc, axis=-1, keepdims=True))
        p = p / jnp.sum(p, axis=-1, keepdims=True)
        o_ref[...] = _mmb(p, v_ref[...]).astype(BF16)

    return _pc(body, name="xattn_fwd", grid=(s // tm, XHEADS),
               in_specs=[pl.BlockSpec((tm, dh), lambda i, h: (i, h)), pl.BlockSpec((nm, dh), lambda i, h: (0, h)),
                         pl.BlockSpec((nm, dh), lambda i, h: (0, h))],
               out_specs=pl.BlockSpec((tm, dh), lambda i, h: (i, h)), out_shape=jax.ShapeDtypeStruct((s, d), BF16),
               compiler_params=_cp(("parallel", "parallel")))(q, k, v)


def _xattn_bwd(q, k, v, do):
    s, d = q.shape
    nm = k.shape[0]
    dh = d // XHEADS
    tm = _tile(s, (512, 256, 128, 64))

    def body(q_ref, k_ref, v_ref, do_ref, dq_ref, dk_ref, dv_ref):
        @pl.when(pl.program_id(1) == 0)
        def _():
            dk_ref[...] = jnp.zeros_like(dk_ref)
            dv_ref[...] = jnp.zeros_like(dv_ref)

        scale = dh ** -0.5
        sc = _mmb_nt(q_ref[...], k_ref[...]) * scale
        p = jnp.exp(sc - jnp.max(sc, axis=-1, keepdims=True))
        p = p / jnp.sum(p, axis=-1, keepdims=True)
        dp = _mmb_nt(do_ref[...], v_ref[...])
        ds = p * (dp - jnp.sum(dp * p, axis=-1, keepdims=True)) * scale
        dq_ref[...] = _mmb(ds, k_ref[...]).astype(BF16)
        dk_ref[...] += _mmb_tn(ds, q_ref[...])
        dv_ref[...] += _mmb_tn(p, do_ref[...])

    rowb = pl.BlockSpec((tm, dh), lambda h, i: (i, h))
    memb = pl.BlockSpec((nm, dh), lambda h, i: (0, h))
    return _pc(body, name="xattn_bwd", grid=(XHEADS, s // tm), in_specs=[rowb, memb, memb, rowb],
               out_specs=[rowb, memb, memb],
               out_shape=[jax.ShapeDtypeStruct((s, d), BF16), jax.ShapeDtypeStruct((nm, d), F32),
                          jax.ShapeDtypeStruct((nm, d), F32)],
               compiler_params=_cp(("parallel", "arbitrary")))(q, k, v, do)


def _ffn_act_fwd(u_pre, conv_w):
    s, f2 = u_pre.shape
    f = f2 // 2
    tm = _tile(s, (256, 128, 64))
    cb = _tile(f, (512, 256, 128))
    nf = f // cb
    kf = conv_w.shape[0]

    def body(g_ref, u_ref, gp_ref, up_ref, wg_ref, wu_ref, a_ref):
        i = pl.program_id(0)

        def conv(c_ref, p_ref, w_ref):
            cur = c_ref[...].astype(F32)
            prev = jnp.where(i > 0, p_ref[...].astype(F32), 0.0)
            out = cur * w_ref[kf - 1:kf, :]
            for j in range(kf - 1):
                out = out + _shift_down(cur, prev, kf - 1 - j) * w_ref[j:j + 1, :]
            return out

        gate, up = conv(g_ref, gp_ref, wg_ref), conv(u_ref, up_ref, wu_ref)
        a_ref[...] = (gate * _sigmoid(gate) * up).astype(BF16)

    return _pc(body, name="ffn_act_fwd", grid=(s // tm, nf),
               in_specs=[pl.BlockSpec((tm, cb), lambda i, j: (i, j)), pl.BlockSpec((tm, cb), lambda i, j: (i, j + nf)),
                         pl.BlockSpec((HALO, cb), _prev_map(tm, lambda j: j)),
                         pl.BlockSpec((HALO, cb), _prev_map(tm, lambda j: j + nf)),
                         pl.BlockSpec((kf, cb), lambda i, j: (0, j)), pl.BlockSpec((kf, cb), lambda i, j: (0, j + nf))],
               out_specs=pl.BlockSpec((tm, cb), lambda i, j: (i, j)), out_shape=jax.ShapeDtypeStruct((s, f), BF16),
               compiler_params=_cp(("parallel", "parallel")))(u_pre, u_pre, u_pre, u_pre, conv_w, conv_w)


def _ffn_act_bwd(da, u_pre, conv_w):
    s, f2 = u_pre.shape
    f = f2 // 2
    tm = _tile(s, (256, 128, 64))
    nt = s // tm
    cb = _tile(f, (512, 256, 128))
    nf = f // cb
    kf = conv_w.shape[0]

    def body(da_ref, dan_ref, g_ref, gp_ref, gn_ref, u_ref, up_ref, un_ref, wg_ref, wu_ref, d_ref, dw_ref):
        i = pl.program_id(1)

        @pl.when(i == 0)
        def _():
            dw_ref[...] = jnp.zeros_like(dw_ref)

        def conv(c_ref, p_ref, n_ref, w_ref):
            ext = jnp.concatenate([c_ref[...].astype(F32), n_ref[...].astype(F32)], axis=0)
            prev = jnp.where(i > 0, p_ref[...].astype(F32), 0.0)
            sh = [_shift_down(ext, prev, kf - 1 - j) for j in range(kf)]
            out = sh[0] * w_ref[0:1, :]
            for j in range(1, kf):
                out = out + sh[j] * w_ref[j:j + 1, :]
            return out, sh

        gate, gsh = conv(g_ref, gp_ref, gn_ref, wg_ref)
        up, ush = conv(u_ref, up_ref, un_ref, wu_ref)
        dav = jnp.concatenate([da_ref[...].astype(F32), dan_ref[...].astype(F32)], axis=0)
        rows = lax.broadcasted_iota(jnp.int32, (tm + HALO, 1), 0)
        dav = jnp.where((rows < tm) | (i < nt - 1), dav, 0.0)
        sg = _sigmoid(gate)
        dgate = dav * up * (sg * (1.0 + gate * (1.0 - sg)))
        dup = dav * (gate * sg)

        def conv_t(dv, w_ref):
            out = dv[:tm] * w_ref[kf - 1:kf, :]
            for j in range(kf - 1):
                out = out + _shift_up(dv, kf - 1 - j, tm) * w_ref[j:j + 1, :]
            return out

        d_ref[0] = conv_t(dgate, wg_ref).astype(BF16)
        d_ref[1] = conv_t(dup, wu_ref).astype(BF16)
        for j in range(kf):
            dw_ref[0, j:j + 1, :] += jnp.sum(dgate[:tm] * gsh[j][:tm], axis=0, keepdims=True)
            dw_ref[1, j:j + 1, :] += jnp.sum(dup[:tm] * ush[j][:tm], axis=0, keepdims=True)

    pm = lambda off: (lambda j, i: _prev_map(tm, lambda c: c + off)(i, j))
    nm = lambda off: (lambda j, i: _next_map(tm, s, lambda c: c + off)(i, j))
    du, dw = _pc(
        body, name="ffn_act_bwd", grid=(nf, nt),
        in_specs=[pl.BlockSpec((tm, cb), lambda j, i: (i, j)), pl.BlockSpec((HALO, cb), nm(0)),
                  pl.BlockSpec((tm, cb), lambda j, i: (i, j)), pl.BlockSpec((HALO, cb), pm(0)),
                  pl.BlockSpec((HALO, cb), nm(0)),
                  pl.BlockSpec((tm, cb), lambda j, i: (i, j + nf)), pl.BlockSpec((HALO, cb), pm(nf)),
                  pl.BlockSpec((HALO, cb), nm(nf)),
                  pl.BlockSpec((kf, cb), lambda j, i: (0, j)), pl.BlockSpec((kf, cb), lambda j, i: (0, j + nf))],
        out_specs=[pl.BlockSpec((2, tm, cb), lambda j, i: (0, i, j)), pl.BlockSpec((2, 8, cb), lambda j, i: (0, 0, j))],
        out_shape=[jax.ShapeDtypeStruct((2, s, f), BF16), jax.ShapeDtypeStruct((2, 8, f), F32)],
        compiler_params=_cp(("parallel", "arbitrary")))(da, da, u_pre, u_pre, u_pre, u_pre, u_pre, u_pre, conv_w, conv_w)
    return du, dw[:, :kf]


def _adamw(w, g, m, v, name):
    shape = w.shape
    c = shape[-1]
    r = w.size // c
    tr = r if r * c <= 262144 else _tile(r, tuple(t for t in (512, 256, 128, 64, 32, 16, 8) if t * c <= 262144))
    bc1 = 1.0 - ADAM_B1 ** ADAM_STEP
    bc2 = 1.0 - ADAM_B2 ** ADAM_STEP

    def body(w_ref, g_ref, m_ref, v_ref, d_ref, nm_ref, nv_ref):
        gv = g_ref[...]
        mn = ADAM_B1 * m_ref[...] + (1.0 - ADAM_B1) * gv
        vn = ADAM_B2 * v_ref[...] + (1.0 - ADAM_B2) * (gv * gv)
        nm_ref[...] = mn
        nv_ref[...] = vn
        d_ref[...] = -ADAM_LR * ((mn / bc1) / (jnp.sqrt(vn / bc2) + ADAM_EPS) + ADAM_WD * w_ref[...])

    blk = pl.BlockSpec((tr, c), lambda i: (i, 0))
    outs = _pc(body, name=name, grid=(r // tr,), in_specs=[blk] * 4, out_specs=[blk] * 3,
               out_shape=[jax.ShapeDtypeStruct((r, c), F32)] * 3,
               compiler_params=_cp(("parallel",)))(*(t.reshape(r, c) for t in (w, g, m, v)))
    return tuple(o.reshape(shape) for o in outs)


def _adamw_sharded(w, m, v, partials, name):
    nl, r, c = w.shape
    tr = _tile(r, tuple(t for t in (256, 128, 64, 32, 16, 8) if t * c <= 131072))
    bc1 = 1.0 - ADAM_B1 ** ADAM_STEP
    bc2 = 1.0 - ADAM_B2 ** ADAM_STEP

    def body(w_ref, m_ref, v_ref, *rest):
        p_refs, (g_ref, d_ref, nm_ref, nv_ref) = rest[:nl], rest[nl:]
        layer = pl.program_id(0)
        for l in range(nl):
            @pl.when(layer == l)
            def _(p_ref=p_refs[l]):
                gv = p_ref[0].astype(F32)
                for dev in range(1, N_DEV):
                    gv = gv + p_ref[dev].astype(F32)
                mn = ADAM_B1 * m_ref[...] + (1.0 - ADAM_B1) * gv
                vn = ADAM_B2 * v_ref[...] + (1.0 - ADAM_B2) * (gv * gv)
                g_ref[...] = gv
                nm_ref[...] = mn
                nv_ref[...] = vn
                d_ref[...] = -ADAM_LR * ((mn / bc1) / (jnp.sqrt(vn / bc2) + ADAM_EPS) + ADAM_WD * w_ref[...])

    blk = pl.BlockSpec((None, tr, c), lambda l, i: (l, i, 0))
    p_specs = [pl.BlockSpec((N_DEV, tr, c), lambda l, i, k=k: (0, jnp.where(l == k, i, 0), 0)) for k in range(nl)]
    return _pc(body, name=name, grid=(nl, r // tr), in_specs=[blk] * 3 + p_specs, out_specs=[blk] * 4,
               out_shape=[jax.ShapeDtypeStruct((nl, r, c), F32)] * 4,
               compiler_params=_cp(("arbitrary", "arbitrary")))(w, m, v, *partials)


def _slot_sum(x, name):
    _, r, c = x.shape
    tr = _tile(r, (512, 256, 128, 64, 32, 16, 8))

    def body(x_ref, o_ref):
        acc = x_ref[0].astype(F32)
        for d in range(1, N_DEV):
            acc = acc + x_ref[d].astype(F32)
        o_ref[...] = acc

    return _pc(body, name=name, grid=(r // tr,), in_specs=[pl.BlockSpec((N_DEV, tr, c), lambda i: (0, i, 0))],
               out_specs=pl.BlockSpec((tr, c), lambda i: (i, 0)), out_shape=jax.ShapeDtypeStruct((r, c), F32),
               compiler_params=_cp(("parallel",)))(x)


def _all_gather(xs, name):
    na = len(xs)
    nl = xs[0].shape[0] * na

    def body(*refs):
        x_refs, out_refs = refs[:na], refs[na:na + nl]
        send_sems, recv_sems, local_sems = refs[na + nl:]
        mx, my, mc = lax.axis_index("x"), lax.axis_index("y"), lax.axis_index("c")
        me, sibling = (mx, my, mc), (mx, my, 1 - mc)
        chips = [(1 - mx, my), (mx, 1 - my), (1 - mx, 1 - my)]

        def local(l):
            return x_refs[l % na].at[l // na]

        def slot(l, dev):
            return out_refs[l].at[4 * dev[0] + 2 * dev[1] + dev[2]]

        def copy(l, k, block, to, src=None):
            return pltpu.make_async_remote_copy(
                src_ref=slot(l, block) if src is None else src, dst_ref=slot(l, block),
                send_sem=send_sems.at[l, k], recv_sem=recv_sems.at[l, k], device_id=to, device_id_type=MESH)

        mine = [pltpu.make_async_copy(local(l), slot(l, me), local_sems.at[l]) for l in range(nl)]
        first = []
        for l in range(nl):
            mine[l].start()
            first.append(copy(l, 0, me, sibling, src=local(l)))
            first += [copy(l, 1 + j, me, (*chip, mc), src=local(l)) for j, chip in enumerate(chips)]
        for cp in first:
            cp.start()
        passed = []
        for l in range(nl):
            for j, chip in enumerate(chips):
                copy(l, 1 + j, (*chip, mc), me).wait_recv()
                fw = copy(l, 4 + j, (*chip, mc), sibling)
                fw.start()
                passed.append(fw)
        for l in range(nl):
            copy(l, 0, sibling, me).wait_recv()
            for j, chip in enumerate(chips):
                copy(l, 4 + j, (*chip, 1 - mc), me).wait_recv()
        for cp in first + passed:
            cp.wait_send()
        for l in range(nl):
            mine[l].wait()

    anyspec = pl.BlockSpec(memory_space=pl.ANY)
    outs = _pc(body, name=name, in_specs=[anyspec] * na, out_specs=[anyspec] * nl,
               out_shape=[jax.ShapeDtypeStruct((N_DEV,) + xs[l % na].shape[1:], xs[l % na].dtype) for l in range(nl)],
               scratch_shapes=[pltpu.SemaphoreType.DMA((nl, 7)), pltpu.SemaphoreType.DMA((nl, 7)),
                               pltpu.SemaphoreType.DMA((nl,))],
               compiler_params=_cp())(*xs)
    return [outs[l * na:(l + 1) * na] for l in range(nl // na)]


def _scatter_blocks(xs, name):
    na = len(xs)

    def body(*refs):
        x_refs, out_refs = refs[:na], refs[na:2 * na]
        send_sems, recv_sems, local_sems = refs[2 * na:]
        mx, my, mc = lax.axis_index("x"), lax.axis_index("y"), lax.axis_index("c")
        me = 4 * mx + 2 * my + mc
        owns = [pltpu.make_async_copy(x_refs[a].at[me], out_refs[a].at[me], local_sems.at[a]) for a in range(na)]
        sends = []
        for a in range(na):
            owns[a].start()
            for k in range(1, N_DEV):
                px, py, pc = mx ^ (k >> 2), my ^ ((k >> 1) & 1), mc ^ (k & 1)
                peer = 4 * px + 2 * py + pc
                cp = pltpu.make_async_remote_copy(
                    src_ref=x_refs[a].at[peer], dst_ref=out_refs[a].at[me], send_sem=send_sems.at[a, k],
                    recv_sem=recv_sems.at[a, k], device_id=(px, py, pc), device_id_type=MESH)
                cp.start()
                sends.append((cp, a, k, peer))
        for cp, a, k, peer in sends:
            pltpu.make_async_remote_copy(
                src_ref=x_refs[a].at[peer], dst_ref=out_refs[a].at[peer], send_sem=send_sems.at[a, k],
                recv_sem=recv_sems.at[a, k], device_id=(mx, my, mc), device_id_type=MESH).wait_recv()
        for cp, _, _, _ in sends:
            cp.wait_send()
        for own in owns:
            own.wait()

    anyspec = pl.BlockSpec(memory_space=pl.ANY)
    return _pc(body, name=name, in_specs=[anyspec] * na, out_specs=[anyspec] * na,
               out_shape=[jax.ShapeDtypeStruct(x.shape, x.dtype) for x in xs],
               scratch_shapes=[pltpu.SemaphoreType.DMA((na, N_DEV)), pltpu.SemaphoreType.DMA((na, N_DEV)),
                               pltpu.SemaphoreType.DMA((na,))],
               compiler_params=_cp())(*xs)


def _mix_in_assemble(shards, width, nh):
    _, d, cs = shards.shape
    tr = _tile(d, (128, 64, 32, 16))
    w4 = 4 * width

    def body(s_ref, main_ref, gate_ref):
        full = jnp.concatenate([s_ref[j].astype(F32) for j in range(N_DEV)], axis=1)
        main_ref[:, :width] = full[:, 3 * width:w4].astype(BF16)
        main_ref[:, width:w4] = full[:, w4 + 2 * nh:].astype(BF16)
        main_ref[:, w4:] = full[:, :3 * width].astype(BF16)
        gate_ref[...] = jnp.concatenate([full[:, w4:w4 + 2 * nh], jnp.zeros((tr, 128 - 2 * nh), F32)],
                                        axis=1).astype(BF16)

    return _pc(body, name="mix_in_assemble", grid=(d // tr,),
               in_specs=[pl.BlockSpec((N_DEV, tr, cs), lambda i: (0, i, 0))],
               out_specs=[pl.BlockSpec((tr, 7 * width), lambda i: (i, 0)), pl.BlockSpec((tr, 128), lambda i: (i, 0))],
               out_shape=[jax.ShapeDtypeStruct((d, 7 * width), BF16), jax.ShapeDtypeStruct((d, 128), BF16)],
               compiler_params=_cp(("parallel",)))(shards)


def _mix_in_shards(dmain, dgate, width, nh):
    d = dmain.shape[0]
    cs = (7 * width + 2 * nh) // N_DEV
    tr = _tile(d, (128, 64, 32, 16))
    w4 = 4 * width

    def body(main_ref, gate_ref, o_ref):
        full = jnp.concatenate([main_ref[:, w4:], main_ref[:, :width], gate_ref[:, :2 * nh], main_ref[:, width:w4]],
                               axis=1)
        for j in range(N_DEV):
            o_ref[j] = full[:, j * cs:(j + 1) * cs].astype(BF16)

    return _pc(body, name="mix_in_shards", grid=(d // tr,),
               in_specs=[pl.BlockSpec((tr, 7 * width), lambda i: (i, 0)), pl.BlockSpec((tr, 128), lambda i: (i, 0))],
               out_specs=pl.BlockSpec((N_DEV, tr, cs), lambda i: (0, i, 0)),
               out_shape=jax.ShapeDtypeStruct((N_DEV, d, cs), BF16),
               compiler_params=_cp(("parallel",)))(dmain, dgate)


PACK_COLS = 1024
BIG = ("w_mix_in", "w_mix_out", "w_xq", "w_xk", "w_xv", "w_xo", "w_ffn_up", "w_ffn_down")


def _pack(blocks, lead):
    flat = jnp.concatenate(blocks, axis=-1)
    n = flat.shape[-1]
    per = 16 * PACK_COLS
    pad = (-n) % per
    flat = jnp.pad(flat, [(0, 0)] * len(lead) + [(0, pad)])
    return flat.reshape(*lead, (n + pad) // PACK_COLS, PACK_COLS)


def _unpack(packed, sizes):
    flat = packed.reshape(*packed.shape[:-2], -1)
    out, off = [], 0
    for n in sizes:
        out.append(flat[..., off:off + n])
        off += n
    return out


def kernel(x, mem, mix_norm, w_mix_in, gdn_conv, gdn_a_log, gdn_dt_bias, gdn_out_norm, sc_conv, w_mix_out, xattn_norm, mem_norm, w_xq, w_xk, w_xv, w_xo, ffn_norm, w_ffn_up, ffn_conv, w_ffn_down, final_norm, loss_target, m_mix_norm, m_w_mix_in, m_gdn_conv, m_gdn_a_log, m_gdn_dt_bias, m_gdn_out_norm, m_sc_conv, m_w_mix_out, m_xattn_norm, m_mem_norm, m_w_xq, m_w_xk, m_w_xv, m_w_xo, m_ffn_norm, m_w_ffn_up, m_ffn_conv, m_w_ffn_down, m_final_norm, v_mix_norm, v_w_mix_in, v_gdn_conv, v_gdn_a_log, v_gdn_dt_bias, v_gdn_out_norm, v_sc_conv, v_w_mix_out, v_xattn_norm, v_mem_norm, v_w_xq, v_w_xk, v_w_xv, v_w_xo, v_ffn_norm, v_w_ffn_up, v_ffn_conv, v_w_ffn_down, v_final_norm):
    names = ["mix_norm", "w_mix_in", "gdn_conv", "gdn_a_log", "gdn_dt_bias", "gdn_out_norm", "sc_conv", "w_mix_out",
             "xattn_norm", "mem_norm", "w_xq", "w_xk", "w_xv", "w_xo", "ffn_norm", "w_ffn_up", "ffn_conv",
             "w_ffn_down", "final_norm"]
    wts = dict(zip(names, (mix_norm, w_mix_in, gdn_conv, gdn_a_log, gdn_dt_bias, gdn_out_norm, sc_conv, w_mix_out,
                           xattn_norm, mem_norm, w_xq, w_xk, w_xv, w_xo, ffn_norm, w_ffn_up, ffn_conv, w_ffn_down,
                           final_norm)))
    mom1 = dict(zip(names, (m_mix_norm, m_w_mix_in, m_gdn_conv, m_gdn_a_log, m_gdn_dt_bias, m_gdn_out_norm, m_sc_conv,
                            m_w_mix_out, m_xattn_norm, m_mem_norm, m_w_xq, m_w_xk, m_w_xv, m_w_xo, m_ffn_norm,
                            m_w_ffn_up, m_ffn_conv, m_w_ffn_down, m_final_norm)))
    mom2 = dict(zip(names, (v_mix_norm, v_w_mix_in, v_gdn_conv, v_gdn_a_log, v_gdn_dt_bias, v_gdn_out_norm, v_sc_conv,
                            v_w_mix_out, v_xattn_norm, v_mem_norm, v_w_xq, v_w_xk, v_w_xv, v_w_xo, v_ffn_norm,
                            v_w_ffn_up, v_ffn_conv, v_w_ffn_down, v_final_norm)))

    x0 = x[0]
    memv = mem[0]
    target = loss_target[0]
    s, d = x0.shape
    depth = mix_norm.shape[0]
    width = d // 2
    nh = width // HEAD
    f = w_ffn_down.shape[1] * N_DEV
    me = 4 * lax.axis_index("x") + 2 * lax.axis_index("y") + lax.axis_index("c")

    gathered = [dict(zip(BIG, per_layer))
                for per_layer in _all_gather([wts[n].astype(BF16) for n in BIG], "all_gather_weights")]
    conv_names = ("gdn_conv", "sc_conv", "ffn_conv")
    conv_local = _pack([wts[n].reshape(1, -1) for n in conv_names], (1,))
    conv_all = _all_gather([conv_local], "all_gather_conv")[0][0]
    conv_parts = _unpack(conv_all, [wts[n].size for n in conv_names])
    conv_full = {}
    for n, part in zip(conv_names, conv_parts):
        _, kt, cs = wts[n].shape
        conv_full[n] = jnp.moveaxis(part.reshape(N_DEV, depth, kt, cs), 0, 2).reshape(depth, kt, N_DEV * cs)

    def layer_weights(l):
        wl = {n: gathered[l][n].reshape(-1, gathered[l][n].shape[-1]) for n in BIG if n not in ("w_mix_in", "w_ffn_up")}
        wl["w_ffn_up"] = gathered[l]["w_ffn_up"]
        return wl

    def gate_params(l):
        rows = jnp.stack([gdn_a_log[l], gdn_dt_bias[l]])
        return jnp.pad(rows, ((0, 6), (nh, 128 - 2 * nh)))

    saved = []
    xc = x0
    for l in range(depth):
        wl = layer_weights(l)
        w_main, w_gate = _mix_in_assemble(gathered[l]["w_mix_in"], width, nh)
        gp = gate_params(l)
        h1 = _rms_fwd(xc, mix_norm[l], "rms_mix")
        proj = _mm(h1, w_main, out_dtype=BF16, name="mm_mix_in")
        ba = _mm(h1, w_gate, out_dtype=F32, name="mm_mix_gates")
        qkv = _gdn_prep_fwd(proj, conv_full["gdn_conv"][l], width)
        o_raw, states, tinv = _gdn_fwd(qkv, ba, gp, width)
        y = _mix_post_fwd(o_raw, proj, gdn_out_norm[l], conv_full["sc_conv"][l], width)
        x1 = _mm(y, wl["w_mix_out"], res=xc, name="mm_mix_out")
        h2 = _rms_fwd(x1, xattn_norm[l], "rms_xattn")
        mem_n = _rms_fwd(memv, mem_norm[l], "rms_mem")
        qx = _mm(h2, wl["w_xq"], out_dtype=BF16, name="mm_xq")
        kx = _mm(mem_n, wl["w_xk"], out_dtype=BF16, name="mm_xk")
        vx = _mm(mem_n, wl["w_xv"], out_dtype=BF16, name="mm_xv")
        ox = _xattn_fwd(qx, kx, vx)
        x2 = _mm(ox, wl["w_xo"], res=x1, name="mm_xo")
        h3 = _rms_fwd(x2, ffn_norm[l], "rms_ffn")
        u_pre = _mm(h3, wl["w_ffn_up"], b_shards=True, out_dtype=BF16, name="mm_ffn_up")
        act = _ffn_act_fwd(u_pre, conv_full["ffn_conv"][l])
        x3 = _mm(act, wl["w_ffn_down"], res=x2, name="mm_ffn_down")
        saved.append(dict(x0=xc, x1=x1, x2=x2, h1=h1, h2=h2, h3=h3, proj=proj, ba=ba, qkv=qkv, o_raw=o_raw,
                          states=states, tinv=tinv, y=y, mem_n=mem_n, qx=qx, kx=kx, vx=vx, ox=ox, u_pre=u_pre, act=act,
                          w_main=w_main, w_gate=w_gate, gp=gp, wl=wl))
        xc = x3

    loss_part, dx, g_final = _loss_head(xc, final_norm, target)

    small = {n: [None] * depth for n in ("mix_norm", "xattn_norm", "mem_norm", "ffn_norm", "gdn_a_log", "gdn_dt_bias",
                                          "gdn_out_norm", "gdn_conv", "sc_conv", "ffn_conv")}
    partials = {n: [None] * depth for n in BIG}
    for l in reversed(range(depth)):
        sv = saved[l]
        wl, w_main, w_gate, gp = sv["wl"], sv["w_main"], sv["w_gate"], sv["gp"]
        big = {}
        d_act = _mm(dx, wl["w_ffn_down"], tb=True, out_dtype=BF16, name="mm_d_act")
        big["w_ffn_down"] = _mm(sv["act"], dx, ta=True, out_dtype=BF16, name="mm_dw_ffn_down")
        du, dcw = _ffn_act_bwd(d_act, sv["u_pre"], conv_full["ffn_conv"][l])
        small["ffn_conv"][l] = jnp.concatenate([dcw[0], dcw[1]], axis=1)
        big["w_ffn_up"] = _mm(sv["h3"], du, ta=True, b_halves=True, out_shards=True, out_dtype=BF16,
                              name="mm_dw_ffn_up")
        dh = _mm(du, wl["w_ffn_up"], tb=True, a_halves=True, b_shards=True, out_dtype=BF16, name="mm_dh3")
        dx, small["ffn_norm"][l] = _rms_bwd(sv["x2"], dh, ffn_norm[l], dx, "rms_bwd_ffn")
        d_ox = _mm(dx, wl["w_xo"], tb=True, out_dtype=BF16, name="mm_d_ox")
        big["w_xo"] = _mm(sv["ox"], dx, ta=True, out_dtype=BF16, name="mm_dw_xo")
        d_qx, d_kx, d_vx = _xattn_bwd(sv["qx"], sv["kx"], sv["vx"], d_ox)
        big["w_xq"] = _mm(sv["h2"], d_qx, ta=True, out_dtype=BF16, name="mm_dw_xq")
        big["w_xk"] = _mm(sv["mem_n"], d_kx, ta=True, out_dtype=BF16, name="mm_dw_xk")
        big["w_xv"] = _mm(sv["mem_n"], d_vx, ta=True, out_dtype=BF16, name="mm_dw_xv")
        dh = _mm(d_qx, wl["w_xq"], tb=True, out_dtype=BF16, name="mm_dh2")
        d_mem = _mm(d_kx, wl["w_xk"], tb=True, out_dtype=F32, name="mm_dmem_k")
        d_mem = _mm(d_vx, wl["w_xv"], tb=True, out_dtype=F32, res=d_mem, name="mm_dmem_v")
        _, small["mem_norm"][l] = _rms_bwd(memv, d_mem, mem_norm[l], jnp.zeros_like(memv), "rms_bwd_mem")
        dx, small["xattn_norm"][l] = _rms_bwd(sv["x1"], dh, xattn_norm[l], dx, "rms_bwd_xattn")
        d_y = _mm(dx, wl["w_mix_out"], tb=True, out_dtype=BF16, name="mm_d_y")
        big["w_mix_out"] = _mm(sv["y"], dx, ta=True, out_dtype=BF16, name="mm_dw_mix_out")
        d_o, d_proj, small["gdn_out_norm"][l], small["sc_conv"][l] = _mix_post_bwd(
            d_y, sv["o_raw"], sv["proj"], gdn_out_norm[l], conv_full["sc_conv"][l], width)
        d_qkv, d_gb = _gdn_bwd(sv["qkv"], sv["ba"], gp, d_o, sv["states"], sv["tinv"], width)
        d_ba, d_gp = _gates_bwd(sv["ba"], d_gb, gp, nh)
        small["gdn_a_log"][l] = d_gp[0, nh:2 * nh]
        small["gdn_dt_bias"][l] = d_gp[1, nh:2 * nh]
        d_proj, small["gdn_conv"][l] = _gdn_prep_bwd(d_qkv, sv["proj"], conv_full["gdn_conv"][l], d_proj, width)
        dw_main = _mm(sv["h1"], d_proj, ta=True, name="mm_dw_mix_in")
        dw_gate = _mm(sv["h1"], d_ba, ta=True, name="mm_dw_mix_gates")
        big["w_mix_in"] = _mix_in_shards(dw_main, dw_gate, width, nh)
        dh = _mm(d_proj, w_main, tb=True, out_dtype=F32, name="mm_dh1_main")
        dh = _mm(d_ba, w_gate, tb=True, out_dtype=BF16, res=dh, name="mm_dh1_gates")
        dx, small["mix_norm"][l] = _rms_bwd(sv["x0"], dh, mix_norm[l], dx, "rms_bwd_mix")
        outgoing = [big[n].reshape((N_DEV,) + wts[n].shape[1:]) for n in BIG]
        for n, got in zip(BIG, _scatter_blocks(outgoing, "scatter_grads")):
            partials[n][l] = got

    small_names = ("mix_norm", "xattn_norm", "mem_norm", "ffn_norm", "gdn_a_log", "gdn_dt_bias", "gdn_out_norm",
                   "gdn_conv", "sc_conv", "ffn_conv")
    small_parts = [jnp.stack(small[n]).reshape(1, -1) for n in small_names]
    small_parts += [g_final.reshape(1, -1), loss_part.reshape(1, 1)]
    small_sizes = [p.shape[1] for p in small_parts]
    small_local = _pack(small_parts, (1,))
    small_sum = _slot_sum(_all_gather([small_local], "all_gather_small")[0][0], "sum_small")
    small_tot = _unpack(small_sum, small_sizes)
    grads = {}
    for n, g in zip(small_names, small_tot[:len(small_names)]):
        if n in conv_names:
            _, kt, cs = wts[n].shape
            g = lax.dynamic_slice_in_dim(g.reshape(depth, kt, N_DEV * cs), me * cs, cs, axis=2)
        grads[n] = g.reshape(wts[n].shape)
    grads["final_norm"] = small_tot[-2].reshape(final_norm.shape)
    loss = small_tot[-1].reshape(())

    delta, new_m, new_v = {}, {}, {}
    for n in names:
        if n in BIG:
            grads[n], delta[n], new_m[n], new_v[n] = _adamw_sharded(wts[n], mom1[n], mom2[n], partials[n], "adamw_" + n)
        else:
            delta[n], new_m[n], new_v[n] = _adamw(wts[n], grads[n], mom1[n], mom2[n], "adamw_" + n)
    return (loss, dx[None], *[grads[n] for n in names], *[delta[n] for n in names],
            *[new_m[n] for n in names], *[new_v[n] for n in names])
```

```python
import functools

import jax
import jax.numpy as jnp
from jax import lax
from jax.experimental import pallas as pl
from jax.experimental.pallas import tpu as pltpu

F32 = jnp.float32
BF16 = jnp.bfloat16
CHUNK = 64
HEAD = 128
XHEADS = 4
GDN_K = 4
EPS = 1e-6
HALO = 16
QKV_COL = 4
N_DEV = 8
VMEM_LIMIT = 56 * 1024 * 1024
ADAM_LR, ADAM_B1, ADAM_B2, ADAM_EPS, ADAM_WD, ADAM_STEP = 0.001, 0.9, 0.999, 1e-08, 0.01, 10
MESH = pl.DeviceIdType.MESH


def _pc(body, **kw):
    return pl.pallas_call(body, **kw)


def _cp(sem=None, **kw):
    if sem is not None:
        kw["dimension_semantics"] = sem
    return pltpu.CompilerParams(vmem_limit_bytes=VMEM_LIMIT, **kw)


def _tile(n, cands):
    for c in cands:
        if n % c == 0:
            return c
    return n


def _sigmoid(x):
    return 1.0 / (1.0 + jnp.exp(-x))


def _softplus(x):
    return jnp.maximum(x, 0.0) + jnp.log(1.0 + jnp.exp(-jnp.abs(x)))


def _mmb(a, b):
    return jnp.dot(a.astype(BF16), b.astype(BF16), preferred_element_type=F32)


def _mmb_nt(a, b):
    return lax.dot_general(a.astype(BF16), b.astype(BF16), (((1,), (1,)), ((), ())), preferred_element_type=F32)


def _mmb_tn(a, b):
    return lax.dot_general(a.astype(BF16), b.astype(BF16), (((0,), (0,)), ((), ())), preferred_element_type=F32)


def _split(x):
    hi = x.astype(BF16)
    return hi, x - hi.astype(F32)


def _mm3(a, b):
    ah, ar = _split(a)
    bh, br = _split(b)
    al, bl = ar.astype(BF16), br.astype(BF16)
    return (jnp.dot(ah, bh, preferred_element_type=F32)
            + (jnp.dot(ah, bl, preferred_element_type=F32) + jnp.dot(al, bh, preferred_element_type=F32)))


def _mm_exact_lhs(a, b):
    ab = a.astype(BF16)
    b1, r1 = _split(b)
    b2, r2 = _split(r1)
    return (jnp.dot(ab, b1, preferred_element_type=F32)
            + (jnp.dot(ab, b2, preferred_element_type=F32) + jnp.dot(ab, r2.astype(BF16), preferred_element_type=F32)))


def _shift_down(cur, prev, s):
    if s == 0:
        return cur
    r = pltpu.roll(cur, s, 0)
    p = pltpu.roll(prev, s, 0)
    rows = lax.broadcasted_iota(jnp.int32, prev.shape, 0)
    first = jnp.where(rows < s, p, r[:HALO])
    return jnp.concatenate([first, r[HALO:]], axis=0)


def _shift_up(ext, s, tm):
    if s == 0:
        return ext[:tm]
    return pltpu.roll(ext, ext.shape[0] - s, 0)[:tm]


def _prev_map(tm, col):
    return lambda i, j: (jnp.maximum(i * (tm // HALO) - 1, 0), col(j))


def _next_map(tm, nrows, col):
    return lambda i, j: (jnp.minimum((i + 1) * (tm // HALO), nrows // HALO - 1), col(j))


def _carry_parts(carry):
    if carry is None:
        return [], [], [], [], []
    _, srcs, _ = carry
    na = len(srcs)
    anyspec = pl.BlockSpec(memory_space=pl.ANY)
    return (list(srcs), [anyspec] * na, [jax.ShapeDtypeStruct((N_DEV,) + s.shape[1:], s.dtype) for s in srcs],
            [anyspec] * na, [pltpu.SemaphoreType.DMA((na, N_DEV)), pltpu.SemaphoreType.DMA((na, N_DEV))])


def _carry_run(carry, src_refs, dst_refs, send_sems, recv_sems, first, last):
    if carry is None:
        return
    kind, _, layer = carry
    mx, my, mc = lax.axis_index("x"), lax.axis_index("y"), lax.axis_index("c")
    me = 4 * mx + 2 * my + mc

    def descriptors(with_recvs):
        sends, recvs = [], []
        for a in range(len(dst_refs)):
            for k in range(N_DEV):
                px, py, pc = mx ^ (k >> 2), my ^ ((k >> 1) & 1), mc ^ (k & 1)
                peer = 4 * px + 2 * py + pc
                src = src_refs[a].at[peer] if kind == "scatter" else src_refs[a].at[layer]
                if k == 0:
                    sends.append(pltpu.make_async_copy(src, dst_refs[a].at[me], send_sems.at[a, 0]))
                    continue
                sends.append(pltpu.make_async_remote_copy(
                    src_ref=src, dst_ref=dst_refs[a].at[me], send_sem=send_sems.at[a, k], recv_sem=recv_sems.at[a, k],
                    device_id=(px, py, pc), device_id_type=MESH))
                if with_recvs:
                    recvs.append(pltpu.make_async_remote_copy(
                        src_ref=src, dst_ref=dst_refs[a].at[peer], send_sem=send_sems.at[a, k],
                        recv_sem=recv_sems.at[a, k], device_id=(mx, my, mc), device_id_type=MESH))
        return sends, recvs

    @pl.when(first)
    def _():
        for cp in descriptors(False)[0]:
            cp.start()

    @pl.when(last)
    def _():
        sends, recvs = descriptors(True)
        for cp in recvs:
            cp.wait_recv()
        for i, cp in enumerate(sends):
            if i % N_DEV == 0:
                cp.wait()
            else:
                cp.wait_send()


def _mm(a, b, *, ta=False, tb=False, out_dtype=F32, res=None, name, a_halves=False, b_shards=False,
        b_halves=False, out_shards=False, carry=None):
    if a_halves:
        m, k = a.shape[1], 2 * a.shape[2]
    else:
        m, k = (a.shape[1], a.shape[0]) if ta else a.shape
    if b_shards:
        cs = b.shape[2]
        n = b.shape[1] if tb else N_DEV * cs
    elif b_halves:
        n = 2 * b.shape[2]
        cs = n // N_DEV
    else:
        n = b.shape[0] if tb else b.shape[1]
        cs = None
    tm = _tile(m, (1024, 512, 256, 128))
    tn = cs if (cs is not None and not tb) else _tile(n, (1024, 512, 256, 128))
    tk = cs if (b_shards and tb) else _tile(k, (512, 256, 128))
    nk = k // tk
    dn = (((0 if ta else 1,), (1 if tb else 0,)), ((), ()))

    c_args, c_in, c_shapes, c_out, c_sems = _carry_parts(carry)
    nca = len(c_args)
    grid = (m // tm, n // tn, nk)

    def body(a_ref, b_ref, *rest):
        if res is not None:
            r_ref, rest = rest[0], rest[1:]
        src_refs, o_ref, dst_refs, acc = rest[:nca], rest[nca], rest[nca + 1:2 * nca + 1], rest[2 * nca + 1]
        kk = pl.program_id(2)
        if carry is not None:
            ii, jj = pl.program_id(0), pl.program_id(1)
            _carry_run(carry, src_refs, dst_refs, rest[-2], rest[-1], (ii == 0) & (jj == 0) & (kk == 0),
                       (ii == grid[0] - 1) & (jj == grid[1] - 1) & (kk == nk - 1))

        @pl.when(kk == 0)
        def _():
            acc[...] = jnp.zeros_like(acc)

        acc[...] += lax.dot_general(a_ref[...].astype(BF16), b_ref[...].astype(BF16), dn,
                                    preferred_element_type=F32)

        @pl.when(kk == nk - 1)
        def _():
            if res is None:
                o_ref[...] = acc[...].astype(out_dtype)
            else:
                o_ref[...] = (acc[...] + r_ref[...].astype(F32)).astype(out_dtype)

    if a_halves:
        per = (k // 2) // tk
        a_spec = pl.BlockSpec((None, tm, tk), lambda i, j, kk: (kk // per, i, kk % per))
    elif ta:
        a_spec = pl.BlockSpec((tk, tm), lambda i, j, kk: (kk, i))
    else:
        a_spec = pl.BlockSpec((tm, tk), lambda i, j, kk: (i, kk))
    if b_shards and tb:
        b_spec = pl.BlockSpec((None, tn, tk), lambda i, j, kk: (kk, j, 0))
    elif b_shards:
        b_spec = pl.BlockSpec((None, tk, tn), lambda i, j, kk: (j, kk, 0))
    elif b_halves:
        perb = (n // 2) // tn
        b_spec = pl.BlockSpec((None, tk, tn), lambda i, j, kk: (j // perb, kk, j % perb))
    elif tb:
        b_spec = pl.BlockSpec((tn, tk), lambda i, j, kk: (j, kk))
    else:
        b_spec = pl.BlockSpec((tk, tn), lambda i, j, kk: (kk, j))
    if out_shards:
        o_spec = pl.BlockSpec((None, tm, tn), lambda i, j, kk: (j, i, 0))
        o_shape = jax.ShapeDtypeStruct((N_DEV, m, tn), out_dtype)
    else:
        o_spec = pl.BlockSpec((tm, tn), lambda i, j, kk: (i, j))
        o_shape = jax.ShapeDtypeStruct((m, n), out_dtype)
    in_specs = [a_spec, b_spec] + ([o_spec] if res is not None else []) + c_in
    args = (a, b) + ((res,) if res is not None else ()) + tuple(c_args)
    sem = ("parallel", "parallel", "arbitrary") if carry is None else ("arbitrary",) * 3
    outs = _pc(body, name=name, grid=grid, in_specs=in_specs, out_specs=[o_spec] + c_out,
               out_shape=[o_shape] + c_shapes, scratch_shapes=[pltpu.VMEM((tm, tn), F32)] + c_sems,
               compiler_params=_cp(sem))(*args)
    return outs[0] if carry is None else (outs[0], outs[1:])


def _rms_fwd(x, w, name):
    s, d = x.shape
    tm = _tile(s, (512, 256, 128, 64))

    def body(x_ref, w_ref, o_ref):
        xv = x_ref[...]
        r = lax.rsqrt(jnp.mean(xv * xv, axis=-1, keepdims=True) + EPS)
        o_ref[...] = (xv * r * w_ref[...]).astype(BF16)

    return _pc(body, name=name, grid=(s // tm,),
               in_specs=[pl.BlockSpec((tm, d), lambda i: (i, 0)), pl.BlockSpec((1, d), lambda i: (0, 0))],
               out_specs=pl.BlockSpec((tm, d), lambda i: (i, 0)), out_shape=jax.ShapeDtypeStruct((s, d), BF16),
               compiler_params=_cp(("parallel",)))(x, w.reshape(1, d))


def _rms_bwd(x, dh, w, dx_in, name):
    s, d = x.shape
    tm = _tile(s, (256, 128, 64))

    def body(x_ref, dh_ref, w_ref, dxi_ref, dx_ref, dxb_ref, dg_ref):
        @pl.when(pl.program_id(0) == 0)
        def _():
            dg_ref[...] = jnp.zeros_like(dg_ref)

        xv = x_ref[...]
        dy = dh_ref[...].astype(F32)
        r = lax.rsqrt(jnp.mean(xv * xv, axis=-1, keepdims=True) + EPS)
        xh = xv * r
        dxh = dy * w_ref[...]
        dx = dxi_ref[...] + r * (dxh - xh * jnp.mean(dxh * xh, axis=-1, keepdims=True))
        dx_ref[...] = dx
        dxb_ref[...] = dx.astype(BF16)
        dg_ref[0:1, :] += jnp.sum(dy * xh, axis=0, keepdims=True)

    row = pl.BlockSpec((tm, d), lambda i: (i, 0))
    dx, dxb, dg = _pc(body, name=name, grid=(s // tm,),
                      in_specs=[row, row, pl.BlockSpec((1, d), lambda i: (0, 0)), row],
                      out_specs=[row, row, pl.BlockSpec((8, d), lambda i: (0, 0))],
                      out_shape=[jax.ShapeDtypeStruct((s, d), F32), jax.ShapeDtypeStruct((s, d), BF16),
                                 jax.ShapeDtypeStruct((8, d), F32)],
                      compiler_params=_cp(("arbitrary",)))(x, dh, w.reshape(1, d), dx_in)
    return dx, dxb, dg[0]


def _loss_head(x, w, target):
    s, d = x.shape
    tm = _tile(s, (256, 128, 64))

    def body(x_ref, w_ref, t_ref, dx_ref, dxb_ref, dg_ref, l_ref):
        @pl.when(pl.program_id(0) == 0)
        def _():
            dg_ref[...] = jnp.zeros_like(dg_ref)
            l_ref[...] = jnp.zeros_like(l_ref)

        xv = x_ref[...]
        r = lax.rsqrt(jnp.mean(xv * xv, axis=-1, keepdims=True) + EPS)
        xh = xv * r
        err = xh * w_ref[...] - t_ref[...]
        l_ref[...] += 0.5 * jnp.sum(jnp.mean(err * err, axis=-1, keepdims=True), axis=0, keepdims=True)
        dy = err * (1.0 / d)
        dxh = dy * w_ref[...]
        dx = r * (dxh - xh * jnp.mean(dxh * xh, axis=-1, keepdims=True))
        dx_ref[...] = dx
        dxb_ref[...] = dx.astype(BF16)
        dg_ref[0:1, :] += jnp.sum(dy * xh, axis=0, keepdims=True)

    row = pl.BlockSpec((tm, d), lambda i: (i, 0))
    dx, dxb, dg, ls = _pc(body, name="loss_head", grid=(s // tm,),
                          in_specs=[row, pl.BlockSpec((1, d), lambda i: (0, 0)), row],
                          out_specs=[row, row, pl.BlockSpec((8, d), lambda i: (0, 0)),
                                     pl.BlockSpec((8, 128), lambda i: (0, 0))],
                          out_shape=[jax.ShapeDtypeStruct((s, d), F32), jax.ShapeDtypeStruct((s, d), BF16),
                                     jax.ShapeDtypeStruct((8, d), F32), jax.ShapeDtypeStruct((8, 128), F32)],
                          compiler_params=_cp(("arbitrary",)))(x, w.reshape(1, d), target)
    return ls[0, 0], dx, dxb, dg[0]


def _gdn_prep_fwd(proj, conv_w, width):
    s = proj.shape[0]
    tm = _tile(s, (256, 128, 64))
    nh = width // HEAD

    def body(c_ref, p_ref, w_ref, o_ref):
        i, seg = pl.program_id(0), pl.program_id(1)
        cur = c_ref[...].astype(F32)
        prev = jnp.where(i > 0, p_ref[...].astype(F32), 0.0)
        pre = cur * w_ref[GDN_K - 1:GDN_K, :]
        for j in range(GDN_K - 1):
            pre = pre + _shift_down(cur, prev, GDN_K - 1 - j) * w_ref[j:j + 1, :]
        act = pre * _sigmoid(pre)
        scale = jnp.where(seg == 0, HEAD ** -0.5, 1.0)
        for h in range(nh):
            a = act[:, h * HEAD:(h + 1) * HEAD]
            rs = lax.rsqrt(jnp.sum(a * a, axis=-1, keepdims=True) + EPS) * scale
            o_ref[:, h * HEAD:(h + 1) * HEAD] = a * jnp.where(seg < 2, rs, 1.0)

    return _pc(body, name="gdn_prep_fwd", grid=(s // tm, 3),
               in_specs=[pl.BlockSpec((tm, width), lambda i, j: (i, j + QKV_COL)),
                         pl.BlockSpec((HALO, width), _prev_map(tm, lambda j: j + QKV_COL)),
                         pl.BlockSpec((GDN_K, width), lambda i, j: (0, j))],
               out_specs=pl.BlockSpec((tm, width), lambda i, j: (i, j)),
               out_shape=jax.ShapeDtypeStruct((s, 3 * width), F32),
               compiler_params=_cp(("parallel", "parallel")))(proj, proj, conv_w)


def _gdn_prep_bwd(dqkv, proj, conv_w, dproj_in, width):
    s = proj.shape[0]
    tm = _tile(s, (256, 128, 64))
    nh = width // HEAD
    nt = s // tm

    def body(c_ref, p_ref, n_ref, d_ref, dn_ref, w_ref, _, o_ref, dw_ref):
        seg, i = pl.program_id(0), pl.program_id(1)

        @pl.when(i == 0)
        def _():
            dw_ref[...] = jnp.zeros_like(dw_ref)

        ext = jnp.concatenate([c_ref[...].astype(F32), n_ref[...].astype(F32)], axis=0)
        prev = jnp.where(i > 0, p_ref[...].astype(F32), 0.0)
        sh = [_shift_down(ext, prev, GDN_K - 1 - j) for j in range(GDN_K)]
        pre = sh[0] * w_ref[0:1, :]
        for j in range(1, GDN_K):
            pre = pre + sh[j] * w_ref[j:j + 1, :]
        sg = _sigmoid(pre)
        act = pre * sg
        dout = jnp.concatenate([d_ref[...], dn_ref[...]], axis=0)
        rows = lax.broadcasted_iota(jnp.int32, (tm + HALO, 1), 0)
        dout = jnp.where((rows < tm) | (i < nt - 1), dout, 0.0)
        scale = jnp.where(seg == 0, HEAD ** -0.5, 1.0)
        parts = []
        for h in range(nh):
            a = act[:, h * HEAD:(h + 1) * HEAD]
            dq = dout[:, h * HEAD:(h + 1) * HEAD]
            rs = lax.rsqrt(jnp.sum(a * a, axis=-1, keepdims=True) + EPS)
            nrm = a * rs
            dn = dq * scale
            da_norm = rs * (dn - nrm * jnp.sum(dn * nrm, axis=-1, keepdims=True))
            parts.append(jnp.where(seg < 2, da_norm, dq))
        dact = jnp.concatenate(parts, axis=1)
        dpre = dact * (sg * (1.0 + pre * (1.0 - sg)))
        dp = _shift_up(dpre, 0, tm) * w_ref[GDN_K - 1:GDN_K, :]
        for j in range(GDN_K - 1):
            dp = dp + _shift_up(dpre, GDN_K - 1 - j, tm) * w_ref[j:j + 1, :]
        o_ref[...] = dp.astype(BF16)
        for j in range(GDN_K):
            dw_ref[j:j + 1, :] += jnp.sum(dpre[:tm] * sh[j][:tm], axis=0, keepdims=True)

    dproj, dw = _pc(body, name="gdn_prep_bwd", grid=(3, nt),
                    in_specs=[pl.BlockSpec((tm, width), lambda j, i: (i, j + QKV_COL)),
                              pl.BlockSpec((HALO, width), lambda j, i: _prev_map(tm, lambda c: c + QKV_COL)(i, j)),
                              pl.BlockSpec((HALO, width), lambda j, i: _next_map(tm, s, lambda c: c + QKV_COL)(i, j)),
                              pl.BlockSpec((tm, width), lambda j, i: (i, j)),
                              pl.BlockSpec((HALO, width), lambda j, i: _next_map(tm, s, lambda c: c)(i, j)),
                              pl.BlockSpec((GDN_K, width), lambda j, i: (0, j)),
                              pl.BlockSpec(memory_space=pl.ANY)],
                    out_specs=[pl.BlockSpec((tm, width), lambda j, i: (i, j + QKV_COL)),
                               pl.BlockSpec((8, width), lambda j, i: (0, j))],
                    out_shape=[jax.ShapeDtypeStruct(dproj_in.shape, BF16), jax.ShapeDtypeStruct((8, 3 * width), F32)],
                    input_output_aliases={6: 0},
                    compiler_params=_cp(("parallel", "arbitrary")))(proj, proj, proj, dqkv, dqkv, conv_w, dproj_in)
    return dproj, dw[:GDN_K]


def _chunk_common(bav, gp_ref, nh):
    g_full = -jnp.exp(gp_ref[0:1, :]) * _softplus(bav + gp_ref[1:2, :])
    beta_full = _sigmoid(bav)
    ri = lax.broadcasted_iota(jnp.int32, (CHUNK, CHUNK), 0)
    ci = lax.broadcasted_iota(jnp.int32, (CHUNK, CHUNK), 1)
    gc_full = _mm_exact_lhs(ri >= ci, g_full)
    gc_t = gc_full.T
    return beta_full, gc_full, gc_t, ri, ci


def _head_gates(h, nh, beta_full, gc_full, gc_t, ri, ci):
    bcol = beta_full[:, h:h + 1]
    gcol = gc_full[:, nh + h:nh + h + 1]
    grow = gc_t[nh + h:nh + h + 1, :]
    dec = jnp.exp(jnp.where(ri >= ci, gcol - grow, -1e30))
    ecol = jnp.exp(gcol)
    gl = gcol[CHUNK - 1:CHUNK, :]
    return bcol, gcol, dec, ecol, gl


def _gdn_fwd(qkv, ba, gp, width, carry=None):
    s = qkv.shape[0]
    nh = width // HEAD
    nc = s // CHUNK
    heads = range(nh)
    c_args, c_in, c_shapes, c_out, c_sems = _carry_parts(carry)
    nca = len(c_args)

    def body(q_ref, k_ref, v_ref, ba_ref, gp_ref, *rest):
        src_refs, (o_ref, st_ref, t_ref), dst_refs = rest[:nca], rest[nca:nca + 3], rest[nca + 3:2 * nca + 3]
        state = rest[2 * nca + 3]
        n = pl.program_id(0)
        if carry is not None:
            _carry_run(carry, src_refs, dst_refs, rest[-2], rest[-1], n == 0, n == nc - 1)

        @pl.when(n == 0)
        def _():
            state[...] = jnp.zeros_like(state)

        beta_full, gc_full, gc_t, ri, ci = _chunk_common(ba_ref[...], gp_ref, nh)
        eye = (ri == ci).astype(F32)
        sls = [slice(h * HEAD, (h + 1) * HEAD) for h in heads]
        q = [q_ref[:, sl] for sl in sls]
        k = [k_ref[:, sl] for sl in sls]
        v = [v_ref[:, sl] for sl in sls]
        st = [state[h] for h in heads]
        gates = [_head_gates(h, nh, beta_full, gc_full, gc_t, ri, ci) for h in heads]
        bcol, gcol, dec, ecol, gl = (list(z) for z in zip(*gates))
        kb = [k[h] * bcol[h] for h in heads]
        a = [jnp.where(ri > ci, _mmb_nt(kb[h], k[h]) * dec[h], 0.0) for h in heads]
        attn = [jnp.where(ri >= ci, _mmb_nt(q[h], k[h]) * dec[h], 0.0) for h in heads]
        t = [eye - a[h] for h in heads]
        pw = [_mm3(a[h], a[h]) for h in heads]
        for _ in range(4):
            both = [_mm3(jnp.concatenate([t[h], pw[h]], axis=0), pw[h]) for h in heads]
            t = [t[h] + both[h][:CHUNK] for h in heads]
            pw = [both[h][CHUNK:] for h in heads]
        t = [t[h] + _mm3(t[h], pw[h]) for h in heads]
        uw = [_mmb(t[h], jnp.concatenate([v[h] * bcol[h], kb[h] * ecol[h]], axis=1)) for h in heads]
        vn = [uw[h][:, :HEAD] - _mmb(uw[h][:, HEAD:], st[h]) for h in heads]
        out = [_mmb(q[h] * ecol[h], st[h]) + _mmb(attn[h], vn[h]) for h in heads]
        new = [st[h] * jnp.exp(gl[h]) + _mmb_tn(k[h] * jnp.exp(gl[h] - gcol[h]), vn[h]) for h in heads]
        for h in heads:
            o_ref[:, sls[h]] = out[h]
            st_ref[0, h] = st[h]
            t_ref[0, h] = t[h]
            state[h] = new[h]

    blk = lambda c: pl.BlockSpec((CHUNK, width), lambda n, c=c: (n, c))
    outs = _pc(body, name="gdn_fwd", grid=(nc,),
               in_specs=[blk(0), blk(1), blk(2), pl.BlockSpec((CHUNK, 128), lambda n: (n, 0)),
                         pl.BlockSpec((8, 128), lambda n: (0, 0))] + c_in,
               out_specs=[blk(0), pl.BlockSpec((1, nh, HEAD, HEAD), lambda n: (n, 0, 0, 0)),
                          pl.BlockSpec((1, nh, CHUNK, CHUNK), lambda n: (n, 0, 0, 0))] + c_out,
               out_shape=[jax.ShapeDtypeStruct((s, width), F32), jax.ShapeDtypeStruct((nc, nh, HEAD, HEAD), F32),
                          jax.ShapeDtypeStruct((nc, nh, CHUNK, CHUNK), F32)] + c_shapes,
               scratch_shapes=[pltpu.VMEM((nh, HEAD, HEAD), F32)] + c_sems,
               compiler_params=_cp(("arbitrary",)))(qkv, qkv, qkv, ba, gp, *c_args)
    return outs[0], outs[1], outs[2], outs[3:]


def _gdn_bwd(qkv, ba, gp, do, states, tinv, width, carry=None):
    s = qkv.shape[0]
    nh = width // HEAD
    nc = s // CHUNK
    heads = range(nh)
    c_args, c_in, c_shapes, c_out, c_sems = _carry_parts(carry)
    nca = len(c_args)

    def body(q_ref, k_ref, v_ref, ba_ref, gp_ref, do_ref, st_ref, t_ref, *rest):
        src_refs, (dqkv_ref, dgb_ref), dst_refs = rest[:nca], rest[nca:nca + 2], rest[nca + 2:2 * nca + 2]
        dstate = rest[2 * nca + 2]
        n = pl.program_id(0)
        if carry is not None:
            _carry_run(carry, src_refs, dst_refs, rest[-2], rest[-1], n == 0, n == nc - 1)

        @pl.when(n == 0)
        def _():
            dstate[...] = jnp.zeros_like(dstate)

        beta_full, gc_full, gc_t, ri, ci = _chunk_common(ba_ref[...], gp_ref, nh)
        lane = lax.broadcasted_iota(jnp.int32, (CHUNK, 128), 1)
        rowi = lax.broadcasted_iota(jnp.int32, (CHUNK, 1), 0)
        low, strict = ri >= ci, ri > ci
        each = lambda fn: [fn(h) for h in heads]
        rowsum = lambda x: jnp.sum(x, axis=1, keepdims=True)
        sls = each(lambda h: slice(h * HEAD, (h + 1) * HEAD))
        q, k, v = each(lambda h: q_ref[:, sls[h]]), each(lambda h: k_ref[:, sls[h]]), each(lambda h: v_ref[:, sls[h]])
        dout = each(lambda h: do_ref[:, sls[h]])
        st, t, dsp = each(lambda h: st_ref[0, h]), each(lambda h: t_ref[0, h]), each(lambda h: dstate[h])
        gates = each(lambda h: _head_gates(h, nh, beta_full, gc_full, gc_t, ri, ci))
        bcol, gcol, dec, ecol, gl = (list(z) for z in zip(*gates))
        el = each(lambda h: jnp.exp(gl[h]))
        kdsc = each(lambda h: jnp.exp(gl[h] - gcol[h]))
        kb = each(lambda h: k[h] * bcol[h])
        a = each(lambda h: jnp.where(strict, _mmb_nt(kb[h], k[h]) * dec[h], 0.0))
        attn = each(lambda h: jnp.where(low, _mmb_nt(q[h], k[h]) * dec[h], 0.0))
        uw = each(lambda h: _mmb(t[h], jnp.concatenate([v[h] * bcol[h], kb[h] * ecol[h]], axis=1)))
        w = each(lambda h: uw[h][:, HEAD:])
        kd = each(lambda h: k[h] * kdsc[h])
        vn = each(lambda h: uw[h][:, :HEAD] - _mmb(w[h], st[h]))
        d_attn = each(lambda h: jnp.where(low, _mmb_nt(dout[h], vn[h]), 0.0))
        d_vn = each(lambda h: _mmb_tn(attn[h], dout[h]) + _mmb(kd[h], dsp[h]))
        d_qd = each(lambda h: _mmb_nt(dout[h], st[h]))
        d_kd = each(lambda h: _mmb_nt(vn[h], dsp[h]))
        d_el = each(lambda h: jnp.sum(rowsum(st[h] * dsp[h]), axis=0, keepdims=True))
        dst_new = each(lambda h: _mmb_tn(q[h] * ecol[h], dout[h]) + el[h] * dsp[h] - _mmb_tn(w[h], d_vn[h]))
        d_w = each(lambda h: -_mmb_nt(d_vn[h], st[h]))
        dr = each(lambda h: _mmb_tn(t[h], jnp.concatenate([d_vn[h], d_w[h]], axis=1)))
        dru, drw = each(lambda h: dr[h][:, :HEAD]), each(lambda h: dr[h][:, HEAD:])
        d_a = each(lambda h: -jnp.where(strict, _mmb_nt(dr[h], uw[h]), 0.0))
        d_kk = each(lambda h: d_a[h] * dec[h])
        d_qk = each(lambda h: d_attn[h] * dec[h])
        d_kb = each(lambda h: _mmb(d_kk[h], k[h]) + drw[h] * ecol[h])
        dk = each(lambda h: _mmb_tn(d_kk[h], kb[h]) + _mmb_tn(d_qk[h], q[h]) + d_kb[h] * bcol[h] + d_kd[h] * kdsc[h])
        dq = each(lambda h: _mmb(d_qk[h], k[h]) + d_qd[h] * ecol[h])
        dbeta = each(lambda h: rowsum(dru[h] * v[h] + d_kb[h] * k[h]))
        de = each(lambda h: rowsum(drw[h] * kb[h] + d_qd[h] * q[h]))
        r = each(lambda h: rowsum(d_kd[h] * k[h]) * kdsc[h])
        mm = each(lambda h: d_a[h] * a[h] + d_attn[h] * attn[h])
        d_gl = each(lambda h: jnp.sum(r[h], axis=0, keepdims=True) + d_el[h] * el[h])
        d_gc = each(lambda h: de[h] * ecol[h] - r[h] + rowsum(mm[h]) - rowsum(mm[h].T)
                    + jnp.where(rowi == CHUNK - 1, d_gl[h], 0.0))
        dbeta_full = jnp.zeros((CHUNK, 128), F32)
        dgc_full = jnp.zeros((CHUNK, 128), F32)
        for h in heads:
            dqkv_ref[:, sls[h]] = dq[h]
            dqkv_ref[:, width + h * HEAD:width + (h + 1) * HEAD] = dk[h]
            dqkv_ref[:, 2 * width + h * HEAD:2 * width + (h + 1) * HEAD] = dru[h] * bcol[h]
            dstate[h] = dst_new[h]
            dbeta_full = dbeta_full + jnp.where(lane == h, dbeta[h], 0.0)
            dgc_full = dgc_full + jnp.where(lane == nh + h, d_gc[h], 0.0)
        dgb_ref[...] = dbeta_full + _mm_exact_lhs(ri <= ci, dgc_full)

    rev = lambda c: pl.BlockSpec((CHUNK, width), lambda n, c=c: (nc - 1 - n, c))
    outs = _pc(body, name="gdn_bwd", grid=(nc,),
               in_specs=[rev(0), rev(1), rev(2), pl.BlockSpec((CHUNK, 128), lambda n: (nc - 1 - n, 0)),
                         pl.BlockSpec((8, 128), lambda n: (0, 0)), rev(0),
                         pl.BlockSpec((1, nh, HEAD, HEAD), lambda n: (nc - 1 - n, 0, 0, 0)),
                         pl.BlockSpec((1, nh, CHUNK, CHUNK), lambda n: (nc - 1 - n, 0, 0, 0))] + c_in,
               out_specs=[pl.BlockSpec((CHUNK, 3 * width), lambda n: (nc - 1 - n, 0)),
                          pl.BlockSpec((CHUNK, 128), lambda n: (nc - 1 - n, 0))] + c_out,
               out_shape=[jax.ShapeDtypeStruct((s, 3 * width), F32), jax.ShapeDtypeStruct((s, 128), F32)] + c_shapes,
               scratch_shapes=[pltpu.VMEM((nh, HEAD, HEAD), F32)] + c_sems,
               compiler_params=_cp(("arbitrary",)))(qkv, qkv, qkv, ba, gp, do, states, tinv, *c_args)
    return outs[0], outs[1], outs[2:]


def _gates_bwd(ba, dgb, gp, nh):
    s = ba.shape[0]
    tm = _tile(s, (512, 256, 128, 64))

    def body(ba_ref, d_ref, gp_ref, o_ref, dp_ref):
        @pl.when(pl.program_id(0) == 0)
        def _():
            dp_ref[...] = jnp.zeros_like(dp_ref)

        bav, dv = ba_ref[...], d_ref[...]
        lane = lax.broadcasted_iota(jnp.int32, bav.shape, 1)
        beta = _sigmoid(bav)
        amp = jnp.exp(gp_ref[0:1, :])
        z = bav + gp_ref[1:2, :]
        d_a = dv * (-amp) * _sigmoid(z)
        d_b = dv * beta * (1.0 - beta)
        is_a = (lane >= nh) & (lane < 2 * nh)
        o_ref[...] = jnp.where(lane < nh, d_b, jnp.where(is_a, d_a, 0.0))
        dp_ref[0:1, :] += jnp.sum(jnp.where(is_a, dv * (-amp) * _softplus(z), 0.0), axis=0, keepdims=True)
        dp_ref[1:2, :] += jnp.sum(jnp.where(is_a, d_a, 0.0), axis=0, keepdims=True)

    row = pl.BlockSpec((tm, 128), lambda i: (i, 0))
    par = pl.BlockSpec((8, 128), lambda i: (0, 0))
    return _pc(body, name="gates_bwd", grid=(s // tm,), in_specs=[row, row, par], out_specs=[row, par],
               out_shape=[jax.ShapeDtypeStruct((s, 128), F32), jax.ShapeDtypeStruct((8, 128), F32)],
               compiler_params=_cp(("arbitrary",)))(ba, dgb, gp)


def _mix_post_fwd(o_raw, proj, gain, sc_w, width):
    s = o_raw.shape[0]
    tm = _tile(s, (256, 128, 64))
    nh = width // HEAD
    ksc = sc_w.shape[0]

    def body(o_ref, z_ref, b_ref, c_ref, h_ref, cp_ref, hp_ref, g_ref, w_ref, y_ref):
        i = pl.program_id(0)
        z = z_ref[...].astype(F32)
        sz = z * _sigmoid(z)
        for h in range(nh):
            sl = slice(h * HEAD, (h + 1) * HEAD)
            o = o_ref[:, sl]
            r = lax.rsqrt(jnp.mean(o * o, axis=-1, keepdims=True) + EPS)
            y_ref[:, sl] = (o * r * g_ref[...] * sz[:, sl]).astype(BF16)
        prod = c_ref[...].astype(F32) * h_ref[...].astype(F32)
        pprev = jnp.where(i > 0, cp_ref[...].astype(F32) * hp_ref[...].astype(F32), 0.0)
        cv = prod * w_ref[ksc - 1:ksc, :]
        for j in range(ksc - 1):
            cv = cv + _shift_down(prod, pprev, ksc - 1 - j) * w_ref[j:j + 1, :]
        y_ref[:, width:] = (b_ref[...].astype(F32) * cv).astype(BF16)

    col = lambda c: pl.BlockSpec((tm, width), lambda i, c=c: (i, c))
    prv = lambda c: pl.BlockSpec((HALO, width), lambda i, c=c: (jnp.maximum(i * (tm // HALO) - 1, 0), c))
    return _pc(body, name="mix_post_fwd", grid=(s // tm,),
               in_specs=[col(0), col(0), col(1), col(2), col(3), prv(2), prv(3),
                         pl.BlockSpec((1, HEAD), lambda i: (0, 0)), pl.BlockSpec((ksc, width), lambda i: (0, 0))],
               out_specs=pl.BlockSpec((tm, 2 * width), lambda i: (i, 0)),
               out_shape=jax.ShapeDtypeStruct((s, 2 * width), BF16),
               compiler_params=_cp(("parallel",)))(o_raw, proj, proj, proj, proj, proj, proj,
                                                    gain.reshape(1, HEAD), sc_w)


def _mix_post_bwd(dy, o_raw, proj, gain, sc_w, width):
    s = o_raw.shape[0]
    tm = _tile(s, (256, 128, 64))
    nt = s // tm
    nh = width // HEAD
    ksc = sc_w.shape[0]

    def body(dyg_ref, dys_ref, dysn_ref, o_ref, z_ref, b_ref, bn_ref, c_ref, h_ref, cp_ref, hp_ref, g_ref, w_ref,
             do_ref, dp_ref, dg_ref, dw_ref):
        i = pl.program_id(0)

        @pl.when(i == 0)
        def _():
            dg_ref[...] = jnp.zeros_like(dg_ref)
            dw_ref[...] = jnp.zeros_like(dw_ref)

        z = z_ref[...].astype(F32)
        sg = _sigmoid(z)
        sz = z * sg
        dsz = sg * (1.0 + z * (1.0 - sg))
        dyg = dyg_ref[...].astype(F32)
        dgain = jnp.zeros((1, HEAD), F32)
        for h in range(nh):
            sl = slice(h * HEAD, (h + 1) * HEAD)
            o = o_ref[:, sl]
            r = lax.rsqrt(jnp.mean(o * o, axis=-1, keepdims=True) + EPS)
            oh = o * r
            d_yn = dyg[:, sl] * sz[:, sl]
            dp_ref[:, sl] = (dyg[:, sl] * oh * g_ref[...] * dsz[:, sl]).astype(BF16)
            dgain = dgain + jnp.sum(d_yn * oh, axis=0, keepdims=True)
            doh = d_yn * g_ref[...]
            do_ref[:, sl] = r * (doh - oh * jnp.mean(doh * oh, axis=-1, keepdims=True))
        dg_ref[0:1, :] += dgain
        cc, hh = c_ref[...].astype(F32), h_ref[...].astype(F32)
        prod = cc * hh
        pprev = jnp.where(i > 0, cp_ref[...].astype(F32) * hp_ref[...].astype(F32), 0.0)
        sh = [_shift_down(prod, pprev, ksc - 1 - j) for j in range(ksc)]
        cv = sh[0] * w_ref[0:1, :]
        for j in range(1, ksc):
            cv = cv + sh[j] * w_ref[j:j + 1, :]
        dys = dys_ref[...].astype(F32)
        dp_ref[:, width:2 * width] = (dys * cv).astype(BF16)
        dcv_n = jnp.where(i < nt - 1, dysn_ref[...].astype(F32) * bn_ref[...].astype(F32), 0.0)
        dcv = jnp.concatenate([dys * b_ref[...].astype(F32), dcv_n], axis=0)
        dprod = dcv[:tm] * w_ref[ksc - 1:ksc, :]
        for j in range(ksc - 1):
            dprod = dprod + _shift_up(dcv, ksc - 1 - j, tm) * w_ref[j:j + 1, :]
        dp_ref[:, 2 * width:3 * width] = (dprod * hh).astype(BF16)
        dp_ref[:, 3 * width:] = (dprod * cc).astype(BF16)
        for j in range(ksc):
            dw_ref[j:j + 1, :] += jnp.sum(dcv[:tm] * sh[j], axis=0, keepdims=True)

    col = lambda c: pl.BlockSpec((tm, width), lambda i, c=c: (i, c))
    prv = lambda c: pl.BlockSpec((HALO, width), lambda i, c=c: (jnp.maximum(i * (tm // HALO) - 1, 0), c))
    nxt = lambda c: pl.BlockSpec((HALO, width), lambda i, c=c: (jnp.minimum((i + 1) * (tm // HALO), s // HALO - 1), c))
    do, dp, dg, dw = _pc(
        body, name="mix_post_bwd", grid=(nt,),
        in_specs=[col(0), col(1), nxt(1), col(0), col(0), col(1), nxt(1), col(2), col(3), prv(2), prv(3),
                  pl.BlockSpec((1, HEAD), lambda i: (0, 0)), pl.BlockSpec((ksc, width), lambda i: (0, 0))],
        out_specs=[col(0), pl.BlockSpec((tm, 4 * width), lambda i: (i, 0)),
                   pl.BlockSpec((8, HEAD), lambda i: (0, 0)), pl.BlockSpec((8, width), lambda i: (0, 0))],
        out_shape=[jax.ShapeDtypeStruct((s, width), F32), jax.ShapeDtypeStruct((s, 7 * width), BF16),
                   jax.ShapeDtypeStruct((8, HEAD), F32), jax.ShapeDtypeStruct((8, width), F32)],
        compiler_params=_cp(("arbitrary",)))(dy, dy, dy, o_raw, proj, proj, proj, proj, proj, proj, proj,
                                             gain.reshape(1, HEAD), sc_w)
    return do, dp, dg[0], dw[:ksc]


def _xattn_fwd(q, k, v):
    s, d = q.shape
    nm = k.shape[0]
    dh = d // XHEADS
    tm = _tile(s, (512, 256, 128, 64))

    def body(q_ref, k_ref, v_ref, o_ref):
        sc = _mmb_nt(q_ref[...], k_ref[...]) * (dh ** -0.5)
        p = jnp.exp(sc - jnp.max(sc, axis=-1, keepdims=True))
        p = p / jnp.sum(p, axis=-1, keepdims=True)
        o_ref[...] = _mmb(p, v_ref[...]).astype(BF16)

    return _pc(body, name="xattn_fwd", grid=(s // tm, XHEADS),
               in_specs=[pl.BlockSpec((tm, dh), lambda i, h: (i, h)), pl.BlockSpec((nm, dh), lambda i, h: (0, h)),
                         pl.BlockSpec((nm, dh), lambda i, h: (0, h))],
               out_specs=pl.BlockSpec((tm, dh), lambda i, h: (i, h)), out_shape=jax.ShapeDtypeStruct((s, d), BF16),
               compiler_params=_cp(("parallel", "parallel")))(q, k, v)


def _xattn_bwd(q, k, v, do):
    s, d = q.shape
    nm = k.shape[0]
    dh = d // XHEADS
    tm = _tile(s, (512, 256, 128, 64))

    def body(q_ref, k_ref, v_ref, do_ref, dq_ref, dk_ref, dv_ref):
        @pl.when(pl.program_id(1) == 0)
        def _():
            dk_ref[...] = jnp.zeros_like(dk_ref)
            dv_ref[...] = jnp.zeros_like(dv_ref)

        scale = dh ** -0.5
        sc = _mmb_nt(q_ref[...], k_ref[...]) * scale
        p = jnp.exp(sc - jnp.max(sc, axis=-1, keepdims=True))
        p = p / jnp.sum(p, axis=-1, keepdims=True)
        dp = _mmb_nt(do_ref[...], v_ref[...])
        ds = p * (dp - jnp.sum(dp * p, axis=-1, keepdims=True)) * scale
        dq_ref[...] = _mmb(ds, k_ref[...]).astype(BF16)
        dk_ref[...] += _mmb_tn(ds, q_ref[...])
        dv_ref[...] += _mmb_tn(p, do_ref[...])

    rowb = pl.BlockSpec((tm, dh), lambda h, i: (i, h))
    memb = pl.BlockSpec((nm, dh), lambda h, i: (0, h))
    return _pc(body, name="xattn_bwd", grid=(XHEADS, s // tm), in_specs=[rowb, memb, memb, rowb],
               out_specs=[rowb, memb, memb],
               out_shape=[jax.ShapeDtypeStruct((s, d), BF16), jax.ShapeDtypeStruct((nm, d), F32),
                          jax.ShapeDtypeStruct((nm, d), F32)],
               compiler_params=_cp(("parallel", "arbitrary")))(q, k, v, do)


def _ffn_act_fwd(u_pre, conv_w):
    s, f2 = u_pre.shape
    f = f2 // 2
    tm = _tile(s, (256, 128, 64))
    cb = _tile(f, (512, 256, 128))
    nf = f // cb
    kf = conv_w.shape[0]

    def body(g_ref, u_ref, gp_ref, up_ref, wg_ref, wu_ref, a_ref):
        i = pl.program_id(0)

        def conv(c_ref, p_ref, w_ref):
            cur = c_ref[...].astype(F32)
            prev = jnp.where(i > 0, p_ref[...].astype(F32), 0.0)
            out = cur * w_ref[kf - 1:kf, :]
            for j in range(kf - 1):
                out = out + _shift_down(cur, prev, kf - 1 - j) * w_ref[j:j + 1, :]
            return out

        gate, up = conv(g_ref, gp_ref, wg_ref), conv(u_ref, up_ref, wu_ref)
        a_ref[...] = (gate * _sigmoid(gate) * up).astype(BF16)

    return _pc(body, name="ffn_act_fwd", grid=(s // tm, nf),
               in_specs=[pl.BlockSpec((tm, cb), lambda i, j: (i, j)), pl.BlockSpec((tm, cb), lambda i, j: (i, j + nf)),
                         pl.BlockSpec((HALO, cb), _prev_map(tm, lambda j: j)),
                         pl.BlockSpec((HALO, cb), _prev_map(tm, lambda j: j + nf)),
                         pl.BlockSpec((kf, cb), lambda i, j: (0, j)), pl.BlockSpec((kf, cb), lambda i, j: (0, j + nf))],
               out_specs=pl.BlockSpec((tm, cb), lambda i, j: (i, j)), out_shape=jax.ShapeDtypeStruct((s, f), BF16),
               compiler_params=_cp(("parallel", "parallel")))(u_pre, u_pre, u_pre, u_pre, conv_w, conv_w)


def _ffn_act_bwd(da, u_pre, conv_w):
    s, f2 = u_pre.shape
    f = f2 // 2
    tm = _tile(s, (256, 128, 64))
    nt = s // tm
    cb = _tile(f, (512, 256, 128))
    nf = f // cb
    kf = conv_w.shape[0]

    def body(da_ref, dan_ref, g_ref, gp_ref, gn_ref, u_ref, up_ref, un_ref, wg_ref, wu_ref, d_ref, dw_ref):
        i = pl.program_id(1)

        @pl.when(i == 0)
        def _():
            dw_ref[...] = jnp.zeros_like(dw_ref)

        def conv(c_ref, p_ref, n_ref, w_ref):
            ext = jnp.concatenate([c_ref[...].astype(F32), n_ref[...].astype(F32)], axis=0)
            prev = jnp.where(i > 0, p_ref[...].astype(F32), 0.0)
            sh = [_shift_down(ext, prev, kf - 1 - j) for j in range(kf)]
            out = sh[0] * w_ref[0:1, :]
            for j in range(1, kf):
                out = out + sh[j] * w_ref[j:j + 1, :]
            return out, sh

        gate, gsh = conv(g_ref, gp_ref, gn_ref, wg_ref)
        up, ush = conv(u_ref, up_ref, un_ref, wu_ref)
        dav = jnp.concatenate([da_ref[...].astype(F32), dan_ref[...].astype(F32)], axis=0)
        rows = lax.broadcasted_iota(jnp.int32, (tm + HALO, 1), 0)
        dav = jnp.where((rows < tm) | (i < nt - 1), dav, 0.0)
        sg = _sigmoid(gate)
        dgate = dav * up * (sg * (1.0 + gate * (1.0 - sg)))
        dup = dav * (gate * sg)

        def conv_t(dv, w_ref):
            out = dv[:tm] * w_ref[kf - 1:kf, :]
            for j in range(kf - 1):
                out = out + _shift_up(dv, kf - 1 - j, tm) * w_ref[j:j + 1, :]
            return out

        d_ref[0] = conv_t(dgate, wg_ref).astype(BF16)
        d_ref[1] = conv_t(dup, wu_ref).astype(BF16)
        for j in range(kf):
            dw_ref[0, j:j + 1, :] += jnp.sum(dgate[:tm] * gsh[j][:tm], axis=0, keepdims=True)
            dw_ref[1, j:j + 1, :] += jnp.sum(dup[:tm] * ush[j][:tm], axis=0, keepdims=True)

    pm = lambda off: (lambda j, i: _prev_map(tm, lambda c: c + off)(i, j))
    nm = lambda off: (lambda j, i: _next_map(tm, s, lambda c: c + off)(i, j))
    du, dw = _pc(
        body, name="ffn_act_bwd", grid=(nf, nt),
        in_specs=[pl.BlockSpec((tm, cb), lambda j, i: (i, j)), pl.BlockSpec((HALO, cb), nm(0)),
                  pl.BlockSpec((tm, cb), lambda j, i: (i, j)), pl.BlockSpec((HALO, cb), pm(0)),
                  pl.BlockSpec((HALO, cb), nm(0)),
                  pl.BlockSpec((tm, cb), lambda j, i: (i, j + nf)), pl.BlockSpec((HALO, cb), pm(nf)),
                  pl.BlockSpec((HALO, cb), nm(nf)),
                  pl.BlockSpec((kf, cb), lambda j, i: (0, j)), pl.BlockSpec((kf, cb), lambda j, i: (0, j + nf))],
        out_specs=[pl.BlockSpec((2, tm, cb), lambda j, i: (0, i, j)), pl.BlockSpec((2, 8, cb), lambda j, i: (0, 0, j))],
        out_shape=[jax.ShapeDtypeStruct((2, s, f), BF16), jax.ShapeDtypeStruct((2, 8, f), F32)],
        compiler_params=_cp(("parallel", "arbitrary")))(da, da, u_pre, u_pre, u_pre, u_pre, u_pre, u_pre, conv_w, conv_w)
    return du, dw[:, :kf]


def _adamw(w, g, m, v, name):
    shape = w.shape
    c = shape[-1]
    r = w.size // c
    tr = r if r * c <= 262144 else _tile(r, tuple(t for t in (512, 256, 128, 64, 32, 16, 8) if t * c <= 262144))
    bc1 = 1.0 - ADAM_B1 ** ADAM_STEP
    bc2 = 1.0 - ADAM_B2 ** ADAM_STEP

    def body(w_ref, g_ref, m_ref, v_ref, d_ref, nm_ref, nv_ref):
        gv = g_ref[...]
        mn = ADAM_B1 * m_ref[...] + (1.0 - ADAM_B1) * gv
        vn = ADAM_B2 * v_ref[...] + (1.0 - ADAM_B2) * (gv * gv)
        nm_ref[...] = mn
        nv_ref[...] = vn
        d_ref[...] = -ADAM_LR * ((mn / bc1) / (jnp.sqrt(vn / bc2) + ADAM_EPS) + ADAM_WD * w_ref[...])

    blk = pl.BlockSpec((tr, c), lambda i: (i, 0))
    outs = _pc(body, name=name, grid=(r // tr,), in_specs=[blk] * 4, out_specs=[blk] * 3,
               out_shape=[jax.ShapeDtypeStruct((r, c), F32)] * 3,
               compiler_params=_cp(("parallel",)))(*(t.reshape(r, c) for t in (w, g, m, v)))
    return tuple(o.reshape(shape) for o in outs)


def _adamw_sharded(w, m, v, partials, name):
    nl, r, c = w.shape
    tr = _tile(r, tuple(t for t in (256, 128, 64, 32, 16, 8) if t * c <= 131072))
    bc1 = 1.0 - ADAM_B1 ** ADAM_STEP
    bc2 = 1.0 - ADAM_B2 ** ADAM_STEP

    def body(w_ref, m_ref, v_ref, *rest):
        p_refs, (g_ref, d_ref, nm_ref, nv_ref) = rest[:nl], rest[nl:]
        layer = pl.program_id(0)
        for l in range(nl):
            @pl.when(layer == l)
            def _(p_ref=p_refs[l]):
                gv = p_ref[0].astype(F32)
                for dev in range(1, N_DEV):
                    gv = gv + p_ref[dev].astype(F32)
                mn = ADAM_B1 * m_ref[...] + (1.0 - ADAM_B1) * gv
                vn = ADAM_B2 * v_ref[...] + (1.0 - ADAM_B2) * (gv * gv)
                g_ref[...] = gv
                nm_ref[...] = mn
                nv_ref[...] = vn
                d_ref[...] = -ADAM_LR * ((mn / bc1) / (jnp.sqrt(vn / bc2) + ADAM_EPS) + ADAM_WD * w_ref[...])

    blk = pl.BlockSpec((None, tr, c), lambda l, i: (l, i, 0))
    p_specs = [pl.BlockSpec((N_DEV, tr, c), lambda l, i, k=k: (0, jnp.where(l == k, i, 0), 0)) for k in range(nl)]
    return _pc(body, name=name, grid=(nl, r // tr), in_specs=[blk] * 3 + p_specs, out_specs=[blk] * 4,
               out_shape=[jax.ShapeDtypeStruct((nl, r, c), F32)] * 4,
               compiler_params=_cp(("arbitrary", "arbitrary")))(w, m, v, *partials)


def _slot_sum(x, name):
    _, r, c = x.shape
    tr = _tile(r, (512, 256, 128, 64, 32, 16, 8))

    def body(x_ref, o_ref):
        acc = x_ref[0].astype(F32)
        for d in range(1, N_DEV):
            acc = acc + x_ref[d].astype(F32)
        o_ref[...] = acc

    return _pc(body, name=name, grid=(r // tr,), in_specs=[pl.BlockSpec((N_DEV, tr, c), lambda i: (0, i, 0))],
               out_specs=pl.BlockSpec((tr, c), lambda i: (i, 0)), out_shape=jax.ShapeDtypeStruct((r, c), F32),
               compiler_params=_cp(("parallel",)))(x)


def _all_gather(xs, name, layers=None):
    na = len(xs)
    layers = tuple(range(xs[0].shape[0])) if layers is None else layers
    nl = len(layers) * na

    def body(*refs):
        x_refs, out_refs = refs[:na], refs[na:na + nl]
        send_sems, recv_sems, local_sems = refs[na + nl:]
        mx, my, mc = lax.axis_index("x"), lax.axis_index("y"), lax.axis_index("c")
        me, sibling = (mx, my, mc), (mx, my, 1 - mc)
        chips = [(1 - mx, my), (mx, 1 - my), (1 - mx, 1 - my)]

        def local(l):
            return x_refs[l % na].at[layers[l // na]]

        def slot(l, dev):
            return out_refs[l].at[4 * dev[0] + 2 * dev[1] + dev[2]]

        def copy(l, k, block, to, src=None):
            return pltpu.make_async_remote_copy(
                src_ref=slot(l, block) if src is None else src, dst_ref=slot(l, block),
                send_sem=send_sems.at[l, k], recv_sem=recv_sems.at[l, k], device_id=to, device_id_type=MESH)

        mine = [pltpu.make_async_copy(local(l), slot(l, me), local_sems.at[l]) for l in range(nl)]
        first = []
        for l in range(nl):
            mine[l].start()
            first.append(copy(l, 0, me, sibling, src=local(l)))
            first += [copy(l, 1 + j, me, (*chip, mc), src=local(l)) for j, chip in enumerate(chips)]
        for cp in first:
            cp.start()
        passed = []
        for l in range(nl):
            for j, chip in enumerate(chips):
                copy(l, 1 + j, (*chip, mc), me).wait_recv()
                fw = copy(l, 4 + j, (*chip, mc), sibling)
                fw.start()
                passed.append(fw)
        for l in range(nl):
            copy(l, 0, sibling, me).wait_recv()
            for j, chip in enumerate(chips):
                copy(l, 4 + j, (*chip, 1 - mc), me).wait_recv()
        for cp in first + passed:
            cp.wait_send()
        for l in range(nl):
            mine[l].wait()

    anyspec = pl.BlockSpec(memory_space=pl.ANY)
    outs = _pc(body, name=name, in_specs=[anyspec] * na, out_specs=[anyspec] * nl,
               out_shape=[jax.ShapeDtypeStruct((N_DEV,) + xs[l % na].shape[1:], xs[l % na].dtype) for l in range(nl)],
               scratch_shapes=[pltpu.SemaphoreType.DMA((nl, 7)), pltpu.SemaphoreType.DMA((nl, 7)),
                               pltpu.SemaphoreType.DMA((nl,))],
               compiler_params=_cp())(*xs)
    return [outs[l * na:(l + 1) * na] for l in range(nl // na)]


def _mix_in_assemble(shards, width, nh):
    _, d, cs = shards.shape
    tr = _tile(d, (128, 64, 32, 16))
    w4 = 4 * width

    def body(s_ref, main_ref, gate_ref):
        full = jnp.concatenate([s_ref[j].astype(F32) for j in range(N_DEV)], axis=1)
        main_ref[:, :width] = full[:, 3 * width:w4].astype(BF16)
        main_ref[:, width:w4] = full[:, w4 + 2 * nh:].astype(BF16)
        main_ref[:, w4:] = full[:, :3 * width].astype(BF16)
        gate_ref[...] = jnp.concatenate([full[:, w4:w4 + 2 * nh], jnp.zeros((tr, 128 - 2 * nh), F32)],
                                        axis=1).astype(BF16)

    return _pc(body, name="mix_in_assemble", grid=(d // tr,),
               in_specs=[pl.BlockSpec((N_DEV, tr, cs), lambda i: (0, i, 0))],
               out_specs=[pl.BlockSpec((tr, 7 * width), lambda i: (i, 0)), pl.BlockSpec((tr, 128), lambda i: (i, 0))],
               out_shape=[jax.ShapeDtypeStruct((d, 7 * width), BF16), jax.ShapeDtypeStruct((d, 128), BF16)],
               compiler_params=_cp(("parallel",)))(shards)


def _mix_in_shards(dmain, dgate, width, nh):
    d = dmain.shape[0]
    cs = (7 * width + 2 * nh) // N_DEV
    tr = _tile(d, (128, 64, 32, 16))
    w4 = 4 * width

    def body(main_ref, gate_ref, o_ref):
        full = jnp.concatenate([main_ref[:, w4:], main_ref[:, :width], gate_ref[:, :2 * nh], main_ref[:, width:w4]],
                               axis=1)
        for j in range(N_DEV):
            o_ref[j] = full[:, j * cs:(j + 1) * cs].astype(BF16)

    return _pc(body, name="mix_in_shards", grid=(d // tr,),
               in_specs=[pl.BlockSpec((tr, 7 * width), lambda i: (i, 0)), pl.BlockSpec((tr, 128), lambda i: (i, 0))],
               out_specs=pl.BlockSpec((N_DEV, tr, cs), lambda i: (0, i, 0)),
               out_shape=jax.ShapeDtypeStruct((N_DEV, d, cs), BF16),
               compiler_params=_cp(("parallel",)))(dmain, dgate)


PACK_COLS = 1024
BIG = ("w_mix_in", "w_mix_out", "w_xq", "w_xk", "w_xv", "w_xo", "w_ffn_up", "w_ffn_down")


def _pack(blocks, lead):
    flat = jnp.concatenate(blocks, axis=-1)
    n = flat.shape[-1]
    per = 16 * PACK_COLS
    pad = (-n) % per
    flat = jnp.pad(flat, [(0, 0)] * len(lead) + [(0, pad)])
    return flat.reshape(*lead, (n + pad) // PACK_COLS, PACK_COLS)


def _unpack(packed, sizes):
    flat = packed.reshape(*packed.shape[:-2], -1)
    out, off = [], 0
    for n in sizes:
        out.append(flat[..., off:off + n])
        off += n
    return out


def kernel(x, mem, mix_norm, w_mix_in, gdn_conv, gdn_a_log, gdn_dt_bias, gdn_out_norm, sc_conv, w_mix_out, xattn_norm, mem_norm, w_xq, w_xk, w_xv, w_xo, ffn_norm, w_ffn_up, ffn_conv, w_ffn_down, final_norm, loss_target, m_mix_norm, m_w_mix_in, m_gdn_conv, m_gdn_a_log, m_gdn_dt_bias, m_gdn_out_norm, m_sc_conv, m_w_mix_out, m_xattn_norm, m_mem_norm, m_w_xq, m_w_xk, m_w_xv, m_w_xo, m_ffn_norm, m_w_ffn_up, m_ffn_conv, m_w_ffn_down, m_final_norm, v_mix_norm, v_w_mix_in, v_gdn_conv, v_gdn_a_log, v_gdn_dt_bias, v_gdn_out_norm, v_sc_conv, v_w_mix_out, v_xattn_norm, v_mem_norm, v_w_xq, v_w_xk, v_w_xv, v_w_xo, v_ffn_norm, v_w_ffn_up, v_ffn_conv, v_w_ffn_down, v_final_norm):
    names = ["mix_norm", "w_mix_in", "gdn_conv", "gdn_a_log", "gdn_dt_bias", "gdn_out_norm", "sc_conv", "w_mix_out",
             "xattn_norm", "mem_norm", "w_xq", "w_xk", "w_xv", "w_xo", "ffn_norm", "w_ffn_up", "ffn_conv",
             "w_ffn_down", "final_norm"]
    wts = dict(zip(names, (mix_norm, w_mix_in, gdn_conv, gdn_a_log, gdn_dt_bias, gdn_out_norm, sc_conv, w_mix_out,
                           xattn_norm, mem_norm, w_xq, w_xk, w_xv, w_xo, ffn_norm, w_ffn_up, ffn_conv, w_ffn_down,
                           final_norm)))
    mom1 = dict(zip(names, (m_mix_norm, m_w_mix_in, m_gdn_conv, m_gdn_a_log, m_gdn_dt_bias, m_gdn_out_norm, m_sc_conv,
                            m_w_mix_out, m_xattn_norm, m_mem_norm, m_w_xq, m_w_xk, m_w_xv, m_w_xo, m_ffn_norm,
                            m_w_ffn_up, m_ffn_conv, m_w_ffn_down, m_final_norm)))
    mom2 = dict(zip(names, (v_mix_norm, v_w_mix_in, v_gdn_conv, v_gdn_a_log, v_gdn_dt_bias, v_gdn_out_norm, v_sc_conv,
                            v_w_mix_out, v_xattn_norm, v_mem_norm, v_w_xq, v_w_xk, v_w_xv, v_w_xo, v_ffn_norm,
                            v_w_ffn_up, v_ffn_conv, v_w_ffn_down, v_final_norm)))

    x0 = x[0]
    memv = mem[0]
    target = loss_target[0]
    s, d = x0.shape
    depth = mix_norm.shape[0]
    width = d // 2
    nh = width // HEAD
    me = 4 * lax.axis_index("x") + 2 * lax.axis_index("y") + lax.axis_index("c")

    local_bf16 = {n: wts[n].astype(BF16) for n in BIG}
    gathered = [None] * depth
    gathered[0] = dict(zip(BIG, _all_gather([local_bf16[n] for n in BIG], "all_gather_weights", layers=(0,))[0]))
    with_gdn = ("w_mix_in", "w_mix_out", "w_xq", "w_xk", "w_xv", "w_xo")

    def gather_next(l, group):
        return None if l + 1 >= depth else ("gather", [local_bf16[n] for n in group], l + 1)

    conv_names = ("gdn_conv", "sc_conv", "ffn_conv")
    conv_local = _pack([wts[n].reshape(1, -1) for n in conv_names], (1,))
    conv_all = _all_gather([conv_local], "all_gather_conv")[0][0]
    conv_parts = _unpack(conv_all, [wts[n].size for n in conv_names])
    conv_full = {}
    for n, part in zip(conv_names, conv_parts):
        _, kt, cs = wts[n].shape
        conv_full[n] = jnp.moveaxis(part.reshape(N_DEV, depth, kt, cs), 0, 2).reshape(depth, kt, N_DEV * cs)

    def layer_weights(l):
        wl = {n: gathered[l][n].reshape(-1, gathered[l][n].shape[-1]) for n in BIG if n not in ("w_mix_in", "w_ffn_up")}
        wl["w_ffn_up"] = gathered[l]["w_ffn_up"]
        return wl

    def gate_params(l):
        rows = jnp.stack([gdn_a_log[l], gdn_dt_bias[l]])
        return jnp.pad(rows, ((0, 6), (nh, 128 - 2 * nh)))

    saved = []
    xc = x0
    for l in range(depth):
        wl = layer_weights(l)
        w_main, w_gate = _mix_in_assemble(gathered[l]["w_mix_in"], width, nh)
        gp = gate_params(l)
        h1 = _rms_fwd(xc, mix_norm[l], "rms_mix")
        proj = _mm(h1, w_main, out_dtype=BF16, name="mm_mix_in")
        ba = _mm(h1, w_gate, out_dtype=F32, name="mm_mix_gates")
        qkv = _gdn_prep_fwd(proj, conv_full["gdn_conv"][l], width)
        o_raw, states, tinv, got_gdn = _gdn_fwd(qkv, ba, gp, width, carry=gather_next(l, with_gdn))
        y = _mix_post_fwd(o_raw, proj, gdn_out_norm[l], conv_full["sc_conv"][l], width)
        x1 = _mm(y, wl["w_mix_out"], res=xc, name="mm_mix_out")
        h2 = _rms_fwd(x1, xattn_norm[l], "rms_xattn")
        mem_n = _rms_fwd(memv, mem_norm[l], "rms_mem")
        qx = _mm(h2, wl["w_xq"], out_dtype=BF16, name="mm_xq")
        kx = _mm(mem_n, wl["w_xk"], out_dtype=BF16, name="mm_xk")
        vx = _mm(mem_n, wl["w_xv"], out_dtype=BF16, name="mm_xv")
        ox = _xattn_fwd(qx, kx, vx)
        x2 = _mm(ox, wl["w_xo"], res=x1, name="mm_xo")
        h3 = _rms_fwd(x2, ffn_norm[l], "rms_ffn")
        if l + 1 < depth:
            u_pre, got_up = _mm(h3, wl["w_ffn_up"], b_shards=True, out_dtype=BF16, name="mm_ffn_up",
                                carry=gather_next(l, ("w_ffn_up",)))
            act = _ffn_act_fwd(u_pre, conv_full["ffn_conv"][l])
            x3, got_down = _mm(act, wl["w_ffn_down"], res=x2, name="mm_ffn_down",
                               carry=gather_next(l, ("w_ffn_down",)))
            gathered[l + 1] = dict(zip(with_gdn + ("w_ffn_up", "w_ffn_down"), (*got_gdn, *got_up, *got_down)))
        else:
            u_pre = _mm(h3, wl["w_ffn_up"], b_shards=True, out_dtype=BF16, name="mm_ffn_up")
            act = _ffn_act_fwd(u_pre, conv_full["ffn_conv"][l])
            x3 = _mm(act, wl["w_ffn_down"], res=x2, name="mm_ffn_down")
        saved.append(dict(x0=xc, x1=x1, x2=x2, h1=h1, h2=h2, h3=h3, proj=proj, ba=ba, qkv=qkv, o_raw=o_raw,
                          states=states, tinv=tinv, y=y, mem_n=mem_n, qx=qx, kx=kx, vx=vx, ox=ox, u_pre=u_pre, act=act,
                          w_main=w_main, w_gate=w_gate, gp=gp, wl=wl))
        xc = x3

    loss_part, dx, dxb, g_final = _loss_head(xc, final_norm, target)

    small = {n: [None] * depth for n in ("mix_norm", "xattn_norm", "mem_norm", "ffn_norm", "gdn_a_log", "gdn_dt_bias",
                                          "gdn_out_norm", "gdn_conv", "sc_conv", "ffn_conv")}
    partials = {n: [None] * depth for n in BIG}

    def scatter(big, group):
        return ("scatter", [big[n].reshape((N_DEV,) + wts[n].shape[1:]) for n in group], None)

    for l in reversed(range(depth)):
        sv = saved[l]
        wl, w_main, w_gate, gp = sv["wl"], sv["w_main"], sv["w_gate"], sv["gp"]
        big = {}
        d_act = _mm(dxb, wl["w_ffn_down"], tb=True, out_dtype=BF16, name="mm_d_act")
        big["w_ffn_down"] = _mm(sv["act"], dxb, ta=True, out_dtype=BF16, name="mm_dw_ffn_down")
        du, dcw = _ffn_act_bwd(d_act, sv["u_pre"], conv_full["ffn_conv"][l])
        small["ffn_conv"][l] = jnp.concatenate([dcw[0], dcw[1]], axis=1)
        big["w_ffn_up"], (partials["w_ffn_down"][l],) = _mm(
            sv["h3"], du, ta=True, b_halves=True, out_shards=True, out_dtype=BF16, name="mm_dw_ffn_up",
            carry=scatter(big, ("w_ffn_down",)))
        dh, (partials["w_ffn_up"][l],) = _mm(du, wl["w_ffn_up"], tb=True, a_halves=True, b_shards=True, out_dtype=BF16,
                                             name="mm_dh3", carry=scatter(big, ("w_ffn_up",)))
        dx, dxb, small["ffn_norm"][l] = _rms_bwd(sv["x2"], dh, ffn_norm[l], dx, "rms_bwd_ffn")
        d_ox = _mm(dxb, wl["w_xo"], tb=True, out_dtype=BF16, name="mm_d_ox")
        big["w_xo"] = _mm(sv["ox"], dxb, ta=True, out_dtype=BF16, name="mm_dw_xo")
        d_qx, d_kx, d_vx = _xattn_bwd(sv["qx"], sv["kx"], sv["vx"], d_ox)
        big["w_xq"] = _mm(sv["h2"], d_qx, ta=True, out_dtype=BF16, name="mm_dw_xq")
        big["w_xk"] = _mm(sv["mem_n"], d_kx, ta=True, out_dtype=BF16, name="mm_dw_xk")
        big["w_xv"] = _mm(sv["mem_n"], d_vx, ta=True, out_dtype=BF16, name="mm_dw_xv")
        dh = _mm(d_qx, wl["w_xq"], tb=True, out_dtype=BF16, name="mm_dh2")
        d_mem = _mm(d_kx, wl["w_xk"], tb=True, out_dtype=F32, name="mm_dmem_k")
        d_mem = _mm(d_vx, wl["w_xv"], tb=True, out_dtype=F32, res=d_mem, name="mm_dmem_v")
        _, _, small["mem_norm"][l] = _rms_bwd(memv, d_mem, mem_norm[l], jnp.zeros_like(memv), "rms_bwd_mem")
        dx, dxb, small["xattn_norm"][l] = _rms_bwd(sv["x1"], dh, xattn_norm[l], dx, "rms_bwd_xattn")
        d_y = _mm(dxb, wl["w_mix_out"], tb=True, out_dtype=BF16, name="mm_d_y")
        big["w_mix_out"] = _mm(sv["y"], dxb, ta=True, out_dtype=BF16, name="mm_dw_mix_out")
        d_o, d_proj, small["gdn_out_norm"][l], small["sc_conv"][l] = _mix_post_bwd(
            d_y, sv["o_raw"], sv["proj"], gdn_out_norm[l], conv_full["sc_conv"][l], width)
        with_gdn_bwd = ("w_xo", "w_xq", "w_xk", "w_xv", "w_mix_out")
        d_qkv, d_gb, got = _gdn_bwd(sv["qkv"], sv["ba"], gp, d_o, sv["states"], sv["tinv"], width,
                                    carry=scatter(big, with_gdn_bwd))
        for n, g in zip(with_gdn_bwd, got):
            partials[n][l] = g
        d_ba, d_gp = _gates_bwd(sv["ba"], d_gb, gp, nh)
        small["gdn_a_log"][l] = d_gp[0, nh:2 * nh]
        small["gdn_dt_bias"][l] = d_gp[1, nh:2 * nh]
        d_proj, small["gdn_conv"][l] = _gdn_prep_bwd(d_qkv, sv["proj"], conv_full["gdn_conv"][l], d_proj, width)
        dw_main = _mm(sv["h1"], d_proj, ta=True, name="mm_dw_mix_in")
        dw_gate = _mm(sv["h1"], d_ba, ta=True, name="mm_dw_mix_gates")
        big["w_mix_in"] = _mix_in_shards(dw_main, dw_gate, width, nh)
        dh, (partials["w_mix_in"][l],) = _mm(d_proj, w_main, tb=True, out_dtype=F32, name="mm_dh1_main",
                                             carry=scatter(big, ("w_mix_in",)))
        dh = _mm(d_ba, w_gate, tb=True, out_dtype=BF16, res=dh, name="mm_dh1_gates")
        dx, dxb, small["mix_norm"][l] = _rms_bwd(sv["x0"], dh, mix_norm[l], dx, "rms_bwd_mix")

    small_names = ("mix_norm", "xattn_norm", "mem_norm", "ffn_norm", "gdn_a_log", "gdn_dt_bias", "gdn_out_norm",
                   "gdn_conv", "sc_conv", "ffn_conv")
    small_parts = [jnp.stack(small[n]).reshape(1, -1) for n in small_names]
    small_parts += [g_final.reshape(1, -1), loss_part.reshape(1, 1)]
    small_sizes = [p.shape[1] for p in small_parts]
    small_local = _pack(small_parts, (1,))
    small_sum = _slot_sum(_all_gather([small_local], "all_gather_small")[0][0], "sum_small")
    small_tot = _unpack(small_sum, small_sizes)
    grads = {}
    for n, g in zip(small_names, small_tot[:len(small_names)]):
        if n in conv_names:
            _, kt, cs = wts[n].shape
            g = lax.dynamic_slice_in_dim(g.reshape(depth, kt, N_DEV * cs), me * cs, cs, axis=2)
        grads[n] = g.reshape(wts[n].shape)
    grads["final_norm"] = small_tot[-2].reshape(final_norm.shape)
    loss = small_tot[-1].reshape(())

    delta, new_m, new_v = {}, {}, {}
    for n in names:
        if n in BIG:
            grads[n], delta[n], new_m[n], new_v[n] = _adamw_sharded(wts[n], mom1[n], mom2[n], partials[n], "adamw_" + n)
        else:
            delta[n], new_m[n], new_v[n] = _adamw(wts[n], grads[n], mom1[n], mom2[n], "adamw_" + n)
    return (loss, dx[None], *[grads[n] for n in names], *[delta[n] for n in names],
            *[new_m[n] for n in names], *[new_v[n] for n in names])
```

```python
import functools

import jax
import jax.numpy as jnp
from jax import lax
from jax.experimental import pallas as pl
from jax.experimental.pallas import tpu as pltpu

F32 = jnp.float32
BF16 = jnp.bfloat16
CHUNK = 64
HEAD = 128
XHEADS = 4
GDN_K = 4
EPS = 1e-6
HALO = 16
QKV_COL = 4
N_DEV = 8
VMEM_LIMIT = 56 * 1024 * 1024
ADAM_LR, ADAM_B1, ADAM_B2, ADAM_EPS, ADAM_WD, ADAM_STEP = 0.001, 0.9, 0.999, 1e-08, 0.01, 10
MESH = pl.DeviceIdType.MESH
MM_TILES_MN = (1408, 1024, 512, 256, 128)
MM_TILES_K = (2048, 1408, 1024, 512, 256, 128)


def _call(body, **kw):
    return pl.pallas_call(body, **kw)


def _pc(body, *, carry=None, **kw):
    if carry is None:
        return _call(body, **kw)
    c_args, c_in, c_shapes, c_out, c_sems = _carry_parts(carry)
    nca = len(c_args)
    grid = kw["grid"]
    single = not isinstance(kw["out_shape"], (list, tuple))
    out_shape = [kw["out_shape"]] if single else list(kw["out_shape"])
    out_specs = [kw["out_specs"]] if single else list(kw["out_specs"])
    n_in, n_out = len(kw["in_specs"]), len(out_shape)

    def wrapped(*refs):
        ins, srcs = refs[:n_in], refs[n_in:n_in + nca]
        outs = refs[n_in + nca:n_in + nca + n_out]
        dsts = refs[n_in + nca + n_out:n_in + 2 * nca + n_out]
        scratch = refs[n_in + 2 * nca + n_out:]
        ids = [pl.program_id(ax) for ax in range(len(grid))]
        first = functools.reduce(lambda p, q: p & q, [i == 0 for i in ids])
        last = functools.reduce(lambda p, q: p & q, [i == g - 1 for i, g in zip(ids, grid)])
        _carry_run(carry, srcs, dsts, scratch[-2], scratch[-1], first, last)
        body(*ins, *outs, *scratch[:-2])

    kw = dict(kw, in_specs=list(kw["in_specs"]) + c_in, out_specs=out_specs + c_out, out_shape=out_shape + c_shapes,
              scratch_shapes=list(kw.get("scratch_shapes", [])) + c_sems,
              compiler_params=_cp(("arbitrary",) * len(grid)))
    call = _call(wrapped, **kw)

    def run(*args):
        res = call(*args, *c_args)
        return (res[0] if single else tuple(res[:n_out])), list(res[n_out:])

    return run


def _cp(sem=None, **kw):
    if sem is not None:
        kw["dimension_semantics"] = sem
    return pltpu.CompilerParams(vmem_limit_bytes=VMEM_LIMIT, **kw)


def _tile(n, cands):
    for c in cands:
        if n % c == 0:
            return c
    return n


def _sigmoid(x):
    return 1.0 / (1.0 + jnp.exp(-x))


def _softplus(x):
    return jnp.maximum(x, 0.0) + jnp.log(1.0 + jnp.exp(-jnp.abs(x)))


def _mmb(a, b):
    return jnp.dot(a.astype(BF16), b.astype(BF16), preferred_element_type=F32)


def _mmb_nt(a, b):
    return lax.dot_general(a.astype(BF16), b.astype(BF16), (((1,), (1,)), ((), ())), preferred_element_type=F32)


def _mmb_tn(a, b):
    return lax.dot_general(a.astype(BF16), b.astype(BF16), (((0,), (0,)), ((), ())), preferred_element_type=F32)


def _split(x):
    hi = x.astype(BF16)
    return hi, x - hi.astype(F32)


def _mm3(a, b):
    ah, ar = _split(a)
    bh, br = _split(b)
    al, bl = ar.astype(BF16), br.astype(BF16)
    return (jnp.dot(ah, bh, preferred_element_type=F32)
            + (jnp.dot(ah, bl, preferred_element_type=F32) + jnp.dot(al, bh, preferred_element_type=F32)))


def _mm_exact_lhs(a, b):
    ab = a.astype(BF16)
    b1, r1 = _split(b)
    b2, r2 = _split(r1)
    return (jnp.dot(ab, b1, preferred_element_type=F32)
            + (jnp.dot(ab, b2, preferred_element_type=F32) + jnp.dot(ab, r2.astype(BF16), preferred_element_type=F32)))


def _shift_down(cur, prev, s):
    if s == 0:
        return cur
    r = pltpu.roll(cur, s, 0)
    p = pltpu.roll(prev, s, 0)
    rows = lax.broadcasted_iota(jnp.int32, prev.shape, 0)
    first = jnp.where(rows < s, p, r[:HALO])
    return jnp.concatenate([first, r[HALO:]], axis=0)


def _shift_up(ext, s, tm):
    if s == 0:
        return ext[:tm]
    return pltpu.roll(ext, ext.shape[0] - s, 0)[:tm]


def _prev_map(tm, col):
    return lambda i, j: (jnp.maximum(i * (tm // HALO) - 1, 0), col(j))


def _next_map(tm, nrows, col):
    return lambda i, j: (jnp.minimum((i + 1) * (tm // HALO), nrows // HALO - 1), col(j))


def _carry_parts(carry):
    if carry is None:
        return [], [], [], [], []
    _, srcs, _ = carry
    na = len(srcs)
    anyspec = pl.BlockSpec(memory_space=pl.ANY)
    return (list(srcs), [anyspec] * na, [jax.ShapeDtypeStruct((N_DEV,) + s.shape[1:], s.dtype) for s in srcs],
            [anyspec] * na, [pltpu.SemaphoreType.DMA((na, N_DEV)), pltpu.SemaphoreType.DMA((na, N_DEV))])


def _carry_run(carry, src_refs, dst_refs, send_sems, recv_sems, first, last):
    if carry is None:
        return
    kind, _, layer = carry
    mx, my, mc = lax.axis_index("x"), lax.axis_index("y"), lax.axis_index("c")
    me = 4 * mx + 2 * my + mc

    def descriptors(with_recvs):
        sends, recvs = [], []
        for a in range(len(dst_refs)):
            for k in range(N_DEV):
                px, py, pc = mx ^ (k >> 2), my ^ ((k >> 1) & 1), mc ^ (k & 1)
                peer = 4 * px + 2 * py + pc
                src = src_refs[a].at[peer] if kind == "scatter" else src_refs[a].at[layer]
                if k == 0:
                    sends.append(pltpu.make_async_copy(src, dst_refs[a].at[me], send_sems.at[a, 0]))
                    continue
                sends.append(pltpu.make_async_remote_copy(
                    src_ref=src, dst_ref=dst_refs[a].at[me], send_sem=send_sems.at[a, k], recv_sem=recv_sems.at[a, k],
                    device_id=(px, py, pc), device_id_type=MESH))
                if with_recvs:
                    recvs.append(pltpu.make_async_remote_copy(
                        src_ref=src, dst_ref=dst_refs[a].at[peer], send_sem=send_sems.at[a, k],
                        recv_sem=recv_sems.at[a, k], device_id=(mx, my, mc), device_id_type=MESH))
        return sends, recvs

    @pl.when(first)
    def _():
        for cp in descriptors(False)[0]:
            cp.start()

    @pl.when(last)
    def _():
        sends, recvs = descriptors(True)
        for cp in recvs:
            cp.wait_recv()
        for i, cp in enumerate(sends):
            if i % N_DEV == 0:
                cp.wait()
            else:
                cp.wait_send()


def _mm(a, b, *, ta=False, tb=False, out_dtype=F32, res=None, name, a_halves=False, b_shards=False,
        b_halves=False, out_shards=False, carry=None):
    if a_halves:
        m, k = a.shape[1], 2 * a.shape[2]
    else:
        m, k = (a.shape[1], a.shape[0]) if ta else a.shape
    if b_shards:
        cs = b.shape[2]
        n = b.shape[1] if tb else N_DEV * cs
    elif b_halves:
        n = 2 * b.shape[2]
        cs = n // N_DEV
    else:
        n = b.shape[0] if tb else b.shape[1]
        cs = None
    tm = _tile(m, MM_TILES_MN)
    tn = cs if (cs is not None and not tb) else _tile(n, MM_TILES_MN)
    tk = cs if (b_shards and tb) else _tile(k, MM_TILES_K)
    nk = k // tk
    dn = (((0 if ta else 1,), (1 if tb else 0,)), ((), ()))

    def body(a_ref, b_ref, *rest):
        if res is None:
            o_ref, acc = rest
        else:
            r_ref, o_ref, acc = rest
        kk = pl.program_id(2)
        part = lax.dot_general(a_ref[...].astype(BF16), b_ref[...].astype(BF16), dn, preferred_element_type=F32)

        def finish(total):
            if res is not None:
                total = total + r_ref[...].astype(F32)
            o_ref[...] = total.astype(out_dtype)

        if nk == 1:
            finish(part)
            return

        @pl.when(kk == 0)
        def _():
            acc[...] = part

        @pl.when((kk > 0) & (kk < nk - 1))
        def _():
            acc[...] += part

        @pl.when(kk == nk - 1)
        def _():
            finish(acc[...] + part)

    if a_halves:
        per = (k // 2) // tk
        a_spec = pl.BlockSpec((None, tm, tk), lambda i, j, kk: (kk // per, i, kk % per))
    elif ta:
        a_spec = pl.BlockSpec((tk, tm), lambda i, j, kk: (kk, i))
    else:
        a_spec = pl.BlockSpec((tm, tk), lambda i, j, kk: (i, kk))
    if b_shards and tb:
        b_spec = pl.BlockSpec((None, tn, tk), lambda i, j, kk: (kk, j, 0))
    elif b_shards:
        b_spec = pl.BlockSpec((None, tk, tn), lambda i, j, kk: (j, kk, 0))
    elif b_halves:
        perb = (n // 2) // tn
        b_spec = pl.BlockSpec((None, tk, tn), lambda i, j, kk: (j // perb, kk, j % perb))
    elif tb:
        b_spec = pl.BlockSpec((tn, tk), lambda i, j, kk: (j, kk))
    else:
        b_spec = pl.BlockSpec((tk, tn), lambda i, j, kk: (kk, j))
    if out_shards:
        o_spec = pl.BlockSpec((None, tm, tn), lambda i, j, kk: (j, i, 0))
        o_shape = jax.ShapeDtypeStruct((N_DEV, m, tn), out_dtype)
    else:
        o_spec = pl.BlockSpec((tm, tn), lambda i, j, kk: (i, j))
        o_shape = jax.ShapeDtypeStruct((m, n), out_dtype)
    in_specs = [a_spec, b_spec] + ([o_spec] if res is not None else [])
    args = (a, b) + ((res,) if res is not None else ())
    return _pc(body, carry=carry, name=name, grid=(m // tm, n // tn, nk), in_specs=in_specs, out_specs=o_spec,
               out_shape=o_shape, scratch_shapes=[pltpu.VMEM((tm, tn), F32)],
               compiler_params=_cp(("parallel", "parallel", "arbitrary")))(*args)


def _rms_fwd(x, w, name):
    s, d = x.shape
    tm = _tile(s, (512, 256, 128, 64))

    def body(x_ref, w_ref, o_ref):
        xv = x_ref[...]
        r = lax.rsqrt(jnp.mean(xv * xv, axis=-1, keepdims=True) + EPS)
        o_ref[...] = (xv * r * w_ref[...]).astype(BF16)

    return _pc(body, name=name, grid=(s // tm,),
               in_specs=[pl.BlockSpec((tm, d), lambda i: (i, 0)), pl.BlockSpec((1, d), lambda i: (0, 0))],
               out_specs=pl.BlockSpec((tm, d), lambda i: (i, 0)), out_shape=jax.ShapeDtypeStruct((s, d), BF16),
               compiler_params=_cp(("parallel",)))(x, w.reshape(1, d))


def _rms_bwd(x, dh, w, dx_in, name, carry=None):
    s, d = x.shape
    tm = _tile(s, (256, 128, 64))

    def body(x_ref, dh_ref, w_ref, dxi_ref, dx_ref, dxb_ref, dg_ref):
        @pl.when(pl.program_id(0) == 0)
        def _():
            dg_ref[...] = jnp.zeros_like(dg_ref)

        xv = x_ref[...]
        dy = dh_ref[...].astype(F32)
        r = lax.rsqrt(jnp.mean(xv * xv, axis=-1, keepdims=True) + EPS)
        xh = xv * r
        dxh = dy * w_ref[...]
        dx = dxi_ref[...] + r * (dxh - xh * jnp.mean(dxh * xh, axis=-1, keepdims=True))
        dx_ref[...] = dx
        dxb_ref[...] = dx.astype(BF16)
        dg_ref[0:1, :] += jnp.sum(dy * xh, axis=0, keepdims=True)

    row = pl.BlockSpec((tm, d), lambda i: (i, 0))
    outs = _pc(body, carry=carry, name=name, grid=(s // tm,),
               in_specs=[row, row, pl.BlockSpec((1, d), lambda i: (0, 0)), row],
               out_specs=[row, row, pl.BlockSpec((8, d), lambda i: (0, 0))],
               out_shape=[jax.ShapeDtypeStruct((s, d), F32), jax.ShapeDtypeStruct((s, d), BF16),
                          jax.ShapeDtypeStruct((8, d), F32)],
               compiler_params=_cp(("arbitrary",)))(x, dh, w.reshape(1, d), dx_in)
    (dx, dxb, dg), got = outs if carry is not None else (outs, [])
    return dx, dxb, dg[0], got


def _loss_head(x, w, target):
    s, d = x.shape
    tm = _tile(s, (256, 128, 64))

    def body(x_ref, w_ref, t_ref, dx_ref, dxb_ref, dg_ref, l_ref):
        @pl.when(pl.program_id(0) == 0)
        def _():
            dg_ref[...] = jnp.zeros_like(dg_ref)
            l_ref[...] = jnp.zeros_like(l_ref)

        xv = x_ref[...]
        r = lax.rsqrt(jnp.mean(xv * xv, axis=-1, keepdims=True) + EPS)
        xh = xv * r
        err = xh * w_ref[...] - t_ref[...]
        l_ref[...] += 0.5 * jnp.sum(jnp.mean(err * err, axis=-1, keepdims=True), axis=0, keepdims=True)
        dy = err * (1.0 / d)
        dxh = dy * w_ref[...]
        dx = r * (dxh - xh * jnp.mean(dxh * xh, axis=-1, keepdims=True))
        dx_ref[...] = dx
        dxb_ref[...] = dx.astype(BF16)
        dg_ref[0:1, :] += jnp.sum(dy * xh, axis=0, keepdims=True)

    row = pl.BlockSpec((tm, d), lambda i: (i, 0))
    dx, dxb, dg, ls = _pc(body, name="loss_head", grid=(s // tm,),
                          in_specs=[row, pl.BlockSpec((1, d), lambda i: (0, 0)), row],
                          out_specs=[row, row, pl.BlockSpec((8, d), lambda i: (0, 0)),
                                     pl.BlockSpec((8, 128), lambda i: (0, 0))],
                          out_shape=[jax.ShapeDtypeStruct((s, d), F32), jax.ShapeDtypeStruct((s, d), BF16),
                                     jax.ShapeDtypeStruct((8, d), F32), jax.ShapeDtypeStruct((8, 128), F32)],
                          compiler_params=_cp(("arbitrary",)))(x, w.reshape(1, d), target)
    return ls[0, 0], dx, dxb, dg[0]


def _gdn_prep_fwd(proj, conv_w, width):
    s = proj.shape[0]
    tm = _tile(s, (256, 128, 64))
    nh = width // HEAD

    def body(c_ref, p_ref, w_ref, o_ref):
        i, seg = pl.program_id(0), pl.program_id(1)
        cur = c_ref[...].astype(F32)
        prev = jnp.where(i > 0, p_ref[...].astype(F32), 0.0)
        pre = cur * w_ref[GDN_K - 1:GDN_K, :]
        for j in range(GDN_K - 1):
            pre = pre + _shift_down(cur, prev, GDN_K - 1 - j) * w_ref[j:j + 1, :]
        act = pre * _sigmoid(pre)
        scale = jnp.where(seg == 0, HEAD ** -0.5, 1.0)
        for h in range(nh):
            a = act[:, h * HEAD:(h + 1) * HEAD]
            rs = lax.rsqrt(jnp.sum(a * a, axis=-1, keepdims=True) + EPS) * scale
            o_ref[:, h * HEAD:(h + 1) * HEAD] = a * jnp.where(seg < 2, rs, 1.0)

    return _pc(body, name="gdn_prep_fwd", grid=(s // tm, 3),
               in_specs=[pl.BlockSpec((tm, width), lambda i, j: (i, j + QKV_COL)),
                         pl.BlockSpec((HALO, width), _prev_map(tm, lambda j: j + QKV_COL)),
                         pl.BlockSpec((GDN_K, width), lambda i, j: (0, j))],
               out_specs=pl.BlockSpec((tm, width), lambda i, j: (i, j)),
               out_shape=jax.ShapeDtypeStruct((s, 3 * width), F32),
               compiler_params=_cp(("parallel", "parallel")))(proj, proj, conv_w)


def _gdn_prep_bwd(dqkv, proj, conv_w, dproj_in, width):
    s = proj.shape[0]
    tm = _tile(s, (256, 128, 64))
    nh = width // HEAD
    nt = s // tm

    def body(c_ref, p_ref, n_ref, d_ref, dn_ref, w_ref, _, o_ref, dw_ref):
        seg, i = pl.program_id(0), pl.program_id(1)

        @pl.when(i == 0)
        def _():
            dw_ref[...] = jnp.zeros_like(dw_ref)

        ext = jnp.concatenate([c_ref[...].astype(F32), n_ref[...].astype(F32)], axis=0)
        prev = jnp.where(i > 0, p_ref[...].astype(F32), 0.0)
        sh = [_shift_down(ext, prev, GDN_K - 1 - j) for j in range(GDN_K)]
        pre = sh[0] * w_ref[0:1, :]
        for j in range(1, GDN_K):
            pre = pre + sh[j] * w_ref[j:j + 1, :]
        sg = _sigmoid(pre)
        act = pre * sg
        dout = jnp.concatenate([d_ref[...], dn_ref[...]], axis=0)
        rows = lax.broadcasted_iota(jnp.int32, (tm + HALO, 1), 0)
        dout = jnp.where((rows < tm) | (i < nt - 1), dout, 0.0)
        scale = jnp.where(seg == 0, HEAD ** -0.5, 1.0)
        parts = []
        for h in range(nh):
            a = act[:, h * HEAD:(h + 1) * HEAD]
            dq = dout[:, h * HEAD:(h + 1) * HEAD]
            rs = lax.rsqrt(jnp.sum(a * a, axis=-1, keepdims=True) + EPS)
            nrm = a * rs
            dn = dq * scale
            da_norm = rs * (dn - nrm * jnp.sum(dn * nrm, axis=-1, keepdims=True))
            parts.append(jnp.where(seg < 2, da_norm, dq))
        dact = jnp.concatenate(parts, axis=1)
        dpre = dact * (sg * (1.0 + pre * (1.0 - sg)))
        dp = _shift_up(dpre, 0, tm) * w_ref[GDN_K - 1:GDN_K, :]
        for j in range(GDN_K - 1):
            dp = dp + _shift_up(dpre, GDN_K - 1 - j, tm) * w_ref[j:j + 1, :]
        o_ref[...] = dp.astype(BF16)
        for j in range(GDN_K):
            dw_ref[j:j + 1, :] += jnp.sum(dpre[:tm] * sh[j][:tm], axis=0, keepdims=True)

    dproj, dw = _pc(body, name="gdn_prep_bwd", grid=(3, nt),
                    in_specs=[pl.BlockSpec((tm, width), lambda j, i: (i, j + QKV_COL)),
                              pl.BlockSpec((HALO, width), lambda j, i: _prev_map(tm, lambda c: c + QKV_COL)(i, j)),
                              pl.BlockSpec((HALO, width), lambda j, i: _next_map(tm, s, lambda c: c + QKV_COL)(i, j)),
                              pl.BlockSpec((tm, width), lambda j, i: (i, j)),
                              pl.BlockSpec((HALO, width), lambda j, i: _next_map(tm, s, lambda c: c)(i, j)),
                              pl.BlockSpec((GDN_K, width), lambda j, i: (0, j)),
                              pl.BlockSpec(memory_space=pl.ANY)],
                    out_specs=[pl.BlockSpec((tm, width), lambda j, i: (i, j + QKV_COL)),
                               pl.BlockSpec((8, width), lambda j, i: (0, j))],
                    out_shape=[jax.ShapeDtypeStruct(dproj_in.shape, BF16), jax.ShapeDtypeStruct((8, 3 * width), F32)],
                    input_output_aliases={6: 0},
                    compiler_params=_cp(("parallel", "arbitrary")))(proj, proj, proj, dqkv, dqkv, conv_w, dproj_in)
    return dproj, dw[:GDN_K]


def _chunk_common(bav, gp_ref, nh):
    g_full = -jnp.exp(gp_ref[0:1, :]) * _softplus(bav + gp_ref[1:2, :])
    beta_full = _sigmoid(bav)
    ri = lax.broadcasted_iota(jnp.int32, (CHUNK, CHUNK), 0)
    ci = lax.broadcasted_iota(jnp.int32, (CHUNK, CHUNK), 1)
    gc_full = _mm_exact_lhs(ri >= ci, g_full)
    gc_t = gc_full.T
    return beta_full, gc_full, gc_t, ri, ci


def _head_gates(h, nh, beta_full, gc_full, gc_t, ri, ci):
    bcol = beta_full[:, h:h + 1]
    gcol = gc_full[:, nh + h:nh + h + 1]
    grow = gc_t[nh + h:nh + h + 1, :]
    dec = jnp.exp(jnp.where(ri >= ci, gcol - grow, -1e30))
    ecol = jnp.exp(gcol)
    gl = gcol[CHUNK - 1:CHUNK, :]
    return bcol, gcol, dec, ecol, gl


def _gdn_fwd(qkv, ba, gp, width, carry=None):
    s = qkv.shape[0]
    nh = width // HEAD
    nc = s // CHUNK
    heads = range(nh)

    def body(q_ref, k_ref, v_ref, ba_ref, gp_ref, o_ref, st_ref, t_ref, state):
        @pl.when(pl.program_id(0) == 0)
        def _():
            state[...] = jnp.zeros_like(state)

        beta_full, gc_full, gc_t, ri, ci = _chunk_common(ba_ref[...], gp_ref, nh)
        eye = (ri == ci).astype(F32)
        sls = [slice(h * HEAD, (h + 1) * HEAD) for h in heads]
        q = [q_ref[:, sl] for sl in sls]
        k = [k_ref[:, sl] for sl in sls]
        v = [v_ref[:, sl] for sl in sls]
        st = [state[h] for h in heads]
        gates = [_head_gates(h, nh, beta_full, gc_full, gc_t, ri, ci) for h in heads]
        bcol, gcol, dec, ecol, gl = (list(z) for z in zip(*gates))
        kb = [k[h] * bcol[h] for h in heads]
        a = [jnp.where(ri > ci, _mmb_nt(kb[h], k[h]) * dec[h], 0.0) for h in heads]
        attn = [jnp.where(ri >= ci, _mmb_nt(q[h], k[h]) * dec[h], 0.0) for h in heads]
        t = [eye - a[h] for h in heads]
        pw = [_mm3(a[h], a[h]) for h in heads]
        for _ in range(4):
            both = [_mm3(jnp.concatenate([t[h], pw[h]], axis=0), pw[h]) for h in heads]
            t = [t[h] + both[h][:CHUNK] for h in heads]
            pw = [both[h][CHUNK:] for h in heads]
        t = [t[h] + _mm3(t[h], pw[h]) for h in heads]
        uw = [_mmb(t[h], jnp.concatenate([v[h] * bcol[h], kb[h] * ecol[h]], axis=1)) for h in heads]
        vn = [uw[h][:, :HEAD] - _mmb(uw[h][:, HEAD:], st[h]) for h in heads]
        out = [_mmb(q[h] * ecol[h], st[h]) + _mmb(attn[h], vn[h]) for h in heads]
        new = [st[h] * jnp.exp(gl[h]) + _mmb_tn(k[h] * jnp.exp(gl[h] - gcol[h]), vn[h]) for h in heads]
        for h in heads:
            o_ref[:, sls[h]] = out[h]
            st_ref[0, h] = st[h]
            t_ref[0, h] = t[h]
            state[h] = new[h]

    blk = lambda c: pl.BlockSpec((CHUNK, width), lambda n, c=c: (n, c))
    outs = _pc(body, carry=carry, name="gdn_fwd", grid=(nc,),
               in_specs=[blk(0), blk(1), blk(2), pl.BlockSpec((CHUNK, 128), lambda n: (n, 0)),
                         pl.BlockSpec((8, 128), lambda n: (0, 0))],
               out_specs=[blk(0), pl.BlockSpec((1, nh, HEAD, HEAD), lambda n: (n, 0, 0, 0)),
                          pl.BlockSpec((1, nh, CHUNK, CHUNK), lambda n: (n, 0, 0, 0))],
               out_shape=[jax.ShapeDtypeStruct((s, width), F32), jax.ShapeDtypeStruct((nc, nh, HEAD, HEAD), F32),
                          jax.ShapeDtypeStruct((nc, nh, CHUNK, CHUNK), F32)],
               scratch_shapes=[pltpu.VMEM((nh, HEAD, HEAD), F32)],
               compiler_params=_cp(("arbitrary",)))(qkv, qkv, qkv, ba, gp)
    (o, st, t), got = outs if carry is not None else (outs, [])
    return o, st, t, got


def _gdn_bwd(qkv, ba, gp, do, states, tinv, width, carry=None):
    s = qkv.shape[0]
    nh = width // HEAD
    nc = s // CHUNK
    heads = range(nh)

    def body(q_ref, k_ref, v_ref, ba_ref, gp_ref, do_ref, st_ref, t_ref, dqkv_ref, dgb_ref, dstate):
        @pl.when(pl.program_id(0) == 0)
        def _():
            dstate[...] = jnp.zeros_like(dstate)

        beta_full, gc_full, gc_t, ri, ci = _chunk_common(ba_ref[...], gp_ref, nh)
        lane = lax.broadcasted_iota(jnp.int32, (CHUNK, 128), 1)
        rowi = lax.broadcasted_iota(jnp.int32, (CHUNK, 1), 0)
        low, strict = ri >= ci, ri > ci
        each = lambda fn: [fn(h) for h in heads]
        rowsum = lambda x: jnp.sum(x, axis=1, keepdims=True)
        sls = each(lambda h: slice(h * HEAD, (h + 1) * HEAD))
        q, k, v = each(lambda h: q_ref[:, sls[h]]), each(lambda h: k_ref[:, sls[h]]), each(lambda h: v_ref[:, sls[h]])
        dout = each(lambda h: do_ref[:, sls[h]])
        st, t, dsp = each(lambda h: st_ref[0, h]), each(lambda h: t_ref[0, h]), each(lambda h: dstate[h])
        gates = each(lambda h: _head_gates(h, nh, beta_full, gc_full, gc_t, ri, ci))
        bcol, gcol, dec, ecol, gl = (list(z) for z in zip(*gates))
        el = each(lambda h: jnp.exp(gl[h]))
        kdsc = each(lambda h: jnp.exp(gl[h] - gcol[h]))
        kb = each(lambda h: k[h] * bcol[h])
        a = each(lambda h: jnp.where(strict, _mmb_nt(kb[h], k[h]) * dec[h], 0.0))
        attn = each(lambda h: jnp.where(low, _mmb_nt(q[h], k[h]) * dec[h], 0.0))
        uw = each(lambda h: _mmb(t[h], jnp.concatenate([v[h] * bcol[h], kb[h] * ecol[h]], axis=1)))
        w = each(lambda h: uw[h][:, HEAD:])
        kd = each(lambda h: k[h] * kdsc[h])
        vn = each(lambda h: uw[h][:, :HEAD] - _mmb(w[h], st[h]))
        d_attn = each(lambda h: jnp.where(low, _mmb_nt(dout[h], vn[h]), 0.0))
        d_vn = each(lambda h: _mmb_tn(attn[h], dout[h]) + _mmb(kd[h], dsp[h]))
        d_qd = each(lambda h: _mmb_nt(dout[h], st[h]))
        d_kd = each(lambda h: _mmb_nt(vn[h], dsp[h]))
        d_el = each(lambda h: jnp.sum(rowsum(st[h] * dsp[h]), axis=0, keepdims=True))
        dst_new = each(lambda h: _mmb_tn(q[h] * ecol[h], dout[h]) + el[h] * dsp[h] - _mmb_tn(w[h], d_vn[h]))
        d_w = each(lambda h: -_mmb_nt(d_vn[h], st[h]))
        dr = each(lambda h: _mmb_tn(t[h], jnp.concatenate([d_vn[h], d_w[h]], axis=1)))
        dru, drw = each(lambda h: dr[h][:, :HEAD]), each(lambda h: dr[h][:, HEAD:])
        d_a = each(lambda h: -jnp.where(strict, _mmb_nt(dr[h], uw[h]), 0.0))
        d_kk = each(lambda h: d_a[h] * dec[h])
        d_qk = each(lambda h: d_attn[h] * dec[h])
        d_kb = each(lambda h: _mmb(d_kk[h], k[h]) + drw[h] * ecol[h])
        dk = each(lambda h: _mmb_tn(d_kk[h], kb[h]) + _mmb_tn(d_qk[h], q[h]) + d_kb[h] * bcol[h] + d_kd[h] * kdsc[h])
        dq = each(lambda h: _mmb(d_qk[h], k[h]) + d_qd[h] * ecol[h])
        dbeta = each(lambda h: rowsum(dru[h] * v[h] + d_kb[h] * k[h]))
        de = each(lambda h: rowsum(drw[h] * kb[h] + d_qd[h] * q[h]))
        r = each(lambda h: rowsum(d_kd[h] * k[h]) * kdsc[h])
        mm = each(lambda h: d_a[h] * a[h] + d_attn[h] * attn[h])
        d_gl = each(lambda h: jnp.sum(r[h], axis=0, keepdims=True) + d_el[h] * el[h])
        d_gc = each(lambda h: de[h] * ecol[h] - r[h] + rowsum(mm[h]) - rowsum(mm[h].T)
                    + jnp.where(rowi == CHUNK - 1, d_gl[h], 0.0))
        dbeta_full = jnp.zeros((CHUNK, 128), F32)
        dgc_full = jnp.zeros((CHUNK, 128), F32)
        for h in heads:
            dqkv_ref[:, sls[h]] = dq[h]
            dqkv_ref[:, width + h * HEAD:width + (h + 1) * HEAD] = dk[h]
            dqkv_ref[:, 2 * width + h * HEAD:2 * width + (h + 1) * HEAD] = dru[h] * bcol[h]
            dstate[h] = dst_new[h]
            dbeta_full = dbeta_full + jnp.where(lane == h, dbeta[h], 0.0)
            dgc_full = dgc_full + jnp.where(lane == nh + h, d_gc[h], 0.0)
        dgb_ref[...] = dbeta_full + _mm_exact_lhs(ri <= ci, dgc_full)

    rev = lambda c: pl.BlockSpec((CHUNK, width), lambda n, c=c: (nc - 1 - n, c))
    outs = _pc(body, carry=carry, name="gdn_bwd", grid=(nc,),
               in_specs=[rev(0), rev(1), rev(2), pl.BlockSpec((CHUNK, 128), lambda n: (nc - 1 - n, 0)),
                         pl.BlockSpec((8, 128), lambda n: (0, 0)), rev(0),
                         pl.BlockSpec((1, nh, HEAD, HEAD), lambda n: (nc - 1 - n, 0, 0, 0)),
                         pl.BlockSpec((1, nh, CHUNK, CHUNK), lambda n: (nc - 1 - n, 0, 0, 0))],
               out_specs=[pl.BlockSpec((CHUNK, 3 * width), lambda n: (nc - 1 - n, 0)),
                          pl.BlockSpec((CHUNK, 128), lambda n: (nc - 1 - n, 0))],
               out_shape=[jax.ShapeDtypeStruct((s, 3 * width), F32), jax.ShapeDtypeStruct((s, 128), F32)],
               scratch_shapes=[pltpu.VMEM((nh, HEAD, HEAD), F32)],
               compiler_params=_cp(("arbitrary",)))(qkv, qkv, qkv, ba, gp, do, states, tinv)
    (dqkv, dgb), got = outs if carry is not None else (outs, [])
    return dqkv, dgb, got


def _gates_bwd(ba, dgb, gp, nh):
    s = ba.shape[0]
    tm = _tile(s, (512, 256, 128, 64))

    def body(ba_ref, d_ref, gp_ref, o_ref, dp_ref):
        @pl.when(pl.program_id(0) == 0)
        def _():
            dp_ref[...] = jnp.zeros_like(dp_ref)

        bav, dv = ba_ref[...], d_ref[...]
        lane = lax.broadcasted_iota(jnp.int32, bav.shape, 1)
        beta = _sigmoid(bav)
        amp = jnp.exp(gp_ref[0:1, :])
        z = bav + gp_ref[1:2, :]
        d_a = dv * (-amp) * _sigmoid(z)
        d_b = dv * beta * (1.0 - beta)
        is_a = (lane >= nh) & (lane < 2 * nh)
        o_ref[...] = jnp.where(lane < nh, d_b, jnp.where(is_a, d_a, 0.0))
        dp_ref[0:1, :] += jnp.sum(jnp.where(is_a, dv * (-amp) * _softplus(z), 0.0), axis=0, keepdims=True)
        dp_ref[1:2, :] += jnp.sum(jnp.where(is_a, d_a, 0.0), axis=0, keepdims=True)

    row = pl.BlockSpec((tm, 128), lambda i: (i, 0))
    par = pl.BlockSpec((8, 128), lambda i: (0, 0))
    return _pc(body, name="gates_bwd", grid=(s // tm,), in_specs=[row, row, par], out_specs=[row, par],
               out_shape=[jax.ShapeDtypeStruct((s, 128), F32), jax.ShapeDtypeStruct((8, 128), F32)],
               compiler_params=_cp(("arbitrary",)))(ba, dgb, gp)


def _mix_post_fwd(o_raw, proj, gain, sc_w, width):
    s = o_raw.shape[0]
    tm = _tile(s, (256, 128, 64))
    nh = width // HEAD
    ksc = sc_w.shape[0]

    def body(o_ref, z_ref, b_ref, c_ref, h_ref, cp_ref, hp_ref, g_ref, w_ref, y_ref):
        i = pl.program_id(0)
        z = z_ref[...].astype(F32)
        sz = z * _sigmoid(z)
        for h in range(nh):
            sl = slice(h * HEAD, (h + 1) * HEAD)
            o = o_ref[:, sl]
            r = lax.rsqrt(jnp.mean(o * o, axis=-1, keepdims=True) + EPS)
            y_ref[:, sl] = (o * r * g_ref[...] * sz[:, sl]).astype(BF16)
        prod = c_ref[...].astype(F32) * h_ref[...].astype(F32)
        pprev = jnp.where(i > 0, cp_ref[...].astype(F32) * hp_ref[...].astype(F32), 0.0)
        cv = prod * w_ref[ksc - 1:ksc, :]
        for j in range(ksc - 1):
            cv = cv + _shift_down(prod, pprev, ksc - 1 - j) * w_ref[j:j + 1, :]
        y_ref[:, width:] = (b_ref[...].astype(F32) * cv).astype(BF16)

    col = lambda c: pl.BlockSpec((tm, width), lambda i, c=c: (i, c))
    prv = lambda c: pl.BlockSpec((HALO, width), lambda i, c=c: (jnp.maximum(i * (tm // HALO) - 1, 0), c))
    return _pc(body, name="mix_post_fwd", grid=(s // tm,),
               in_specs=[col(0), col(0), col(1), col(2), col(3), prv(2), prv(3),
                         pl.BlockSpec((1, HEAD), lambda i: (0, 0)), pl.BlockSpec((ksc, width), lambda i: (0, 0))],
               out_specs=pl.BlockSpec((tm, 2 * width), lambda i: (i, 0)),
               out_shape=jax.ShapeDtypeStruct((s, 2 * width), BF16),
               compiler_params=_cp(("parallel",)))(o_raw, proj, proj, proj, proj, proj, proj,
                                                    gain.reshape(1, HEAD), sc_w)


def _mix_post_bwd(dy, o_raw, proj, gain, sc_w, width, carry=None):
    s = o_raw.shape[0]
    tm = _tile(s, (256, 128, 64))
    nt = s // tm
    nh = width // HEAD
    ksc = sc_w.shape[0]

    def body(dyg_ref, dys_ref, dysn_ref, o_ref, z_ref, b_ref, bn_ref, c_ref, h_ref, cp_ref, hp_ref, g_ref, w_ref,
             do_ref, dp_ref, dg_ref, dw_ref):
        i = pl.program_id(0)

        @pl.when(i == 0)
        def _():
            dg_ref[...] = jnp.zeros_like(dg_ref)
            dw_ref[...] = jnp.zeros_like(dw_ref)

        z = z_ref[...].astype(F32)
        sg = _sigmoid(z)
        sz = z * sg
        dsz = sg * (1.0 + z * (1.0 - sg))
        dyg = dyg_ref[...].astype(F32)
        dgain = jnp.zeros((1, HEAD), F32)
        for h in range(nh):
            sl = slice(h * HEAD, (h + 1) * HEAD)
            o = o_ref[:, sl]
            r = lax.rsqrt(jnp.mean(o * o, axis=-1, keepdims=True) + EPS)
            oh = o * r
            d_yn = dyg[:, sl] * sz[:, sl]
            dp_ref[:, sl] = (dyg[:, sl] * oh * g_ref[...] * dsz[:, sl]).astype(BF16)
            dgain = dgain + jnp.sum(d_yn * oh, axis=0, keepdims=True)
            doh = d_yn * g_ref[...]
            do_ref[:, sl] = r * (doh - oh * jnp.mean(doh * oh, axis=-1, keepdims=True))
        dg_ref[0:1, :] += dgain
        cc, hh = c_ref[...].astype(F32), h_ref[...].astype(F32)
        prod = cc * hh
        pprev = jnp.where(i > 0, cp_ref[...].astype(F32) * hp_ref[...].astype(F32), 0.0)
        sh = [_shift_down(prod, pprev, ksc - 1 - j) for j in range(ksc)]
        cv = sh[0] * w_ref[0:1, :]
        for j in range(1, ksc):
            cv = cv + sh[j] * w_ref[j:j + 1, :]
        dys = dys_ref[...].astype(F32)
        dp_ref[:, width:2 * width] = (dys * cv).astype(BF16)
        dcv_n = jnp.where(i < nt - 1, dysn_ref[...].astype(F32) * bn_ref[...].astype(F32), 0.0)
        dcv = jnp.concatenate([dys * b_ref[...].astype(F32), dcv_n], axis=0)
        dprod = dcv[:tm] * w_ref[ksc - 1:ksc, :]
        for j in range(ksc - 1):
            dprod = dprod + _shift_up(dcv, ksc - 1 - j, tm) * w_ref[j:j + 1, :]
        dp_ref[:, 2 * width:3 * width] = (dprod * hh).astype(BF16)
        dp_ref[:, 3 * width:] = (dprod * cc).astype(BF16)
        for j in range(ksc):
            dw_ref[j:j + 1, :] += jnp.sum(dcv[:tm] * sh[j], axis=0, keepdims=True)

    col = lambda c: pl.BlockSpec((tm, width), lambda i, c=c: (i, c))
    prv = lambda c: pl.BlockSpec((HALO, width), lambda i, c=c: (jnp.maximum(i * (tm // HALO) - 1, 0), c))
    nxt = lambda c: pl.BlockSpec((HALO, width), lambda i, c=c: (jnp.minimum((i + 1) * (tm // HALO), s // HALO - 1), c))
    outs = _pc(
        body, carry=carry, name="mix_post_bwd", grid=(nt,),
        in_specs=[col(0), col(1), nxt(1), col(0), col(0), col(1), nxt(1), col(2), col(3), prv(2), prv(3),
                  pl.BlockSpec((1, HEAD), lambda i: (0, 0)), pl.BlockSpec((ksc, width), lambda i: (0, 0))],
        out_specs=[col(0), pl.BlockSpec((tm, 4 * width), lambda i: (i, 0)),
                   pl.BlockSpec((8, HEAD), lambda i: (0, 0)), pl.BlockSpec((8, width), lambda i: (0, 0))],
        out_shape=[jax.ShapeDtypeStruct((s, width), F32), jax.ShapeDtypeStruct((s, 7 * width), BF16),
                   jax.ShapeDtypeStruct((8, HEAD), F32), jax.ShapeDtypeStruct((8, width), F32)],
        compiler_params=_cp(("arbitrary",)))(dy, dy, dy, o_raw, proj, proj, proj, proj, proj, proj, proj,
                                             gain.reshape(1, HEAD), sc_w)
    (do, dp, dg, dw), got = outs if carry is not None else (outs, [])
    return do, dp, dg[0], dw[:ksc], got


def _xattn_fwd(q, k, v):
    s, d = q.shape
    nm = k.shape[0]
    dh = d // XHEADS
    tm = _tile(s, (512, 256, 128, 64))

    def body(q_ref, k_ref, v_ref, o_ref):
        sc = _mmb_nt(q_ref[...], k_ref[...]) * (dh ** -0.5)
        p = jnp.exp(sc - jnp.max(sc, axis=-1, keepdims=True))
        p = p / jnp.sum(p, axis=-1, keepdims=True)
        o_ref[...] = _mmb(p, v_ref[...]).astype(BF16)

    return _pc(body, name="xattn_fwd", grid=(s // tm, XHEADS),
               in_specs=[pl.BlockSpec((tm, dh), lambda i, h: (i, h)), pl.BlockSpec((nm, dh), lambda i, h: (0, h)),
                         pl.BlockSpec((nm, dh), lambda i, h: (0, h))],
               out_specs=pl.BlockSpec((tm, dh), lambda i, h: (i, h)), out_shape=jax.ShapeDtypeStruct((s, d), BF16),
               compiler_params=_cp(("parallel", "parallel")))(q, k, v)


def _xattn_bwd(q, k, v, do, carry=None):
    s, d = q.shape
    nm = k.shape[0]
    dh = d // XHEADS
    tm = _tile(s, (512, 256, 128, 64))

    def body(q_ref, k_ref, v_ref, do_ref, dq_ref, dk_ref, dv_ref):
        @pl.when(pl.program_id(1) == 0)
        def _():
            dk_ref[...] = jnp.zeros_like(dk_ref)
            dv_ref[...] = jnp.zeros_like(dv_ref)

        scale = dh ** -0.5
        sc = _mmb_nt(q_ref[...], k_ref[...]) * scale
        p = jnp.exp(sc - jnp.max(sc, axis=-1, keepdims=True))
        p = p / jnp.sum(p, axis=-1, keepdims=True)
        dp = _mmb_nt(do_ref[...], v_ref[...])
        ds = p * (dp - jnp.sum(dp * p, axis=-1, keepdims=True)) * scale
        dq_ref[...] = _mmb(ds, k_ref[...]).astype(BF16)
        dk_ref[...] += _mmb_tn(ds, q_ref[...])
        dv_ref[...] += _mmb_tn(p, do_ref[...])

    rowb = pl.BlockSpec((tm, dh), lambda h, i: (i, h))
    memb = pl.BlockSpec((nm, dh), lambda h, i: (0, h))
    outs = _pc(body, carry=carry, name="xattn_bwd", grid=(XHEADS, s // tm), in_specs=[rowb, memb, memb, rowb],
               out_specs=[rowb, memb, memb],
               out_shape=[jax.ShapeDtypeStruct((s, d), BF16), jax.ShapeDtypeStruct((nm, d), F32),
                          jax.ShapeDtypeStruct((nm, d), F32)],
               compiler_params=_cp(("parallel", "arbitrary")))(q, k, v, do)
    (dq, dk, dv), got = outs if carry is not None else (outs, [])
    return dq, dk, dv, got


def _ffn_act_fwd(u_pre, conv_w):
    s, f2 = u_pre.shape
    f = f2 // 2
    tm = _tile(s, (256, 128, 64))
    cb = _tile(f, (512, 256, 128))
    nf = f // cb
    kf = conv_w.shape[0]

    def body(g_ref, u_ref, gp_ref, up_ref, wg_ref, wu_ref, a_ref):
        i = pl.program_id(0)

        def conv(c_ref, p_ref, w_ref):
            cur = c_ref[...].astype(F32)
            prev = jnp.where(i > 0, p_ref[...].astype(F32), 0.0)
            out = cur * w_ref[kf - 1:kf, :]
            for j in range(kf - 1):
                out = out + _shift_down(cur, prev, kf - 1 - j) * w_ref[j:j + 1, :]
            return out

        gate, up = conv(g_ref, gp_ref, wg_ref), conv(u_ref, up_ref, wu_ref)
        a_ref[...] = (gate * _sigmoid(gate) * up).astype(BF16)

    return _pc(body, name="ffn_act_fwd", grid=(s // tm, nf),
               in_specs=[pl.BlockSpec((tm, cb), lambda i, j: (i, j)), pl.BlockSpec((tm, cb), lambda i, j: (i, j + nf)),
                         pl.BlockSpec((HALO, cb), _prev_map(tm, lambda j: j)),
                         pl.BlockSpec((HALO, cb), _prev_map(tm, lambda j: j + nf)),
                         pl.BlockSpec((kf, cb), lambda i, j: (0, j)), pl.BlockSpec((kf, cb), lambda i, j: (0, j + nf))],
               out_specs=pl.BlockSpec((tm, cb), lambda i, j: (i, j)), out_shape=jax.ShapeDtypeStruct((s, f), BF16),
               compiler_params=_cp(("parallel", "parallel")))(u_pre, u_pre, u_pre, u_pre, conv_w, conv_w)


def _ffn_act_bwd(da, u_pre, conv_w):
    s, f2 = u_pre.shape
    f = f2 // 2
    tm = _tile(s, (256, 128, 64))
    nt = s // tm
    cb = _tile(f, (512, 256, 128))
    nf = f // cb
    kf = conv_w.shape[0]

    def body(da_ref, dan_ref, g_ref, gp_ref, gn_ref, u_ref, up_ref, un_ref, wg_ref, wu_ref, d_ref, dw_ref):
        i = pl.program_id(1)

        @pl.when(i == 0)
        def _():
            dw_ref[...] = jnp.zeros_like(dw_ref)

        def conv(c_ref, p_ref, n_ref, w_ref):
            ext = jnp.concatenate([c_ref[...].astype(F32), n_ref[...].astype(F32)], axis=0)
            prev = jnp.where(i > 0, p_ref[...].astype(F32), 0.0)
            sh = [_shift_down(ext, prev, kf - 1 - j) for j in range(kf)]
            out = sh[0] * w_ref[0:1, :]
            for j in range(1, kf):
                out = out + sh[j] * w_ref[j:j + 1, :]
            return out, sh

        gate, gsh = conv(g_ref, gp_ref, gn_ref, wg_ref)
        up, ush = conv(u_ref, up_ref, un_ref, wu_ref)
        dav = jnp.concatenate([da_ref[...].astype(F32), dan_ref[...].astype(F32)], axis=0)
        rows = lax.broadcasted_iota(jnp.int32, (tm + HALO, 1), 0)
        dav = jnp.where((rows < tm) | (i < nt - 1), dav, 0.0)
        sg = _sigmoid(gate)
        dgate = dav * up * (sg * (1.0 + gate * (1.0 - sg)))
        dup = dav * (gate * sg)

        def conv_t(dv, w_ref):
            out = dv[:tm] * w_ref[kf - 1:kf, :]
            for j in range(kf - 1):
                out = out + _shift_up(dv, kf - 1 - j, tm) * w_ref[j:j + 1, :]
            return out

        d_ref[0] = conv_t(dgate, wg_ref).astype(BF16)
        d_ref[1] = conv_t(dup, wu_ref).astype(BF16)
        for j in range(kf):
            dw_ref[0, j:j + 1, :] += jnp.sum(dgate[:tm] * gsh[j][:tm], axis=0, keepdims=True)
            dw_ref[1, j:j + 1, :] += jnp.sum(dup[:tm] * ush[j][:tm], axis=0, keepdims=True)

    pm = lambda off: (lambda j, i: _prev_map(tm, lambda c: c + off)(i, j))
    nm = lambda off: (lambda j, i: _next_map(tm, s, lambda c: c + off)(i, j))
    du, dw = _pc(
        body, name="ffn_act_bwd", grid=(nf, nt),
        in_specs=[pl.BlockSpec((tm, cb), lambda j, i: (i, j)), pl.BlockSpec((HALO, cb), nm(0)),
                  pl.BlockSpec((tm, cb), lambda j, i: (i, j)), pl.BlockSpec((HALO, cb), pm(0)),
                  pl.BlockSpec((HALO, cb), nm(0)),
                  pl.BlockSpec((tm, cb), lambda j, i: (i, j + nf)), pl.BlockSpec((HALO, cb), pm(nf)),
                  pl.BlockSpec((HALO, cb), nm(nf)),
                  pl.BlockSpec((kf, cb), lambda j, i: (0, j)), pl.BlockSpec((kf, cb), lambda j, i: (0, j + nf))],
        out_specs=[pl.BlockSpec((2, tm, cb), lambda j, i: (0, i, j)), pl.BlockSpec((2, 8, cb), lambda j, i: (0, 0, j))],
        out_shape=[jax.ShapeDtypeStruct((2, s, f), BF16), jax.ShapeDtypeStruct((2, 8, f), F32)],
        compiler_params=_cp(("parallel", "arbitrary")))(da, da, u_pre, u_pre, u_pre, u_pre, u_pre, u_pre, conv_w, conv_w)
    return du, dw[:, :kf]


def _adamw(w, g, m, v, name):
    shape = w.shape
    c = shape[-1]
    r = w.size // c
    tr = r if r * c <= 262144 else _tile(r, tuple(t for t in (512, 256, 128, 64, 32, 16, 8) if t * c <= 262144))
    bc1 = 1.0 - ADAM_B1 ** ADAM_STEP
    bc2 = 1.0 - ADAM_B2 ** ADAM_STEP

    def body(w_ref, g_ref, m_ref, v_ref, d_ref, nm_ref, nv_ref):
        gv = g_ref[...]
        mn = ADAM_B1 * m_ref[...] + (1.0 - ADAM_B1) * gv
        vn = ADAM_B2 * v_ref[...] + (1.0 - ADAM_B2) * (gv * gv)
        nm_ref[...] = mn
        nv_ref[...] = vn
        d_ref[...] = -ADAM_LR * ((mn / bc1) / (jnp.sqrt(vn / bc2) + ADAM_EPS) + ADAM_WD * w_ref[...])

    blk = pl.BlockSpec((tr, c), lambda i: (i, 0))
    outs = _pc(body, name=name, grid=(r // tr,), in_specs=[blk] * 4, out_specs=[blk] * 3,
               out_shape=[jax.ShapeDtypeStruct((r, c), F32)] * 3,
               compiler_params=_cp(("parallel",)))(*(t.reshape(r, c) for t in (w, g, m, v)))
    return tuple(o.reshape(shape) for o in outs)


def _adamw_sharded(w, m, v, partials, name):
    nl, r, c = w.shape
    tr = _tile(r, tuple(t for t in (256, 128, 64, 32, 16, 8) if t * c <= 131072))
    bc1 = 1.0 - ADAM_B1 ** ADAM_STEP
    bc2 = 1.0 - ADAM_B2 ** ADAM_STEP

    def body(w_ref, m_ref, v_ref, *rest):
        p_refs, (g_ref, d_ref, nm_ref, nv_ref) = rest[:nl], rest[nl:]
        layer = pl.program_id(0)
        for l in range(nl):
            @pl.when(layer == l)
            def _(p_ref=p_refs[l]):
                gv = p_ref[0].astype(F32)
                for dev in range(1, N_DEV):
                    gv = gv + p_ref[dev].astype(F32)
                mn = ADAM_B1 * m_ref[...] + (1.0 - ADAM_B1) * gv
                vn = ADAM_B2 * v_ref[...] + (1.0 - ADAM_B2) * (gv * gv)
                g_ref[...] = gv
                nm_ref[...] = mn
                nv_ref[...] = vn
                d_ref[...] = -ADAM_LR * ((mn / bc1) / (jnp.sqrt(vn / bc2) + ADAM_EPS) + ADAM_WD * w_ref[...])

    blk = pl.BlockSpec((None, tr, c), lambda l, i: (l, i, 0))
    p_specs = [pl.BlockSpec((N_DEV, tr, c), lambda l, i, k=k: (0, jnp.where(l == k, i, 0), 0)) for k in range(nl)]
    return _pc(body, name=name, grid=(nl, r // tr), in_specs=[blk] * 3 + p_specs, out_specs=[blk] * 4,
               out_shape=[jax.ShapeDtypeStruct((nl, r, c), F32)] * 4,
               compiler_params=_cp(("arbitrary", "arbitrary")))(w, m, v, *partials)


def _slot_sum(x, name):
    _, r, c = x.shape
    tr = _tile(r, (512, 256, 128, 64, 32, 16, 8))

    def body(x_ref, o_ref):
        acc = x_ref[0].astype(F32)
        for d in range(1, N_DEV):
            acc = acc + x_ref[d].astype(F32)
        o_ref[...] = acc

    return _pc(body, name=name, grid=(r // tr,), in_specs=[pl.BlockSpec((N_DEV, tr, c), lambda i: (0, i, 0))],
               out_specs=pl.BlockSpec((tr, c), lambda i: (i, 0)), out_shape=jax.ShapeDtypeStruct((r, c), F32),
               compiler_params=_cp(("parallel",)))(x)


def _all_gather(xs, name, layers=None):
    na = len(xs)
    layers = tuple(range(xs[0].shape[0])) if layers is None else layers
    nl = len(layers) * na

    def body(*refs):
        x_refs, out_refs = refs[:na], refs[na:na + nl]
        send_sems, recv_sems, local_sems = refs[na + nl:]
        mx, my, mc = lax.axis_index("x"), lax.axis_index("y"), lax.axis_index("c")
        me, sibling = (mx, my, mc), (mx, my, 1 - mc)
        chips = [(1 - mx, my), (mx, 1 - my), (1 - mx, 1 - my)]

        def local(l):
            return x_refs[l % na].at[layers[l // na]]

        def slot(l, dev):
            return out_refs[l].at[4 * dev[0] + 2 * dev[1] + dev[2]]

        def copy(l, k, block, to, src=None):
            return pltpu.make_async_remote_copy(
                src_ref=slot(l, block) if src is None else src, dst_ref=slot(l, block),
                send_sem=send_sems.at[l, k], recv_sem=recv_sems.at[l, k], device_id=to, device_id_type=MESH)

        mine = [pltpu.make_async_copy(local(l), slot(l, me), local_sems.at[l]) for l in range(nl)]
        first = []
        for l in range(nl):
            mine[l].start()
            first.append(copy(l, 0, me, sibling, src=local(l)))
            first += [copy(l, 1 + j, me, (*chip, mc), src=local(l)) for j, chip in enumerate(chips)]
        for cp in first:
            cp.start()
        passed = []
        for l in range(nl):
            for j, chip in enumerate(chips):
                copy(l, 1 + j, (*chip, mc), me).wait_recv()
                fw = copy(l, 4 + j, (*chip, mc), sibling)
                fw.start()
                passed.append(fw)
        for l in range(nl):
            copy(l, 0, sibling, me).wait_recv()
            for j, chip in enumerate(chips):
                copy(l, 4 + j, (*chip, 1 - mc), me).wait_recv()
        for cp in first + passed:
            cp.wait_send()
        for l in range(nl):
            mine[l].wait()

    anyspec = pl.BlockSpec(memory_space=pl.ANY)
    outs = _pc(body, name=name, in_specs=[anyspec] * na, out_specs=[anyspec] * nl,
               out_shape=[jax.ShapeDtypeStruct((N_DEV,) + xs[l % na].shape[1:], xs[l % na].dtype) for l in range(nl)],
               scratch_shapes=[pltpu.SemaphoreType.DMA((nl, 7)), pltpu.SemaphoreType.DMA((nl, 7)),
                               pltpu.SemaphoreType.DMA((nl,))],
               compiler_params=_cp())(*xs)
    return [outs[l * na:(l + 1) * na] for l in range(nl // na)]


def _mix_in_assemble(shards, width, nh):
    _, d, cs = shards.shape
    tr = _tile(d, (128, 64, 32, 16))
    w4 = 4 * width

    def body(s_ref, main_ref, gate_ref):
        full = jnp.concatenate([s_ref[j].astype(F32) for j in range(N_DEV)], axis=1)
        main_ref[:, :width] = full[:, 3 * width:w4].astype(BF16)
        main_ref[:, width:w4] = full[:, w4 + 2 * nh:].astype(BF16)
        main_ref[:, w4:] = full[:, :3 * width].astype(BF16)
        gate_ref[...] = jnp.concatenate([full[:, w4:w4 + 2 * nh], jnp.zeros((tr, 128 - 2 * nh), F32)],
                                        axis=1).astype(BF16)

    return _pc(body, name="mix_in_assemble", grid=(d // tr,),
               in_specs=[pl.BlockSpec((N_DEV, tr, cs), lambda i: (0, i, 0))],
               out_specs=[pl.BlockSpec((tr, 7 * width), lambda i: (i, 0)), pl.BlockSpec((tr, 128), lambda i: (i, 0))],
               out_shape=[jax.ShapeDtypeStruct((d, 7 * width), BF16), jax.ShapeDtypeStruct((d, 128), BF16)],
               compiler_params=_cp(("parallel",)))(shards)


def _mix_in_shards(dmain, dgate, width, nh):
    d = dmain.shape[0]
    cs = (7 * width + 2 * nh) // N_DEV
    tr = _tile(d, (128, 64, 32, 16))
    w4 = 4 * width

    def body(main_ref, gate_ref, o_ref):
        full = jnp.concatenate([main_ref[:, w4:], main_ref[:, :width], gate_ref[:, :2 * nh], main_ref[:, width:w4]],
                               axis=1)
        for j in range(N_DEV):
            o_ref[j] = full[:, j * cs:(j + 1) * cs].astype(BF16)

    return _pc(body, name="mix_in_shards", grid=(d // tr,),
               in_specs=[pl.BlockSpec((tr, 7 * width), lambda i: (i, 0)), pl.BlockSpec((tr, 128), lambda i: (i, 0))],
               out_specs=pl.BlockSpec((N_DEV, tr, cs), lambda i: (0, i, 0)),
               out_shape=jax.ShapeDtypeStruct((N_DEV, d, cs), BF16),
               compiler_params=_cp(("parallel",)))(dmain, dgate)


PACK_COLS = 1024
BIG = ("w_mix_in", "w_mix_out", "w_xq", "w_xk", "w_xv", "w_xo", "w_ffn_up", "w_ffn_down")


def _pack(blocks, lead):
    flat = jnp.concatenate(blocks, axis=-1)
    n = flat.shape[-1]
    per = 16 * PACK_COLS
    pad = (-n) % per
    flat = jnp.pad(flat, [(0, 0)] * len(lead) + [(0, pad)])
    return flat.reshape(*lead, (n + pad) // PACK_COLS, PACK_COLS)


def _unpack(packed, sizes):
    flat = packed.reshape(*packed.shape[:-2], -1)
    out, off = [], 0
    for n in sizes:
        out.append(flat[..., off:off + n])
        off += n
    return out


def kernel(x, mem, mix_norm, w_mix_in, gdn_conv, gdn_a_log, gdn_dt_bias, gdn_out_norm, sc_conv, w_mix_out, xattn_norm, mem_norm, w_xq, w_xk, w_xv, w_xo, ffn_norm, w_ffn_up, ffn_conv, w_ffn_down, final_norm, loss_target, m_mix_norm, m_w_mix_in, m_gdn_conv, m_gdn_a_log, m_gdn_dt_bias, m_gdn_out_norm, m_sc_conv, m_w_mix_out, m_xattn_norm, m_mem_norm, m_w_xq, m_w_xk, m_w_xv, m_w_xo, m_ffn_norm, m_w_ffn_up, m_ffn_conv, m_w_ffn_down, m_final_norm, v_mix_norm, v_w_mix_in, v_gdn_conv, v_gdn_a_log, v_gdn_dt_bias, v_gdn_out_norm, v_sc_conv, v_w_mix_out, v_xattn_norm, v_mem_norm, v_w_xq, v_w_xk, v_w_xv, v_w_xo, v_ffn_norm, v_w_ffn_up, v_ffn_conv, v_w_ffn_down, v_final_norm):
    names = ["mix_norm", "w_mix_in", "gdn_conv", "gdn_a_log", "gdn_dt_bias", "gdn_out_norm", "sc_conv", "w_mix_out",
             "xattn_norm", "mem_norm", "w_xq", "w_xk", "w_xv", "w_xo", "ffn_norm", "w_ffn_up", "ffn_conv",
             "w_ffn_down", "final_norm"]
    wts = dict(zip(names, (mix_norm, w_mix_in, gdn_conv, gdn_a_log, gdn_dt_bias, gdn_out_norm, sc_conv, w_mix_out,
                           xattn_norm, mem_norm, w_xq, w_xk, w_xv, w_xo, ffn_norm, w_ffn_up, ffn_conv, w_ffn_down,
                           final_norm)))
    mom1 = dict(zip(names, (m_mix_norm, m_w_mix_in, m_gdn_conv, m_gdn_a_log, m_gdn_dt_bias, m_gdn_out_norm, m_sc_conv,
                            m_w_mix_out, m_xattn_norm, m_mem_norm, m_w_xq, m_w_xk, m_w_xv, m_w_xo, m_ffn_norm,
                            m_w_ffn_up, m_ffn_conv, m_w_ffn_down, m_final_norm)))
    mom2 = dict(zip(names, (v_mix_norm, v_w_mix_in, v_gdn_conv, v_gdn_a_log, v_gdn_dt_bias, v_gdn_out_norm, v_sc_conv,
                            v_w_mix_out, v_xattn_norm, v_mem_norm, v_w_xq, v_w_xk, v_w_xv, v_w_xo, v_ffn_norm,
                            v_w_ffn_up, v_ffn_conv, v_w_ffn_down, v_final_norm)))

    x0 = x[0]
    memv = mem[0]
    target = loss_target[0]
    s, d = x0.shape
    depth = mix_norm.shape[0]
    width = d // 2
    nh = width // HEAD
    me = 4 * lax.axis_index("x") + 2 * lax.axis_index("y") + lax.axis_index("c")

    local_bf16 = {n: wts[n].astype(BF16) for n in BIG}
    gathered = [None] * depth
    gathered[0] = dict(zip(BIG, _all_gather([local_bf16[n] for n in BIG], "all_gather_weights", layers=(0,))[0]))

    def gather_next(l, group):
        return None if l + 1 >= depth else ("gather", [local_bf16[n] for n in group], l + 1)

    def mm_gathering(l, group, *args, **kw):
        carry = gather_next(l, group)
        if carry is None:
            return _mm(*args, **kw)
        out, got = _mm(*args, carry=carry, **kw)
        gathered[l + 1].update(zip(group, got))
        return out

    conv_names = ("gdn_conv", "sc_conv", "ffn_conv")
    conv_local = _pack([wts[n].reshape(1, -1) for n in conv_names], (1,))
    conv_all = _all_gather([conv_local], "all_gather_conv")[0][0]
    conv_parts = _unpack(conv_all, [wts[n].size for n in conv_names])
    conv_full = {}
    for n, part in zip(conv_names, conv_parts):
        _, kt, cs = wts[n].shape
        conv_full[n] = jnp.moveaxis(part.reshape(N_DEV, depth, kt, cs), 0, 2).reshape(depth, kt, N_DEV * cs)

    def layer_weights(l):
        wl = {n: gathered[l][n].reshape(-1, gathered[l][n].shape[-1]) for n in BIG if n not in ("w_mix_in", "w_ffn_up")}
        wl["w_ffn_up"] = gathered[l]["w_ffn_up"]
        return wl

    def gate_params(l):
        rows = jnp.stack([gdn_a_log[l], gdn_dt_bias[l]])
        return jnp.pad(rows, ((0, 6), (nh, 128 - 2 * nh)))

    saved = []
    xc = x0
    for l in range(depth):
        wl = layer_weights(l)
        w_main, w_gate = _mix_in_assemble(gathered[l]["w_mix_in"], width, nh)
        gp = gate_params(l)
        if l + 1 < depth:
            gathered[l + 1] = {}
        h1 = _rms_fwd(xc, mix_norm[l], "rms_mix")
        proj = mm_gathering(l, ("w_mix_in",), h1, w_main, out_dtype=BF16, name="mm_mix_in")
        ba = _mm(h1, w_gate, out_dtype=F32, name="mm_mix_gates")
        qkv = _gdn_prep_fwd(proj, conv_full["gdn_conv"][l], width)
        with_gdn = ("w_mix_out", "w_xq")
        o_raw, states, tinv, got = _gdn_fwd(qkv, ba, gp, width, carry=gather_next(l, with_gdn))
        if got:
            gathered[l + 1].update(zip(with_gdn, got))
        y = _mix_post_fwd(o_raw, proj, gdn_out_norm[l], conv_full["sc_conv"][l], width)
        x1 = mm_gathering(l, ("w_xk",), y, wl["w_mix_out"], res=xc, name="mm_mix_out")
        h2 = _rms_fwd(x1, xattn_norm[l], "rms_xattn")
        mem_n = _rms_fwd(memv, mem_norm[l], "rms_mem")
        qx = mm_gathering(l, ("w_xv",), h2, wl["w_xq"], out_dtype=BF16, name="mm_xq")
        kx = _mm(mem_n, wl["w_xk"], out_dtype=BF16, name="mm_xk")
        vx = _mm(mem_n, wl["w_xv"], out_dtype=BF16, name="mm_xv")
        ox = _xattn_fwd(qx, kx, vx)
        x2 = mm_gathering(l, ("w_xo",), ox, wl["w_xo"], res=x1, name="mm_xo")
        h3 = _rms_fwd(x2, ffn_norm[l], "rms_ffn")
        u_pre = mm_gathering(l, ("w_ffn_up",), h3, wl["w_ffn_up"], b_shards=True, out_dtype=BF16, name="mm_ffn_up")
        act = _ffn_act_fwd(u_pre, conv_full["ffn_conv"][l])
        x3 = mm_gathering(l, ("w_ffn_down",), act, wl["w_ffn_down"], res=x2, name="mm_ffn_down")
        saved.append(dict(x0=xc, x1=x1, x2=x2, h1=h1, h2=h2, h3=h3, proj=proj, ba=ba, qkv=qkv, o_raw=o_raw,
                          states=states, tinv=tinv, y=y, mem_n=mem_n, qx=qx, kx=kx, vx=vx, ox=ox, u_pre=u_pre, act=act,
                          w_main=w_main, w_gate=w_gate, gp=gp, wl=wl))
        xc = x3

    loss_part, dx, dxb, g_final = _loss_head(xc, final_norm, target)

    small = {n: [None] * depth for n in ("mix_norm", "xattn_norm", "mem_norm", "ffn_norm", "gdn_a_log", "gdn_dt_bias",
                                          "gdn_out_norm", "gdn_conv", "sc_conv", "ffn_conv")}
    partials = {n: [None] * depth for n in BIG}

    def scatter(big, group):
        return ("scatter", [big[n].reshape((N_DEV,) + wts[n].shape[1:]) for n in group], None)

    for l in reversed(range(depth)):
        sv = saved[l]
        wl, w_main, w_gate, gp = sv["wl"], sv["w_main"], sv["w_gate"], sv["gp"]
        big = {}
        d_act = _mm(dxb, wl["w_ffn_down"], tb=True, out_dtype=BF16, name="mm_d_act")
        big["w_ffn_down"] = _mm(sv["act"], dxb, ta=True, out_dtype=BF16, name="mm_dw_ffn_down")
        du, dcw = _ffn_act_bwd(d_act, sv["u_pre"], conv_full["ffn_conv"][l])
        small["ffn_conv"][l] = jnp.concatenate([dcw[0], dcw[1]], axis=1)
        big["w_ffn_up"], (partials["w_ffn_down"][l],) = _mm(
            sv["h3"], du, ta=True, b_halves=True, out_shards=True, out_dtype=BF16, name="mm_dw_ffn_up",
            carry=scatter(big, ("w_ffn_down",)))
        dh, (partials["w_ffn_up"][l],) = _mm(du, wl["w_ffn_up"], tb=True, a_halves=True, b_shards=True, out_dtype=BF16,
                                             name="mm_dh3", carry=scatter(big, ("w_ffn_up",)))
        dx, dxb, small["ffn_norm"][l], _ = _rms_bwd(sv["x2"], dh, ffn_norm[l], dx, "rms_bwd_ffn")
        d_ox = _mm(dxb, wl["w_xo"], tb=True, out_dtype=BF16, name="mm_d_ox")
        big["w_xo"] = _mm(sv["ox"], dxb, ta=True, out_dtype=BF16, name="mm_dw_xo")
        d_qx, d_kx, d_vx, (partials["w_xo"][l],) = _xattn_bwd(sv["qx"], sv["kx"], sv["vx"], d_ox,
                                                               carry=scatter(big, ("w_xo",)))
        big["w_xq"] = _mm(sv["h2"], d_qx, ta=True, out_dtype=BF16, name="mm_dw_xq")
        big["w_xk"] = _mm(sv["mem_n"], d_kx, ta=True, out_dtype=BF16, name="mm_dw_xk")
        big["w_xv"] = _mm(sv["mem_n"], d_vx, ta=True, out_dtype=BF16, name="mm_dw_xv")
        dh, (partials["w_xq"][l],) = _mm(d_qx, wl["w_xq"], tb=True, out_dtype=BF16, name="mm_dh2",
                                         carry=scatter(big, ("w_xq",)))
        d_mem = _mm(d_kx, wl["w_xk"], tb=True, out_dtype=F32, name="mm_dmem_k")
        d_mem = _mm(d_vx, wl["w_xv"], tb=True, out_dtype=F32, res=d_mem, name="mm_dmem_v")
        _, _, small["mem_norm"][l], _ = _rms_bwd(memv, d_mem, mem_norm[l], jnp.zeros_like(memv), "rms_bwd_mem")
        dx, dxb, small["xattn_norm"][l], (partials["w_xk"][l],) = _rms_bwd(
            sv["x1"], dh, xattn_norm[l], dx, "rms_bwd_xattn", carry=scatter(big, ("w_xk",)))
        d_y, (partials["w_xv"][l],) = _mm(dxb, wl["w_mix_out"], tb=True, out_dtype=BF16, name="mm_d_y",
                                          carry=scatter(big, ("w_xv",)))
        big["w_mix_out"] = _mm(sv["y"], dxb, ta=True, out_dtype=BF16, name="mm_dw_mix_out")
        d_o, d_proj, small["gdn_out_norm"][l], small["sc_conv"][l], (partials["w_mix_out"][l],) = _mix_post_bwd(
            d_y, sv["o_raw"], sv["proj"], gdn_out_norm[l], conv_full["sc_conv"][l], width,
            carry=scatter(big, ("w_mix_out",)))
        d_qkv, d_gb, _ = _gdn_bwd(sv["qkv"], sv["ba"], gp, d_o, sv["states"], sv["tinv"], width)
        d_ba, d_gp = _gates_bwd(sv["ba"], d_gb, gp, nh)
        small["gdn_a_log"][l] = d_gp[0, nh:2 * nh]
        small["gdn_dt_bias"][l] = d_gp[1, nh:2 * nh]
        d_proj, small["gdn_conv"][l] = _gdn_prep_bwd(d_qkv, sv["proj"], conv_full["gdn_conv"][l], d_proj, width)
        dw_main = _mm(sv["h1"], d_proj, ta=True, name="mm_dw_mix_in")
        dw_gate = _mm(sv["h1"], d_ba, ta=True, name="mm_dw_mix_gates")
        big["w_mix_in"] = _mix_in_shards(dw_main, dw_gate, width, nh)
        dh, (partials["w_mix_in"][l],) = _mm(d_proj, w_main, tb=True, out_dtype=F32, name="mm_dh1_main",
                                             carry=scatter(big, ("w_mix_in",)))
        dh = _mm(d_ba, w_gate, tb=True, out_dtype=BF16, res=dh, name="mm_dh1_gates")
        dx, dxb, small["mix_norm"][l], _ = _rms_bwd(sv["x0"], dh, mix_norm[l], dx, "rms_bwd_mix")

    small_names = ("mix_norm", "xattn_norm", "mem_norm", "ffn_norm", "gdn_a_log", "gdn_dt_bias", "gdn_out_norm",
                   "gdn_conv", "sc_conv", "ffn_conv")
    small_parts = [jnp.stack(small[n]).reshape(1, -1) for n in small_names]
    small_parts += [g_final.reshape(1, -1), loss_part.reshape(1, 1)]
    small_sizes = [p.shape[1] for p in small_parts]
    small_local = _pack(small_parts, (1,))
    small_sum = _slot_sum(_all_gather([small_local], "all_gather_small")[0][0], "sum_small")
    small_tot = _unpack(small_sum, small_sizes)
    grads = {}
    for n, g in zip(small_names, small_tot[:len(small_names)]):
        if n in conv_names:
            _, kt, cs = wts[n].shape
            g = lax.dynamic_slice_in_dim(g.reshape(depth, kt, N_DEV * cs), me * cs, cs, axis=2)
        grads[n] = g.reshape(wts[n].shape)
    grads["final_norm"] = small_tot[-2].reshape(final_norm.shape)
    loss = small_tot[-1].reshape(())

    delta, new_m, new_v = {}, {}, {}
    for n in names:
        if n in BIG:
            grads[n], delta[n], new_m[n], new_v[n] = _adamw_sharded(wts[n], mom1[n], mom2[n], partials[n], "adamw_" + n)
        else:
            delta[n], new_m[n], new_v[n] = _adamw(wts[n], grads[n], mom1[n], mom2[n], "adamw_" + n)
    return (loss, dx[None], *[grads[n] for n in names], *[delta[n] for n in names],
            *[new_m[n] for n in names], *[new_v[n] for n in names])
```

```python
import functools

import jax
import jax.numpy as jnp
from jax import lax
from jax.experimental import pallas as pl
from jax.experimental.pallas import tpu as pltpu

F32 = jnp.float32
BF16 = jnp.bfloat16
CHUNK = 64
HEAD = 128
XHEADS = 4
GDN_K = 4
EPS = 1e-6
HALO = 16
QKV_COL = 4
N_DEV = 8
VMEM_LIMIT = 56 * 1024 * 1024
ADAM_LR, ADAM_B1, ADAM_B2, ADAM_EPS, ADAM_WD, ADAM_STEP = 0.001, 0.9, 0.999, 1e-08, 0.01, 10
MESH = pl.DeviceIdType.MESH
MM_TILES_MN = (1408, 1024, 512, 256, 128)
MM_TILES_K = (2048, 1408, 1024, 512, 256, 128)


def _call(body, **kw):
    return pl.pallas_call(body, **kw)


def _pc(body, *, carry=None, **kw):
    if carry is None:
        return _call(body, **kw)
    c_args, c_in, c_shapes, c_out, c_sems = _carry_parts(carry)
    nca = len(c_args)
    grid = kw["grid"]
    single = not isinstance(kw["out_shape"], (list, tuple))
    out_shape = [kw["out_shape"]] if single else list(kw["out_shape"])
    out_specs = [kw["out_specs"]] if single else list(kw["out_specs"])
    n_in, n_out = len(kw["in_specs"]), len(out_shape)

    def wrapped(*refs):
        ins, srcs = refs[:n_in], refs[n_in:n_in + nca]
        outs = refs[n_in + nca:n_in + nca + n_out]
        dsts = refs[n_in + nca + n_out:n_in + 2 * nca + n_out]
        scratch = refs[n_in + 2 * nca + n_out:]
        ids = [pl.program_id(ax) for ax in range(len(grid))]
        first = functools.reduce(lambda p, q: p & q, [i == 0 for i in ids])
        last = functools.reduce(lambda p, q: p & q, [i == g - 1 for i, g in zip(ids, grid)])
        _carry_run(carry, srcs, dsts, scratch[-2], scratch[-1], first, last)
        body(*ins, *outs, *scratch[:-2])

    kw = dict(kw, in_specs=list(kw["in_specs"]) + c_in, out_specs=out_specs + c_out, out_shape=out_shape + c_shapes,
              scratch_shapes=list(kw.get("scratch_shapes", [])) + c_sems,
              compiler_params=_cp(("arbitrary",) * len(grid)))
    call = _call(wrapped, **kw)

    def run(*args):
        res = call(*args, *c_args)
        return (res[0] if single else tuple(res[:n_out])), list(res[n_out:])

    return run


def _cp(sem=None, **kw):
    if sem is not None:
        kw["dimension_semantics"] = sem
    return pltpu.CompilerParams(vmem_limit_bytes=VMEM_LIMIT, **kw)


def _tile(n, cands):
    for c in cands:
        if n % c == 0:
            return c
    return n


def _sigmoid(x):
    return 0.5 * jnp.tanh(0.5 * x) + 0.5

def _softplus(x):
    return jnp.maximum(x, 0.0) + jnp.log(1.0 + jnp.exp(-jnp.abs(x)))


def _mmb(a, b):
    return jnp.dot(a.astype(BF16), b.astype(BF16), preferred_element_type=F32)


def _mmb_nt(a, b):
    return lax.dot_general(a.astype(BF16), b.astype(BF16), (((1,), (1,)), ((), ())), preferred_element_type=F32)


def _mmb_tn(a, b):
    return lax.dot_general(a.astype(BF16), b.astype(BF16), (((0,), (0,)), ((), ())), preferred_element_type=F32)


def _split(x):
    hi = x.astype(BF16)
    return hi, x - hi.astype(F32)


def _mm3(a, b):
    ah, ar = _split(a)
    bh, br = _split(b)
    al, bl = ar.astype(BF16), br.astype(BF16)
    return (jnp.dot(ah, bh, preferred_element_type=F32)
            + (jnp.dot(ah, bl, preferred_element_type=F32) + jnp.dot(al, bh, preferred_element_type=F32)))


def _mm_exact_lhs(a, b):
    ab = a.astype(BF16)
    b1, r1 = _split(b)
    b2, r2 = _split(r1)
    return (jnp.dot(ab, b1, preferred_element_type=F32)
            + (jnp.dot(ab, b2, preferred_element_type=F32) + jnp.dot(ab, r2.astype(BF16), preferred_element_type=F32)))


def _shift_down(cur, prev, s):
    if s == 0:
        return cur
    r = pltpu.roll(cur, s, 0)
    p = pltpu.roll(prev, s, 0)
    rows = lax.broadcasted_iota(jnp.int32, prev.shape, 0)
    first = jnp.where(rows < s, p, r[:HALO])
    return jnp.concatenate([first, r[HALO:]], axis=0)


def _shift_up(ext, s, tm):
    if s == 0:
        return ext[:tm]
    return pltpu.roll(ext, ext.shape[0] - s, 0)[:tm]


def _prev_map(tm, col):
    return lambda i, j: (jnp.maximum(i * (tm // HALO) - 1, 0), col(j))


def _next_map(tm, nrows, col):
    return lambda i, j: (jnp.minimum((i + 1) * (tm // HALO), nrows // HALO - 1), col(j))


def _carry_parts(carry):
    if carry is None:
        return [], [], [], [], []
    _, srcs, _ = carry
    na = len(srcs)
    anyspec = pl.BlockSpec(memory_space=pl.ANY)
    return (list(srcs), [anyspec] * na, [jax.ShapeDtypeStruct((N_DEV,) + s.shape[1:], s.dtype) for s in srcs],
            [anyspec] * na, [pltpu.SemaphoreType.DMA((na, N_DEV)), pltpu.SemaphoreType.DMA((na, N_DEV))])


def _carry_run(carry, src_refs, dst_refs, send_sems, recv_sems, first, last):
    if carry is None:
        return
    kind, _, layer = carry
    mx, my, mc = lax.axis_index("x"), lax.axis_index("y"), lax.axis_index("c")
    me = 4 * mx + 2 * my + mc

    def descriptors(with_recvs):
        sends, recvs = [], []
        for a in range(len(dst_refs)):
            for k in range(N_DEV):
                px, py, pc = mx ^ (k >> 2), my ^ ((k >> 1) & 1), mc ^ (k & 1)
                peer = 4 * px + 2 * py + pc
                src = src_refs[a].at[peer] if kind == "scatter" else src_refs[a].at[layer]
                if k == 0:
                    sends.append(pltpu.make_async_copy(src, dst_refs[a].at[me], send_sems.at[a, 0]))
                    continue
                sends.append(pltpu.make_async_remote_copy(
                    src_ref=src, dst_ref=dst_refs[a].at[me], send_sem=send_sems.at[a, k], recv_sem=recv_sems.at[a, k],
                    device_id=(px, py, pc), device_id_type=MESH))
                if with_recvs:
                    recvs.append(pltpu.make_async_remote_copy(
                        src_ref=src, dst_ref=dst_refs[a].at[peer], send_sem=send_sems.at[a, k],
                        recv_sem=recv_sems.at[a, k], device_id=(mx, my, mc), device_id_type=MESH))
        return sends, recvs

    @pl.when(first)
    def _():
        for cp in descriptors(False)[0]:
            cp.start()

    @pl.when(last)
    def _():
        sends, recvs = descriptors(True)
        for cp in recvs:
            cp.wait_recv()
        for i, cp in enumerate(sends):
            if i % N_DEV == 0:
                cp.wait()
            else:
                cp.wait_send()


def _mm(a, b, *, ta=False, tb=False, out_dtype=F32, res=None, name, a_halves=False, b_shards=False,
        b_halves=False, out_shards=False, carry=None):
    if a_halves:
        m, k = a.shape[1], 2 * a.shape[2]
    else:
        m, k = (a.shape[1], a.shape[0]) if ta else a.shape
    if b_shards:
        cs = b.shape[2]
        n = b.shape[1] if tb else N_DEV * cs
    elif b_halves:
        n = 2 * b.shape[2]
        cs = n // N_DEV
    else:
        n = b.shape[0] if tb else b.shape[1]
        cs = None
    tm = _tile(m, MM_TILES_MN)
    tn = cs if (cs is not None and not tb) else _tile(n, MM_TILES_MN)
    tk = cs if (b_shards and tb) else _tile(k, MM_TILES_K)
    nk = k // tk
    dn = (((0 if ta else 1,), (1 if tb else 0,)), ((), ()))

    def body(a_ref, b_ref, *rest):
        if res is None:
            o_ref, acc = rest
        else:
            r_ref, o_ref, acc = rest
        kk = pl.program_id(2)
        part = lax.dot_general(a_ref[...].astype(BF16), b_ref[...].astype(BF16), dn, preferred_element_type=F32)

        def finish(total):
            if res is not None:
                total = total + r_ref[...].astype(F32)
            o_ref[...] = total.astype(out_dtype)

        if nk == 1:
            finish(part)
            return

        @pl.when(kk == 0)
        def _():
            acc[...] = part

        @pl.when((kk > 0) & (kk < nk - 1))
        def _():
            acc[...] += part

        @pl.when(kk == nk - 1)
        def _():
            finish(acc[...] + part)

    if a_halves:
        per = (k // 2) // tk
        a_spec = pl.BlockSpec((None, tm, tk), lambda i, j, kk: (kk // per, i, kk % per))
    elif ta:
        a_spec = pl.BlockSpec((tk, tm), lambda i, j, kk: (kk, i))
    else:
        a_spec = pl.BlockSpec((tm, tk), lambda i, j, kk: (i, kk))
    if b_shards and tb:
        b_spec = pl.BlockSpec((None, tn, tk), lambda i, j, kk: (kk, j, 0))
    elif b_shards:
        b_spec = pl.BlockSpec((None, tk, tn), lambda i, j, kk: (j, kk, 0))
    elif b_halves:
        perb = (n // 2) // tn
        b_spec = pl.BlockSpec((None, tk, tn), lambda i, j, kk: (j // perb, kk, j % perb))
    elif tb:
        b_spec = pl.BlockSpec((tn, tk), lambda i, j, kk: (j, kk))
    else:
        b_spec = pl.BlockSpec((tk, tn), lambda i, j, kk: (kk, j))
    if out_shards:
        o_spec = pl.BlockSpec((None, tm, tn), lambda i, j, kk: (j, i, 0))
        o_shape = jax.ShapeDtypeStruct((N_DEV, m, tn), out_dtype)
    else:
        o_spec = pl.BlockSpec((tm, tn), lambda i, j, kk: (i, j))
        o_shape = jax.ShapeDtypeStruct((m, n), out_dtype)
    in_specs = [a_spec, b_spec] + ([o_spec] if res is not None else [])
    args = (a, b) + ((res,) if res is not None else ())
    return _pc(body, carry=carry, name=name, grid=(m // tm, n // tn, nk), in_specs=in_specs, out_specs=o_spec,
               out_shape=o_shape, scratch_shapes=[pltpu.VMEM((tm, tn), F32)],
               compiler_params=_cp(("parallel", "parallel", "arbitrary")))(*args)


def _rms_fwd(x, w, name):
    s, d = x.shape
    tm = _tile(s, (512, 256, 128, 64))

    def body(x_ref, w_ref, o_ref):
        xv = x_ref[...]
        r = lax.rsqrt(jnp.mean(xv * xv, axis=-1, keepdims=True) + EPS)
        o_ref[...] = (xv * r * w_ref[...]).astype(BF16)

    return _pc(body, name=name, grid=(s // tm,),
               in_specs=[pl.BlockSpec((tm, d), lambda i: (i, 0)), pl.BlockSpec((1, d), lambda i: (0, 0))],
               out_specs=pl.BlockSpec((tm, d), lambda i: (i, 0)), out_shape=jax.ShapeDtypeStruct((s, d), BF16),
               compiler_params=_cp(("parallel",)))(x, w.reshape(1, d))


def _rms_bwd(x, dh, w, dx_in, name, carry=None):
    s, d = x.shape
    tm = _tile(s, (256, 128, 64))

    def body(x_ref, dh_ref, w_ref, dxi_ref, dx_ref, dxb_ref, dg_ref):
        @pl.when(pl.program_id(0) == 0)
        def _():
            dg_ref[...] = jnp.zeros_like(dg_ref)

        xv = x_ref[...]
        dy = dh_ref[...].astype(F32)
        r = lax.rsqrt(jnp.mean(xv * xv, axis=-1, keepdims=True) + EPS)
        xh = xv * r
        dxh = dy * w_ref[...]
        dx = dxi_ref[...] + r * (dxh - xh * jnp.mean(dxh * xh, axis=-1, keepdims=True))
        dx_ref[...] = dx
        dxb_ref[...] = dx.astype(BF16)
        dg_ref[0:1, :] += jnp.sum(dy * xh, axis=0, keepdims=True)

    row = pl.BlockSpec((tm, d), lambda i: (i, 0))
    outs = _pc(body, carry=carry, name=name, grid=(s // tm,),
               in_specs=[row, row, pl.BlockSpec((1, d), lambda i: (0, 0)), row],
               out_specs=[row, row, pl.BlockSpec((8, d), lambda i: (0, 0))],
               out_shape=[jax.ShapeDtypeStruct((s, d), F32), jax.ShapeDtypeStruct((s, d), BF16),
                          jax.ShapeDtypeStruct((8, d), F32)],
               compiler_params=_cp(("arbitrary",)))(x, dh, w.reshape(1, d), dx_in)
    (dx, dxb, dg), got = outs if carry is not None else (outs, [])
    return dx, dxb, dg[0], got


def _loss_head(x, w, target):
    s, d = x.shape
    tm = _tile(s, (256, 128, 64))

    def body(x_ref, w_ref, t_ref, dx_ref, dxb_ref, dg_ref, l_ref):
        @pl.when(pl.program_id(0) == 0)
        def _():
            dg_ref[...] = jnp.zeros_like(dg_ref)
            l_ref[...] = jnp.zeros_like(l_ref)

        xv = x_ref[...]
        r = lax.rsqrt(jnp.mean(xv * xv, axis=-1, keepdims=True) + EPS)
        xh = xv * r
        err = xh * w_ref[...] - t_ref[...]
        l_ref[...] += 0.5 * jnp.sum(jnp.mean(err * err, axis=-1, keepdims=True), axis=0, keepdims=True)
        dy = err * (1.0 / d)
        dxh = dy * w_ref[...]
        dx = r * (dxh - xh * jnp.mean(dxh * xh, axis=-1, keepdims=True))
        dx_ref[...] = dx
        dxb_ref[...] = dx.astype(BF16)
        dg_ref[0:1, :] += jnp.sum(dy * xh, axis=0, keepdims=True)

    row = pl.BlockSpec((tm, d), lambda i: (i, 0))
    dx, dxb, dg, ls = _pc(body, name="loss_head", grid=(s // tm,),
                          in_specs=[row, pl.BlockSpec((1, d), lambda i: (0, 0)), row],
                          out_specs=[row, row, pl.BlockSpec((8, d), lambda i: (0, 0)),
                                     pl.BlockSpec((8, 128), lambda i: (0, 0))],
                          out_shape=[jax.ShapeDtypeStruct((s, d), F32), jax.ShapeDtypeStruct((s, d), BF16),
                                     jax.ShapeDtypeStruct((8, d), F32), jax.ShapeDtypeStruct((8, 128), F32)],
                          compiler_params=_cp(("arbitrary",)))(x, w.reshape(1, d), target)
    return ls[0, 0], dx, dxb, dg[0]


def _gdn_prep_fwd(proj, conv_w, width):
    s = proj.shape[0]
    tm = _tile(s, (256, 128, 64))
    nh = width // HEAD

    def body(c_ref, p_ref, w_ref, o_ref):
        i, seg = pl.program_id(0), pl.program_id(1)
        cur = c_ref[...].astype(F32)
        prev = jnp.where(i > 0, p_ref[...].astype(F32), 0.0)
        pre = cur * w_ref[GDN_K - 1:GDN_K, :]
        for j in range(GDN_K - 1):
            pre = pre + _shift_down(cur, prev, GDN_K - 1 - j) * w_ref[j:j + 1, :]
        act = pre * _sigmoid(pre)
        scale = jnp.where(seg == 0, HEAD ** -0.5, 1.0)
        for h in range(nh):
            a = act[:, h * HEAD:(h + 1) * HEAD]
            rs = lax.rsqrt(jnp.sum(a * a, axis=-1, keepdims=True) + EPS) * scale
            o_ref[:, h * HEAD:(h + 1) * HEAD] = a * jnp.where(seg < 2, rs, 1.0)

    return _pc(body, name="gdn_prep_fwd", grid=(s // tm, 3),
               in_specs=[pl.BlockSpec((tm, width), lambda i, j: (i, j + QKV_COL)),
                         pl.BlockSpec((HALO, width), _prev_map(tm, lambda j: j + QKV_COL)),
                         pl.BlockSpec((GDN_K, width), lambda i, j: (0, j))],
               out_specs=pl.BlockSpec((tm, width), lambda i, j: (i, j)),
               out_shape=jax.ShapeDtypeStruct((s, 3 * width), F32),
               compiler_params=_cp(("parallel", "parallel")))(proj, proj, conv_w)


def _gdn_prep_bwd(dqkv, proj, conv_w, dproj_in, width):
    s = proj.shape[0]
    tm = _tile(s, (256, 128, 64))
    nh = width // HEAD
    nt = s // tm

    def body(c_ref, p_ref, n_ref, d_ref, dn_ref, w_ref, _, o_ref, dw_ref):
        seg, i = pl.program_id(0), pl.program_id(1)

        @pl.when(i == 0)
        def _():
            dw_ref[...] = jnp.zeros_like(dw_ref)

        ext = jnp.concatenate([c_ref[...].astype(F32), n_ref[...].astype(F32)], axis=0)
        prev = jnp.where(i > 0, p_ref[...].astype(F32), 0.0)
        sh = [_shift_down(ext, prev, GDN_K - 1 - j) for j in range(GDN_K)]
        pre = sh[0] * w_ref[0:1, :]
        for j in range(1, GDN_K):
            pre = pre + sh[j] * w_ref[j:j + 1, :]
        sg = _sigmoid(pre)
        act = pre * sg
        dout = jnp.concatenate([d_ref[...], dn_ref[...]], axis=0)
        rows = lax.broadcasted_iota(jnp.int32, (tm + HALO, 1), 0)
        dout = jnp.where((rows < tm) | (i < nt - 1), dout, 0.0)
        scale = jnp.where(seg == 0, HEAD ** -0.5, 1.0)
        parts = []
        for h in range(nh):
            a = act[:, h * HEAD:(h + 1) * HEAD]
            dq = dout[:, h * HEAD:(h + 1) * HEAD]
            rs = lax.rsqrt(jnp.sum(a * a, axis=-1, keepdims=True) + EPS)
            nrm = a * rs
            dn = dq * scale
            da_norm = rs * (dn - nrm * jnp.sum(dn * nrm, axis=-1, keepdims=True))
            parts.append(jnp.where(seg < 2, da_norm, dq))
        dact = jnp.concatenate(parts, axis=1)
        dpre = dact * (sg * (1.0 + pre * (1.0 - sg)))
        dp = _shift_up(dpre, 0, tm) * w_ref[GDN_K - 1:GDN_K, :]
        for j in range(GDN_K - 1):
            dp = dp + _shift_up(dpre, GDN_K - 1 - j, tm) * w_ref[j:j + 1, :]
        o_ref[...] = dp.astype(BF16)
        for j in range(GDN_K):
            dw_ref[j:j + 1, :] += jnp.sum(dpre[:tm] * sh[j][:tm], axis=0, keepdims=True)

    dproj, dw = _pc(body, name="gdn_prep_bwd", grid=(3, nt),
                    in_specs=[pl.BlockSpec((tm, width), lambda j, i: (i, j + QKV_COL)),
                              pl.BlockSpec((HALO, width), lambda j, i: _prev_map(tm, lambda c: c + QKV_COL)(i, j)),
                              pl.BlockSpec((HALO, width), lambda j, i: _next_map(tm, s, lambda c: c + QKV_COL)(i, j)),
                              pl.BlockSpec((tm, width), lambda j, i: (i, j)),
                              pl.BlockSpec((HALO, width), lambda j, i: _next_map(tm, s, lambda c: c)(i, j)),
                              pl.BlockSpec((GDN_K, width), lambda j, i: (0, j)),
                              pl.BlockSpec(memory_space=pl.ANY)],
                    out_specs=[pl.BlockSpec((tm, width), lambda j, i: (i, j + QKV_COL)),
                               pl.BlockSpec((8, width), lambda j, i: (0, j))],
                    out_shape=[jax.ShapeDtypeStruct(dproj_in.shape, BF16), jax.ShapeDtypeStruct((8, 3 * width), F32)],
                    input_output_aliases={6: 0},
                    compiler_params=_cp(("parallel", "arbitrary")))(proj, proj, proj, dqkv, dqkv, conv_w, dproj_in)
    return dproj, dw[:GDN_K]


def _chunk_common(bav, gp_ref, nh):
    g_full = -jnp.exp(gp_ref[0:1, :]) * _softplus(bav + gp_ref[1:2, :])
    beta_full = _sigmoid(bav)
    ri = lax.broadcasted_iota(jnp.int32, (CHUNK, CHUNK), 0)
    ci = lax.broadcasted_iota(jnp.int32, (CHUNK, CHUNK), 1)
    gc_full = _mm_exact_lhs(ri >= ci, g_full)
    gc_t = gc_full.T
    return beta_full, gc_full, gc_t, ri, ci


def _head_gates(h, nh, beta_full, gc_full, gc_t, ri, ci):
    bcol = beta_full[:, h:h + 1]
    gcol = gc_full[:, nh + h:nh + h + 1]
    grow = gc_t[nh + h:nh + h + 1, :]
    dec = jnp.exp(jnp.where(ri >= ci, gcol - grow, -1e30))
    ecol = jnp.exp(gcol)
    gl = gcol[CHUNK - 1:CHUNK, :]
    return bcol, gcol, dec, ecol, gl


def _gdn_fwd(qkv, ba, gp, width, carry=None):
    s = qkv.shape[0]
    nh = width // HEAD
    nc = s // CHUNK
    heads = range(nh)

    def body(q_ref, k_ref, v_ref, ba_ref, gp_ref, o_ref, st_ref, t_ref, state):
        @pl.when(pl.program_id(0) == 0)
        def _():
            state[...] = jnp.zeros_like(state)

        beta_full, gc_full, gc_t, ri, ci = _chunk_common(ba_ref[...], gp_ref, nh)
        eye = (ri == ci).astype(F32)
        sls = [slice(h * HEAD, (h + 1) * HEAD) for h in heads]
        q = [q_ref[:, sl] for sl in sls]
        k = [k_ref[:, sl] for sl in sls]
        v = [v_ref[:, sl] for sl in sls]
        st = [state[h] for h in heads]
        gates = [_head_gates(h, nh, beta_full, gc_full, gc_t, ri, ci) for h in heads]
        bcol, gcol, dec, ecol, gl = (list(z) for z in zip(*gates))
        kb = [k[h] * bcol[h] for h in heads]
        a = [jnp.where(ri > ci, _mmb_nt(kb[h], k[h]) * dec[h], 0.0) for h in heads]
        attn = [jnp.where(ri >= ci, _mmb_nt(q[h], k[h]) * dec[h], 0.0) for h in heads]
        t = [eye - a[h] for h in heads]
        pw = [_mm3(a[h], a[h]) for h in heads]
        for _ in range(4):
            both = [_mm3(jnp.concatenate([t[h], pw[h]], axis=0), pw[h]) for h in heads]
            t = [t[h] + both[h][:CHUNK] for h in heads]
            pw = [both[h][CHUNK:] for h in heads]
        t = [t[h] + _mm3(t[h], pw[h]) for h in heads]
        uw = [_mmb(t[h], jnp.concatenate([v[h] * bcol[h], kb[h] * ecol[h]], axis=1)) for h in heads]
        vn = [uw[h][:, :HEAD] - _mmb(uw[h][:, HEAD:], st[h]) for h in heads]
        out = [_mmb(q[h] * ecol[h], st[h]) + _mmb(attn[h], vn[h]) for h in heads]
        new = [st[h] * jnp.exp(gl[h]) + _mmb_tn(k[h] * jnp.exp(gl[h] - gcol[h]), vn[h]) for h in heads]
        for h in heads:
            o_ref[:, sls[h]] = out[h]
            st_ref[0, h] = st[h]
            t_ref[0, h] = t[h]
            state[h] = new[h]

    blk = lambda c: pl.BlockSpec((CHUNK, width), lambda n, c=c: (n, c))
    outs = _pc(body, carry=carry, name="gdn_fwd", grid=(nc,),
               in_specs=[blk(0), blk(1), blk(2), pl.BlockSpec((CHUNK, 128), lambda n: (n, 0)),
                         pl.BlockSpec((8, 128), lambda n: (0, 0))],
               out_specs=[blk(0), pl.BlockSpec((1, nh, HEAD, HEAD), lambda n: (n, 0, 0, 0)),
                          pl.BlockSpec((1, nh, CHUNK, CHUNK), lambda n: (n, 0, 0, 0))],
               out_shape=[jax.ShapeDtypeStruct((s, width), F32), jax.ShapeDtypeStruct((nc, nh, HEAD, HEAD), F32),
                          jax.ShapeDtypeStruct((nc, nh, CHUNK, CHUNK), F32)],
               scratch_shapes=[pltpu.VMEM((nh, HEAD, HEAD), F32)],
               compiler_params=_cp(("arbitrary",)))(qkv, qkv, qkv, ba, gp)
    (o, st, t), got = outs if carry is not None else (outs, [])
    return o, st, t, got


def _gdn_bwd(qkv, ba, gp, do, states, tinv, width, carry=None):
    s = qkv.shape[0]
    nh = width // HEAD
    nc = s // CHUNK
    heads = range(nh)

    def body(q_ref, k_ref, v_ref, ba_ref, gp_ref, do_ref, st_ref, t_ref, dqkv_ref, dgb_ref, dstate):
        @pl.when(pl.program_id(0) == 0)
        def _():
            dstate[...] = jnp.zeros_like(dstate)

        beta_full, gc_full, gc_t, ri, ci = _chunk_common(ba_ref[...], gp_ref, nh)
        lane = lax.broadcasted_iota(jnp.int32, (CHUNK, 128), 1)
        rowi = lax.broadcasted_iota(jnp.int32, (CHUNK, 1), 0)
        low, strict = ri >= ci, ri > ci
        each = lambda fn: [fn(h) for h in heads]
        rowsum = lambda x: jnp.sum(x, axis=1, keepdims=True)
        sls = each(lambda h: slice(h * HEAD, (h + 1) * HEAD))
        q, k, v = each(lambda h: q_ref[:, sls[h]]), each(lambda h: k_ref[:, sls[h]]), each(lambda h: v_ref[:, sls[h]])
        dout = each(lambda h: do_ref[:, sls[h]])
        st, t, dsp = each(lambda h: st_ref[0, h]), each(lambda h: t_ref[0, h]), each(lambda h: dstate[h])
        gates = each(lambda h: _head_gates(h, nh, beta_full, gc_full, gc_t, ri, ci))
        bcol, gcol, dec, ecol, gl = (list(z) for z in zip(*gates))
        el = each(lambda h: jnp.exp(gl[h]))
        kdsc = each(lambda h: jnp.exp(gl[h] - gcol[h]))
        kb = each(lambda h: k[h] * bcol[h])
        a = each(lambda h: jnp.where(strict, _mmb_nt(kb[h], k[h]) * dec[h], 0.0))
        attn = each(lambda h: jnp.where(low, _mmb_nt(q[h], k[h]) * dec[h], 0.0))
        uw = each(lambda h: _mmb(t[h], jnp.concatenate([v[h] * bcol[h], kb[h] * ecol[h]], axis=1)))
        w = each(lambda h: uw[h][:, HEAD:])
        kd = each(lambda h: k[h] * kdsc[h])
        vn = each(lambda h: uw[h][:, :HEAD] - _mmb(w[h], st[h]))
        d_attn = each(lambda h: jnp.where(low, _mmb_nt(dout[h], vn[h]), 0.0))
        d_vn = each(lambda h: _mmb_tn(attn[h], dout[h]) + _mmb(kd[h], dsp[h]))
        d_qd = each(lambda h: _mmb_nt(dout[h], st[h]))
        d_kd = each(lambda h: _mmb_nt(vn[h], dsp[h]))
        d_el = each(lambda h: jnp.sum(rowsum(st[h] * dsp[h]), axis=0, keepdims=True))
        dst_new = each(lambda h: _mmb_tn(q[h] * ecol[h], dout[h]) + el[h] * dsp[h] - _mmb_tn(w[h], d_vn[h]))
        d_w = each(lambda h: -_mmb_nt(d_vn[h], st[h]))
        dr = each(lambda h: _mmb_tn(t[h], jnp.concatenate([d_vn[h], d_w[h]], axis=1)))
        dru, drw = each(lambda h: dr[h][:, :HEAD]), each(lambda h: dr[h][:, HEAD:])
        d_a = each(lambda h: -jnp.where(strict, _mmb_nt(dr[h], uw[h]), 0.0))
        d_kk = each(lambda h: d_a[h] * dec[h])
        d_qk = each(lambda h: d_attn[h] * dec[h])
        d_kb = each(lambda h: _mmb(d_kk[h], k[h]) + drw[h] * ecol[h])
        dk = each(lambda h: _mmb_tn(d_kk[h], kb[h]) + _mmb_tn(d_qk[h], q[h]) + d_kb[h] * bcol[h] + d_kd[h] * kdsc[h])
        dq = each(lambda h: _mmb(d_qk[h], k[h]) + d_qd[h] * ecol[h])
        dbeta = each(lambda h: rowsum(dru[h] * v[h] + d_kb[h] * k[h]))
        de = each(lambda h: rowsum(drw[h] * kb[h] + d_qd[h] * q[h]))
        r = each(lambda h: rowsum(d_kd[h] * k[h]) * kdsc[h])
        mm = each(lambda h: d_a[h] * a[h] + d_attn[h] * attn[h])
        d_gl = each(lambda h: jnp.sum(r[h], axis=0, keepdims=True) + d_el[h] * el[h])
        d_gc = each(lambda h: de[h] * ecol[h] - r[h] + rowsum(mm[h]) - rowsum(mm[h].T)
                    + jnp.where(rowi == CHUNK - 1, d_gl[h], 0.0))
        dbeta_full = jnp.zeros((CHUNK, 128), F32)
        dgc_full = jnp.zeros((CHUNK, 128), F32)
        for h in heads:
            dqkv_ref[:, sls[h]] = dq[h]
            dqkv_ref[:, width + h * HEAD:width + (h + 1) * HEAD] = dk[h]
            dqkv_ref[:, 2 * width + h * HEAD:2 * width + (h + 1) * HEAD] = dru[h] * bcol[h]
            dstate[h] = dst_new[h]
            dbeta_full = dbeta_full + jnp.where(lane == h, dbeta[h], 0.0)
            dgc_full = dgc_full + jnp.where(lane == nh + h, d_gc[h], 0.0)
        dgb_ref[...] = dbeta_full + _mm_exact_lhs(ri <= ci, dgc_full)

    rev = lambda c: pl.BlockSpec((CHUNK, width), lambda n, c=c: (nc - 1 - n, c))
    outs = _pc(body, carry=carry, name="gdn_bwd", grid=(nc,),
               in_specs=[rev(0), rev(1), rev(2), pl.BlockSpec((CHUNK, 128), lambda n: (nc - 1 - n, 0)),
                         pl.BlockSpec((8, 128), lambda n: (0, 0)), rev(0),
                         pl.BlockSpec((1, nh, HEAD, HEAD), lambda n: (nc - 1 - n, 0, 0, 0)),
                         pl.BlockSpec((1, nh, CHUNK, CHUNK), lambda n: (nc - 1 - n, 0, 0, 0))],
               out_specs=[pl.BlockSpec((CHUNK, 3 * width), lambda n: (nc - 1 - n, 0)),
                          pl.BlockSpec((CHUNK, 128), lambda n: (nc - 1 - n, 0))],
               out_shape=[jax.ShapeDtypeStruct((s, 3 * width), F32), jax.ShapeDtypeStruct((s, 128), F32)],
               scratch_shapes=[pltpu.VMEM((nh, HEAD, HEAD), F32)],
               compiler_params=_cp(("arbitrary",)))(qkv, qkv, qkv, ba, gp, do, states, tinv)
    (dqkv, dgb), got = outs if carry is not None else (outs, [])
    return dqkv, dgb, got


def _gates_bwd(ba, dgb, gp, nh):
    s = ba.shape[0]
    tm = _tile(s, (512, 256, 128, 64))

    def body(ba_ref, d_ref, gp_ref, o_ref, dp_ref):
        @pl.when(pl.program_id(0) == 0)
        def _():
            dp_ref[...] = jnp.zeros_like(dp_ref)

        bav, dv = ba_ref[...], d_ref[...]
        lane = lax.broadcasted_iota(jnp.int32, bav.shape, 1)
        beta = _sigmoid(bav)
        amp = jnp.exp(gp_ref[0:1, :])
        z = bav + gp_ref[1:2, :]
        d_a = dv * (-amp) * _sigmoid(z)
        d_b = dv * beta * (1.0 - beta)
        is_a = (lane >= nh) & (lane < 2 * nh)
        o_ref[...] = jnp.where(lane < nh, d_b, jnp.where(is_a, d_a, 0.0))
        dp_ref[0:1, :] += jnp.sum(jnp.where(is_a, dv * (-amp) * _softplus(z), 0.0), axis=0, keepdims=True)
        dp_ref[1:2, :] += jnp.sum(jnp.where(is_a, d_a, 0.0), axis=0, keepdims=True)

    row = pl.BlockSpec((tm, 128), lambda i: (i, 0))
    par = pl.BlockSpec((8, 128), lambda i: (0, 0))
    return _pc(body, name="gates_bwd", grid=(s // tm,), in_specs=[row, row, par], out_specs=[row, par],
               out_shape=[jax.ShapeDtypeStruct((s, 128), F32), jax.ShapeDtypeStruct((8, 128), F32)],
               compiler_params=_cp(("arbitrary",)))(ba, dgb, gp)


def _mix_post_fwd(o_raw, proj, gain, sc_w, width):
    s = o_raw.shape[0]
    tm = _tile(s, (256, 128, 64))
    nh = width // HEAD
    ksc = sc_w.shape[0]

    def body(o_ref, z_ref, b_ref, c_ref, h_ref, cp_ref, hp_ref, g_ref, w_ref, y_ref):
        i = pl.program_id(0)
        z = z_ref[...].astype(F32)
        sz = z * _sigmoid(z)
        for h in range(nh):
            sl = slice(h * HEAD, (h + 1) * HEAD)
            o = o_ref[:, sl]
            r = lax.rsqrt(jnp.mean(o * o, axis=-1, keepdims=True) + EPS)
            y_ref[:, sl] = (o * r * g_ref[...] * sz[:, sl]).astype(BF16)
        prod = c_ref[...].astype(F32) * h_ref[...].astype(F32)
        pprev = jnp.where(i > 0, cp_ref[...].astype(F32) * hp_ref[...].astype(F32), 0.0)
        cv = prod * w_ref[ksc - 1:ksc, :]
        for j in range(ksc - 1):
            cv = cv + _shift_down(prod, pprev, ksc - 1 - j) * w_ref[j:j + 1, :]
        y_ref[:, width:] = (b_ref[...].astype(F32) * cv).astype(BF16)

    col = lambda c: pl.BlockSpec((tm, width), lambda i, c=c: (i, c))
    prv = lambda c: pl.BlockSpec((HALO, width), lambda i, c=c: (jnp.maximum(i * (tm // HALO) - 1, 0), c))
    return _pc(body, name="mix_post_fwd", grid=(s // tm,),
               in_specs=[col(0), col(0), col(1), col(2), col(3), prv(2), prv(3),
                         pl.BlockSpec((1, HEAD), lambda i: (0, 0)), pl.BlockSpec((ksc, width), lambda i: (0, 0))],
               out_specs=pl.BlockSpec((tm, 2 * width), lambda i: (i, 0)),
               out_shape=jax.ShapeDtypeStruct((s, 2 * width), BF16),
               compiler_params=_cp(("parallel",)))(o_raw, proj, proj, proj, proj, proj, proj,
                                                    gain.reshape(1, HEAD), sc_w)


def _mix_post_bwd(dy, o_raw, proj, gain, sc_w, width, carry=None):
    s = o_raw.shape[0]
    tm = _tile(s, (256, 128, 64))
    nt = s // tm
    nh = width // HEAD
    ksc = sc_w.shape[0]

    def body(dyg_ref, dys_ref, dysn_ref, o_ref, z_ref, b_ref, bn_ref, c_ref, h_ref, cp_ref, hp_ref, g_ref, w_ref,
             do_ref, dp_ref, dg_ref, dw_ref):
        i = pl.program_id(0)

        @pl.when(i == 0)
        def _():
            dg_ref[...] = jnp.zeros_like(dg_ref)
            dw_ref[...] = jnp.zeros_like(dw_ref)

        z = z_ref[...].astype(F32)
        sg = _sigmoid(z)
        sz = z * sg
        dsz = sg * (1.0 + z * (1.0 - sg))
        dyg = dyg_ref[...].astype(F32)
        dgain = jnp.zeros((1, HEAD), F32)
        for h in range(nh):
            sl = slice(h * HEAD, (h + 1) * HEAD)
            o = o_ref[:, sl]
            r = lax.rsqrt(jnp.mean(o * o, axis=-1, keepdims=True) + EPS)
            oh = o * r
            d_yn = dyg[:, sl] * sz[:, sl]
            dp_ref[:, sl] = (dyg[:, sl] * oh * g_ref[...] * dsz[:, sl]).astype(BF16)
            dgain = dgain + jnp.sum(d_yn * oh, axis=0, keepdims=True)
            doh = d_yn * g_ref[...]
            do_ref[:, sl] = r * (doh - oh * jnp.mean(doh * oh, axis=-1, keepdims=True))
        dg_ref[0:1, :] += dgain
        cc, hh = c_ref[...].astype(F32), h_ref[...].astype(F32)
        prod = cc * hh
        pprev = jnp.where(i > 0, cp_ref[...].astype(F32) * hp_ref[...].astype(F32), 0.0)
        sh = [_shift_down(prod, pprev, ksc - 1 - j) for j in range(ksc)]
        cv = sh[0] * w_ref[0:1, :]
        for j in range(1, ksc):
            cv = cv + sh[j] * w_ref[j:j + 1, :]
        dys = dys_ref[...].astype(F32)
        dp_ref[:, width:2 * width] = (dys * cv).astype(BF16)
        dcv_n = jnp.where(i < nt - 1, dysn_ref[...].astype(F32) * bn_ref[...].astype(F32), 0.0)
        dcv = jnp.concatenate([dys * b_ref[...].astype(F32), dcv_n], axis=0)
        dprod = dcv[:tm] * w_ref[ksc - 1:ksc, :]
        for j in range(ksc - 1):
            dprod = dprod + _shift_up(dcv, ksc - 1 - j, tm) * w_ref[j:j + 1, :]
        dp_ref[:, 2 * width:3 * width] = (dprod * hh).astype(BF16)
        dp_ref[:, 3 * width:] = (dprod * cc).astype(BF16)
        for j in range(ksc):
            dw_ref[j:j + 1, :] += jnp.sum(dcv[:tm] * sh[j], axis=0, keepdims=True)

    col = lambda c: pl.BlockSpec((tm, width), lambda i, c=c: (i, c))
    prv = lambda c: pl.BlockSpec((HALO, width), lambda i, c=c: (jnp.maximum(i * (tm // HALO) - 1, 0), c))
    nxt = lambda c: pl.BlockSpec((HALO, width), lambda i, c=c: (jnp.minimum((i + 1) * (tm // HALO), s // HALO - 1), c))
    outs = _pc(
        body, carry=carry, name="mix_post_bwd", grid=(nt,),
        in_specs=[col(0), col(1), nxt(1), col(0), col(0), col(1), nxt(1), col(2), col(3), prv(2), prv(3),
                  pl.BlockSpec((1, HEAD), lambda i: (0, 0)), pl.BlockSpec((ksc, width), lambda i: (0, 0))],
        out_specs=[col(0), pl.BlockSpec((tm, 4 * width), lambda i: (i, 0)),
                   pl.BlockSpec((8, HEAD), lambda i: (0, 0)), pl.BlockSpec((8, width), lambda i: (0, 0))],
        out_shape=[jax.ShapeDtypeStruct((s, width), F32), jax.ShapeDtypeStruct((s, 7 * width), BF16),
                   jax.ShapeDtypeStruct((8, HEAD), F32), jax.ShapeDtypeStruct((8, width), F32)],
        compiler_params=_cp(("arbitrary",)))(dy, dy, dy, o_raw, proj, proj, proj, proj, proj, proj, proj,
                                             gain.reshape(1, HEAD), sc_w)
    (do, dp, dg, dw), got = outs if carry is not None else (outs, [])
    return do, dp, dg[0], dw[:ksc], got


def _xattn_fwd(q, k, v):
    s, d = q.shape
    nm = k.shape[0]
    dh = d // XHEADS
    tm = _tile(s, (512, 256, 128, 64))

    def body(q_ref, k_ref, v_ref, o_ref):
        sc = _mmb_nt(q_ref[...], k_ref[...]) * (dh ** -0.5)
        p = jnp.exp(sc - jnp.max(sc, axis=-1, keepdims=True))
        p = p / jnp.sum(p, axis=-1, keepdims=True)
        o_ref[...] = _mmb(p, v_ref[...]).astype(BF16)

    return _pc(body, name="xattn_fwd", grid=(s // tm, XHEADS),
               in_specs=[pl.BlockSpec((tm, dh), lambda i, h: (i, h)), pl.BlockSpec((nm, dh), lambda i, h: (0, h)),
                         pl.BlockSpec((nm, dh), lambda i, h: (0, h))],
               out_specs=pl.BlockSpec((tm, dh), lambda i, h: (i, h)), out_shape=jax.ShapeDtypeStruct((s, d), BF16),
               compiler_params=_cp(("parallel", "parallel")))(q, k, v)


def _xattn_bwd(q, k, v, do, carry=None):
    s, d = q.shape
    nm = k.shape[0]
    dh = d // XHEADS
    tm = _tile(s, (512, 256, 128, 64))

    def body(q_ref, k_ref, v_ref, do_ref, dq_ref, dk_ref, dv_ref):
        @pl.when(pl.program_id(1) == 0)
        def _():
            dk_ref[...] = jnp.zeros_like(dk_ref)
            dv_ref[...] = jnp.zeros_like(dv_ref)

        scale = dh ** -0.5
        sc = _mmb_nt(q_ref[...], k_ref[...]) * scale
        p = jnp.exp(sc - jnp.max(sc, axis=-1, keepdims=True))
        p = p / jnp.sum(p, axis=-1, keepdims=True)
        dp = _mmb_nt(do_ref[...], v_ref[...])
        ds = p * (dp - jnp.sum(dp * p, axis=-1, keepdims=True)) * scale
        dq_ref[...] = _mmb(ds, k_ref[...]).astype(BF16)
        dk_ref[...] += _mmb_tn(ds, q_ref[...])
        dv_ref[...] += _mmb_tn(p, do_ref[...])

    rowb = pl.BlockSpec((tm, dh), lambda h, i: (i, h))
    memb = pl.BlockSpec((nm, dh), lambda h, i: (0, h))
    outs = _pc(body, carry=carry, name="xattn_bwd", grid=(XHEADS, s // tm), in_specs=[rowb, memb, memb, rowb],
               out_specs=[rowb, memb, memb],
               out_shape=[jax.ShapeDtypeStruct((s, d), BF16), jax.ShapeDtypeStruct((nm, d), F32),
                          jax.ShapeDtypeStruct((nm, d), F32)],
               compiler_params=_cp(("parallel", "arbitrary")))(q, k, v, do)
    (dq, dk, dv), got = outs if carry is not None else (outs, [])
    return dq, dk, dv, got


def _ffn_act_fwd(u_pre, conv_w):
    s, f2 = u_pre.shape
    f = f2 // 2
    tm = _tile(s, (256, 128, 64))
    cb = _tile(f, (512, 256, 128))
    nf = f // cb
    kf = conv_w.shape[0]

    def body(g_ref, u_ref, gp_ref, up_ref, wg_ref, wu_ref, a_ref):
        i = pl.program_id(0)

        def conv(c_ref, p_ref, w_ref):
            cur = c_ref[...].astype(F32)
            prev = jnp.where(i > 0, p_ref[...].astype(F32), 0.0)
            out = cur * w_ref[kf - 1:kf, :]
            for j in range(kf - 1):
                out = out + _shift_down(cur, prev, kf - 1 - j) * w_ref[j:j + 1, :]
            return out

        gate, up = conv(g_ref, gp_ref, wg_ref), conv(u_ref, up_ref, wu_ref)
        a_ref[...] = (gate * _sigmoid(gate) * up).astype(BF16)

    return _pc(body, name="ffn_act_fwd", grid=(s // tm, nf),
               in_specs=[pl.BlockSpec((tm, cb), lambda i, j: (i, j)), pl.BlockSpec((tm, cb), lambda i, j: (i, j + nf)),
                         pl.BlockSpec((HALO, cb), _prev_map(tm, lambda j: j)),
                         pl.BlockSpec((HALO, cb), _prev_map(tm, lambda j: j + nf)),
                         pl.BlockSpec((kf, cb), lambda i, j: (0, j)), pl.BlockSpec((kf, cb), lambda i, j: (0, j + nf))],
               out_specs=pl.BlockSpec((tm, cb), lambda i, j: (i, j)), out_shape=jax.ShapeDtypeStruct((s, f), BF16),
               compiler_params=_cp(("parallel", "parallel")))(u_pre, u_pre, u_pre, u_pre, conv_w, conv_w)


def _ffn_act_bwd(da, u_pre, conv_w):
    s, f2 = u_pre.shape
    f = f2 // 2
    tm = _tile(s, (256, 128, 64))
    nt = s // tm
    cb = _tile(f, (512, 256, 128))
    nf = f // cb
    kf = conv_w.shape[0]

    def body(da_ref, dan_ref, g_ref, gp_ref, gn_ref, u_ref, up_ref, un_ref, wg_ref, wu_ref, d_ref, dw_ref):
        i = pl.program_id(1)

        @pl.when(i == 0)
        def _():
            dw_ref[...] = jnp.zeros_like(dw_ref)

        def conv(c_ref, p_ref, n_ref, w_ref):
            ext = jnp.concatenate([c_ref[...].astype(F32), n_ref[...].astype(F32)], axis=0)
            prev = jnp.where(i > 0, p_ref[...].astype(F32), 0.0)
            sh = [_shift_down(ext, prev, kf - 1 - j) for j in range(kf)]
            out = sh[0] * w_ref[0:1, :]
            for j in range(1, kf):
                out = out + sh[j] * w_ref[j:j + 1, :]
            return out, sh

        gate, gsh = conv(g_ref, gp_ref, gn_ref, wg_ref)
        up, ush = conv(u_ref, up_ref, un_ref, wu_ref)
        dav = jnp.concatenate([da_ref[...].astype(F32), dan_ref[...].astype(F32)], axis=0)
        rows = lax.broadcasted_iota(jnp.int32, (tm + HALO, 1), 0)
        dav = jnp.where((rows < tm) | (i < nt - 1), dav, 0.0)
        sg = _sigmoid(gate)
        dgate = dav * up * (sg * (1.0 + gate * (1.0 - sg)))
        dup = dav * (gate * sg)

        def conv_t(dv, w_ref):
            out = dv[:tm] * w_ref[kf - 1:kf, :]
            for j in range(kf - 1):
                out = out + _shift_up(dv, kf - 1 - j, tm) * w_ref[j:j + 1, :]
            return out

        d_ref[0] = conv_t(dgate, wg_ref).astype(BF16)
        d_ref[1] = conv_t(dup, wu_ref).astype(BF16)
        for j in range(kf):
            dw_ref[0, j:j + 1, :] += jnp.sum(dgate[:tm] * gsh[j][:tm], axis=0, keepdims=True)
            dw_ref[1, j:j + 1, :] += jnp.sum(dup[:tm] * ush[j][:tm], axis=0, keepdims=True)

    pm = lambda off: (lambda j, i: _prev_map(tm, lambda c: c + off)(i, j))
    nm = lambda off: (lambda j, i: _next_map(tm, s, lambda c: c + off)(i, j))
    du, dw = _pc(
        body, name="ffn_act_bwd", grid=(nf, nt),
        in_specs=[pl.BlockSpec((tm, cb), lambda j, i: (i, j)), pl.BlockSpec((HALO, cb), nm(0)),
                  pl.BlockSpec((tm, cb), lambda j, i: (i, j)), pl.BlockSpec((HALO, cb), pm(0)),
                  pl.BlockSpec((HALO, cb), nm(0)),
                  pl.BlockSpec((tm, cb), lambda j, i: (i, j + nf)), pl.BlockSpec((HALO, cb), pm(nf)),
                  pl.BlockSpec((HALO, cb), nm(nf)),
                  pl.BlockSpec((kf, cb), lambda j, i: (0, j)), pl.BlockSpec((kf, cb), lambda j, i: (0, j + nf))],
        out_specs=[pl.BlockSpec((2, tm, cb), lambda j, i: (0, i, j)), pl.BlockSpec((2, 8, cb), lambda j, i: (0, 0, j))],
        out_shape=[jax.ShapeDtypeStruct((2, s, f), BF16), jax.ShapeDtypeStruct((2, 8, f), F32)],
        compiler_params=_cp(("parallel", "arbitrary")))(da, da, u_pre, u_pre, u_pre, u_pre, u_pre, u_pre, conv_w, conv_w)
    return du, dw[:, :kf]


def _adamw(w, g, m, v, name):
    shape = w.shape
    c = shape[-1]
    r = w.size // c
    tr = r if r * c <= 262144 else _tile(r, tuple(t for t in (512, 256, 128, 64, 32, 16, 8) if t * c <= 262144))
    bc1 = 1.0 - ADAM_B1 ** ADAM_STEP
    bc2 = 1.0 - ADAM_B2 ** ADAM_STEP

    def body(w_ref, g_ref, m_ref, v_ref, d_ref, nm_ref, nv_ref):
        gv = g_ref[...]
        mn = ADAM_B1 * m_ref[...] + (1.0 - ADAM_B1) * gv
        vn = ADAM_B2 * v_ref[...] + (1.0 - ADAM_B2) * (gv * gv)
        nm_ref[...] = mn
        nv_ref[...] = vn
        d_ref[...] = -ADAM_LR * ((mn / bc1) / (jnp.sqrt(vn / bc2) + ADAM_EPS) + ADAM_WD * w_ref[...])

    blk = pl.BlockSpec((tr, c), lambda i: (i, 0))
    outs = _pc(body, name=name, grid=(r // tr,), in_specs=[blk] * 4, out_specs=[blk] * 3,
               out_shape=[jax.ShapeDtypeStruct((r, c), F32)] * 3,
               compiler_params=_cp(("parallel",)))(*(t.reshape(r, c) for t in (w, g, m, v)))
    return tuple(o.reshape(shape) for o in outs)


def _adamw_sharded(w, m, v, partials, name):
    nl, r, c = w.shape
    tr = _tile(r, tuple(t for t in (256, 128, 64, 32, 16, 8) if t * c <= 131072))
    bc1 = 1.0 - ADAM_B1 ** ADAM_STEP
    bc2 = 1.0 - ADAM_B2 ** ADAM_STEP

    def body(w_ref, m_ref, v_ref, *rest):
        p_refs, (g_ref, d_ref, nm_ref, nv_ref) = rest[:nl], rest[nl:]
        layer = pl.program_id(0)
        for l in range(nl):
            @pl.when(layer == l)
            def _(p_ref=p_refs[l]):
                gv = p_ref[0].astype(F32)
                for dev in range(1, N_DEV):
                    gv = gv + p_ref[dev].astype(F32)
                mn = ADAM_B1 * m_ref[...] + (1.0 - ADAM_B1) * gv
                vn = ADAM_B2 * v_ref[...] + (1.0 - ADAM_B2) * (gv * gv)
                g_ref[...] = gv
                nm_ref[...] = mn
                nv_ref[...] = vn
                d_ref[...] = -ADAM_LR * ((mn / bc1) / (jnp.sqrt(vn / bc2) + ADAM_EPS) + ADAM_WD * w_ref[...])

    blk = pl.BlockSpec((None, tr, c), lambda l, i: (l, i, 0))
    p_specs = [pl.BlockSpec((N_DEV, tr, c), lambda l, i, k=k: (0, jnp.where(l == k, i, 0), 0)) for k in range(nl)]
    return _pc(body, name=name, grid=(nl, r // tr), in_specs=[blk] * 3 + p_specs, out_specs=[blk] * 4,
               out_shape=[jax.ShapeDtypeStruct((nl, r, c), F32)] * 4,
               compiler_params=_cp(("arbitrary", "arbitrary")))(w, m, v, *partials)


def _slot_sum(x, name):
    _, r, c = x.shape
    tr = _tile(r, (512, 256, 128, 64, 32, 16, 8))

    def body(x_ref, o_ref):
        acc = x_ref[0].astype(F32)
        for d in range(1, N_DEV):
            acc = acc + x_ref[d].astype(F32)
        o_ref[...] = acc

    return _pc(body, name=name, grid=(r // tr,), in_specs=[pl.BlockSpec((N_DEV, tr, c), lambda i: (0, i, 0))],
               out_specs=pl.BlockSpec((tr, c), lambda i: (i, 0)), out_shape=jax.ShapeDtypeStruct((r, c), F32),
               compiler_params=_cp(("parallel",)))(x)


def _all_gather(xs, name, layers=None):
    na = len(xs)
    layers = tuple(range(xs[0].shape[0])) if layers is None else layers
    nl = len(layers) * na

    def body(*refs):
        x_refs, out_refs = refs[:na], refs[na:na + nl]
        send_sems, recv_sems, local_sems = refs[na + nl:]
        mx, my, mc = lax.axis_index("x"), lax.axis_index("y"), lax.axis_index("c")
        me, sibling = (mx, my, mc), (mx, my, 1 - mc)
        chips = [(1 - mx, my), (mx, 1 - my), (1 - mx, 1 - my)]

        def local(l):
            return x_refs[l % na].at[layers[l // na]]

        def slot(l, dev):
            return out_refs[l].at[4 * dev[0] + 2 * dev[1] + dev[2]]

        def copy(l, k, block, to, src=None):
            return pltpu.make_async_remote_copy(
                src_ref=slot(l, block) if src is None else src, dst_ref=slot(l, block),
                send_sem=send_sems.at[l, k], recv_sem=recv_sems.at[l, k], device_id=to, device_id_type=MESH)

        mine = [pltpu.make_async_copy(local(l), slot(l, me), local_sems.at[l]) for l in range(nl)]
        first = []
        for l in range(nl):
            mine[l].start()
            first.append(copy(l, 0, me, sibling, src=local(l)))
            first += [copy(l, 1 + j, me, (*chip, mc), src=local(l)) for j, chip in enumerate(chips)]
        for cp in first:
            cp.start()
        passed = []
        for l in range(nl):
            for j, chip in enumerate(chips):
                copy(l, 1 + j, (*chip, mc), me).wait_recv()
                fw = copy(l, 4 + j, (*chip, mc), sibling)
                fw.start()
                passed.append(fw)
        for l in range(nl):
            copy(l, 0, sibling, me).wait_recv()
            for j, chip in enumerate(chips):
                copy(l, 4 + j, (*chip, 1 - mc), me).wait_recv()
        for cp in first + passed:
            cp.wait_send()
        for l in range(nl):
            mine[l].wait()

    anyspec = pl.BlockSpec(memory_space=pl.ANY)
    outs = _pc(body, name=name, in_specs=[anyspec] * na, out_specs=[anyspec] * nl,
               out_shape=[jax.ShapeDtypeStruct((N_DEV,) + xs[l % na].shape[1:], xs[l % na].dtype) for l in range(nl)],
               scratch_shapes=[pltpu.SemaphoreType.DMA((nl, 7)), pltpu.SemaphoreType.DMA((nl, 7)),
                               pltpu.SemaphoreType.DMA((nl,))],
               compiler_params=_cp())(*xs)
    return [outs[l * na:(l + 1) * na] for l in range(nl // na)]


def _mix_in_assemble(shards, width, nh):
    _, d, cs = shards.shape
    tr = _tile(d, (128, 64, 32, 16))
    w4 = 4 * width

    def body(s_ref, main_ref, gate_ref):
        full = jnp.concatenate([s_ref[j].astype(F32) for j in range(N_DEV)], axis=1)
        main_ref[:, :width] = full[:, 3 * width:w4].astype(BF16)
        main_ref[:, width:w4] = full[:, w4 + 2 * nh:].astype(BF16)
        main_ref[:, w4:] = full[:, :3 * width].astype(BF16)
        gate_ref[...] = jnp.concatenate([full[:, w4:w4 + 2 * nh], jnp.zeros((tr, 128 - 2 * nh), F32)],
                                        axis=1).astype(BF16)

    return _pc(body, name="mix_in_assemble", grid=(d // tr,),
               in_specs=[pl.BlockSpec((N_DEV, tr, cs), lambda i: (0, i, 0))],
               out_specs=[pl.BlockSpec((tr, 7 * width), lambda i: (i, 0)), pl.BlockSpec((tr, 128), lambda i: (i, 0))],
               out_shape=[jax.ShapeDtypeStruct((d, 7 * width), BF16), jax.ShapeDtypeStruct((d, 128), BF16)],
               compiler_params=_cp(("parallel",)))(shards)


def _mix_in_shards(dmain, dgate, width, nh):
    d = dmain.shape[0]
    cs = (7 * width + 2 * nh) // N_DEV
    tr = _tile(d, (128, 64, 32, 16))
    w4 = 4 * width

    def body(main_ref, gate_ref, o_ref):
        full = jnp.concatenate([main_ref[:, w4:], main_ref[:, :width], gate_ref[:, :2 * nh], main_ref[:, width:w4]],
                               axis=1)
        for j in range(N_DEV):
            o_ref[j] = full[:, j * cs:(j + 1) * cs].astype(BF16)

    return _pc(body, name="mix_in_shards", grid=(d // tr,),
               in_specs=[pl.BlockSpec((tr, 7 * width), lambda i: (i, 0)), pl.BlockSpec((tr, 128), lambda i: (i, 0))],
               out_specs=pl.BlockSpec((N_DEV, tr, cs), lambda i: (0, i, 0)),
               out_shape=jax.ShapeDtypeStruct((N_DEV, d, cs), BF16),
               compiler_params=_cp(("parallel",)))(dmain, dgate)


PACK_COLS = 1024
BIG = ("w_mix_in", "w_mix_out", "w_xq", "w_xk", "w_xv", "w_xo", "w_ffn_up", "w_ffn_down")


def _pack(blocks, lead):
    flat = jnp.concatenate(blocks, axis=-1)
    n = flat.shape[-1]
    per = 16 * PACK_COLS
    pad = (-n) % per
    flat = jnp.pad(flat, [(0, 0)] * len(lead) + [(0, pad)])
    return flat.reshape(*lead, (n + pad) // PACK_COLS, PACK_COLS)


def _unpack(packed, sizes):
    flat = packed.reshape(*packed.shape[:-2], -1)
    out, off = [], 0
    for n in sizes:
        out.append(flat[..., off:off + n])
        off += n
    return out


def kernel(x, mem, mix_norm, w_mix_in, gdn_conv, gdn_a_log, gdn_dt_bias, gdn_out_norm, sc_conv, w_mix_out, xattn_norm, mem_norm, w_xq, w_xk, w_xv, w_xo, ffn_norm, w_ffn_up, ffn_conv, w_ffn_down, final_norm, loss_target, m_mix_norm, m_w_mix_in, m_gdn_conv, m_gdn_a_log, m_gdn_dt_bias, m_gdn_out_norm, m_sc_conv, m_w_mix_out, m_xattn_norm, m_mem_norm, m_w_xq, m_w_xk, m_w_xv, m_w_xo, m_ffn_norm, m_w_ffn_up, m_ffn_conv, m_w_ffn_down, m_final_norm, v_mix_norm, v_w_mix_in, v_gdn_conv, v_gdn_a_log, v_gdn_dt_bias, v_gdn_out_norm, v_sc_conv, v_w_mix_out, v_xattn_norm, v_mem_norm, v_w_xq, v_w_xk, v_w_xv, v_w_xo, v_ffn_norm, v_w_ffn_up, v_ffn_conv, v_w_ffn_down, v_final_norm):
    names = ["mix_norm", "w_mix_in", "gdn_conv", "gdn_a_log", "gdn_dt_bias", "gdn_out_norm", "sc_conv", "w_mix_out",
             "xattn_norm", "mem_norm", "w_xq", "w_xk", "w_xv", "w_xo", "ffn_norm", "w_ffn_up", "ffn_conv",
             "w_ffn_down", "final_norm"]
    wts = dict(zip(names, (mix_norm, w_mix_in, gdn_conv, gdn_a_log, gdn_dt_bias, gdn_out_norm, sc_conv, w_mix_out,
                           xattn_norm, mem_norm, w_xq, w_xk, w_xv, w_xo, ffn_norm, w_ffn_up, ffn_conv, w_ffn_down,
                           final_norm)))
    mom1 = dict(zip(names, (m_mix_norm, m_w_mix_in, m_gdn_conv, m_gdn_a_log, m_gdn_dt_bias, m_gdn_out_norm, m_sc_conv,
                            m_w_mix_out, m_xattn_norm, m_mem_norm, m_w_xq, m_w_xk, m_w_xv, m_w_xo, m_ffn_norm,
                            m_w_ffn_up, m_ffn_conv, m_w_ffn_down, m_final_norm)))
    mom2 = dict(zip(names, (v_mix_norm, v_w_mix_in, v_gdn_conv, v_gdn_a_log, v_gdn_dt_bias, v_gdn_out_norm, v_sc_conv,
                            v_w_mix_out, v_xattn_norm, v_mem_norm, v_w_xq, v_w_xk, v_w_xv, v_w_xo, v_ffn_norm,
                            v_w_ffn_up, v_ffn_conv, v_w_ffn_down, v_final_norm)))

    x0 = x[0]
    memv = mem[0]
    target = loss_target[0]
    s, d = x0.shape
    depth = mix_norm.shape[0]
    width = d // 2
    nh = width // HEAD
    me = 4 * lax.axis_index("x") + 2 * lax.axis_index("y") + lax.axis_index("c")

    local_bf16 = {n: wts[n].astype(BF16) for n in BIG}
    gathered = [None] * depth
    gathered[0] = dict(zip(BIG, _all_gather([local_bf16[n] for n in BIG], "all_gather_weights", layers=(0,))[0]))

    def gather_next(l, group):
        return None if l + 1 >= depth else ("gather", [local_bf16[n] for n in group], l + 1)

    def mm_gathering(l, group, *args, **kw):
        carry = gather_next(l, group)
        if carry is None:
            return _mm(*args, **kw)
        out, got = _mm(*args, carry=carry, **kw)
        gathered[l + 1].update(zip(group, got))
        return out

    conv_names = ("gdn_conv", "sc_conv", "ffn_conv")
    conv_local = _pack([wts[n].reshape(1, -1) for n in conv_names], (1,))
    conv_all = _all_gather([conv_local], "all_gather_conv")[0][0]
    conv_parts = _unpack(conv_all, [wts[n].size for n in conv_names])
    conv_full = {}
    for n, part in zip(conv_names, conv_parts):
        _, kt, cs = wts[n].shape
        conv_full[n] = jnp.moveaxis(part.reshape(N_DEV, depth, kt, cs), 0, 2).reshape(depth, kt, N_DEV * cs)

    def layer_weights(l):
        wl = {n: gathered[l][n].reshape(-1, gathered[l][n].shape[-1]) for n in BIG if n not in ("w_mix_in", "w_ffn_up")}
        wl["w_ffn_up"] = gathered[l]["w_ffn_up"]
        return wl

    def gate_params(l):
        rows = jnp.stack([gdn_a_log[l], gdn_dt_bias[l]])
        return jnp.pad(rows, ((0, 6), (nh, 128 - 2 * nh)))

    saved = []
    xc = x0
    for l in range(depth):
        wl = layer_weights(l)
        w_main, w_gate = _mix_in_assemble(gathered[l]["w_mix_in"], width, nh)
        gp = gate_params(l)
        if l + 1 < depth:
            gathered[l + 1] = {}
        h1 = _rms_fwd(xc, mix_norm[l], "rms_mix")
        proj = mm_gathering(l, ("w_mix_in",), h1, w_main, out_dtype=BF16, name="mm_mix_in")
        ba = _mm(h1, w_gate, out_dtype=F32, name="mm_mix_gates")
        qkv = _gdn_prep_fwd(proj, conv_full["gdn_conv"][l], width)
        with_gdn = ("w_mix_out", "w_xq")
        o_raw, states, tinv, got = _gdn_fwd(qkv, ba, gp, width, carry=gather_next(l, with_gdn))
        if got:
            gathered[l + 1].update(zip(with_gdn, got))
        y = _mix_post_fwd(o_raw, proj, gdn_out_norm[l], conv_full["sc_conv"][l], width)
        x1 = mm_gathering(l, ("w_xk",), y, wl["w_mix_out"], res=xc, name="mm_mix_out")
        h2 = _rms_fwd(x1, xattn_norm[l], "rms_xattn")
        mem_n = _rms_fwd(memv, mem_norm[l], "rms_mem")
        qx = mm_gathering(l, ("w_xv",), h2, wl["w_xq"], out_dtype=BF16, name="mm_xq")
        kx = _mm(mem_n, wl["w_xk"], out_dtype=BF16, name="mm_xk")
        vx = _mm(mem_n, wl["w_xv"], out_dtype=BF16, name="mm_xv")
        ox = _xattn_fwd(qx, kx, vx)
        x2 = mm_gathering(l, ("w_xo",), ox, wl["w_xo"], res=x1, name="mm_xo")
        h3 = _rms_fwd(x2, ffn_norm[l], "rms_ffn")
        u_pre = mm_gathering(l, ("w_ffn_up",), h3, wl["w_ffn_up"], b_shards=True, out_dtype=BF16, name="mm_ffn_up")
        act = _ffn_act_fwd(u_pre, conv_full["ffn_conv"][l])
        x3 = mm_gathering(l, ("w_ffn_down",), act, wl["w_ffn_down"], res=x2, name="mm_ffn_down")
        saved.append(dict(x0=xc, x1=x1, x2=x2, h1=h1, h2=h2, h3=h3, proj=proj, ba=ba, qkv=qkv, o_raw=o_raw,
                          states=states, tinv=tinv, y=y, mem_n=mem_n, qx=qx, kx=kx, vx=vx, ox=ox, u_pre=u_pre, act=act,
                          w_main=w_main, w_gate=w_gate, gp=gp, wl=wl))
        xc = x3

    loss_part, dx, dxb, g_final = _loss_head(xc, final_norm, target)

    small = {n: [None] * depth for n in ("mix_norm", "xattn_norm", "mem_norm", "ffn_norm", "gdn_a_log", "gdn_dt_bias",
                                          "gdn_out_norm", "gdn_conv", "sc_conv", "ffn_conv")}
    partials = {n: [None] * depth for n in BIG}

    def scatter(big, group):
        return ("scatter", [big[n].reshape((N_DEV,) + wts[n].shape[1:]) for n in group], None)

    for l in reversed(range(depth)):
        sv = saved[l]
        wl, w_main, w_gate, gp = sv["wl"], sv["w_main"], sv["w_gate"], sv["gp"]
        big = {}
        d_act = _mm(dxb, wl["w_ffn_down"], tb=True, out_dtype=BF16, name="mm_d_act")
        big["w_ffn_down"] = _mm(sv["act"], dxb, ta=True, out_dtype=BF16, name="mm_dw_ffn_down")
        du, dcw = _ffn_act_bwd(d_act, sv["u_pre"], conv_full["ffn_conv"][l])
        small["ffn_conv"][l] = jnp.concatenate([dcw[0], dcw[1]], axis=1)
        big["w_ffn_up"], (partials["w_ffn_down"][l],) = _mm(
            sv["h3"], du, ta=True, b_halves=True, out_shards=True, out_dtype=BF16, name="mm_dw_ffn_up",
            carry=scatter(big, ("w_ffn_down",)))
        dh, (partials["w_ffn_up"][l],) = _mm(du, wl["w_ffn_up"], tb=True, a_halves=True, b_shards=True, out_dtype=BF16,
                                             name="mm_dh3", carry=scatter(big, ("w_ffn_up",)))
        dx, dxb, small["ffn_norm"][l], _ = _rms_bwd(sv["x2"], dh, ffn_norm[l], dx, "rms_bwd_ffn")
        d_ox = _mm(dxb, wl["w_xo"], tb=True, out_dtype=BF16, name="mm_d_ox")
        big["w_xo"] = _mm(sv["ox"], dxb, ta=True, out_dtype=BF16, name="mm_dw_xo")
        d_qx, d_kx, d_vx, (partials["w_xo"][l],) = _xattn_bwd(sv["qx"], sv["kx"], sv["vx"], d_ox,
                                                               carry=scatter(big, ("w_xo",)))
        big["w_xq"] = _mm(sv["h2"], d_qx, ta=True, out_dtype=BF16, name="mm_dw_xq")
        big["w_xk"] = _mm(sv["mem_n"], d_kx, ta=True, out_dtype=BF16, name="mm_dw_xk")
        big["w_xv"] = _mm(sv["mem_n"], d_vx, ta=True, out_dtype=BF16, name="mm_dw_xv")
        dh, (partials["w_xq"][l],) = _mm(d_qx, wl["w_xq"], tb=True, out_dtype=BF16, name="mm_dh2",
                                         carry=scatter(big, ("w_xq",)))
        d_mem = _mm(d_kx, wl["w_xk"], tb=True, out_dtype=F32, name="mm_dmem_k")
        d_mem = _mm(d_vx, wl["w_xv"], tb=True, out_dtype=F32, res=d_mem, name="mm_dmem_v")
        _, _, small["mem_norm"][l], _ = _rms_bwd(memv, d_mem, mem_norm[l], jnp.zeros_like(memv), "rms_bwd_mem")
        dx, dxb, small["xattn_norm"][l], (partials["w_xk"][l],) = _rms_bwd(
            sv["x1"], dh, xattn_norm[l], dx, "rms_bwd_xattn", carry=scatter(big, ("w_xk",)))
        d_y, (partials["w_xv"][l],) = _mm(dxb, wl["w_mix_out"], tb=True, out_dtype=BF16, name="mm_d_y",
                                          carry=scatter(big, ("w_xv",)))
        big["w_mix_out"] = _mm(sv["y"], dxb, ta=True, out_dtype=BF16, name="mm_dw_mix_out")
        d_o, d_proj, small["gdn_out_norm"][l], small["sc_conv"][l], (partials["w_mix_out"][l],) = _mix_post_bwd(
            d_y, sv["o_raw"], sv["proj"], gdn_out_norm[l], conv_full["sc_conv"][l], width,
            carry=scatter(big, ("w_mix_out",)))
        d_qkv, d_gb, got = _gdn_bwd(sv["qkv"], sv["ba"], gp, d_o, sv["states"], sv["tinv"], width,
                                    carry=None if l + 1 >= depth else ("scatter", [mix_in_above], None))
        if got:
            partials["w_mix_in"][l + 1] = got[0]
        d_ba, d_gp = _gates_bwd(sv["ba"], d_gb, gp, nh)
        small["gdn_a_log"][l] = d_gp[0, nh:2 * nh]
        small["gdn_dt_bias"][l] = d_gp[1, nh:2 * nh]
        d_proj, small["gdn_conv"][l] = _gdn_prep_bwd(d_qkv, sv["proj"], conv_full["gdn_conv"][l], d_proj, width)
        dw_main = _mm(sv["h1"], d_proj, ta=True, name="mm_dw_mix_in")
        dw_gate = _mm(sv["h1"], d_ba, ta=True, name="mm_dw_mix_gates")
        big["w_mix_in"] = _mix_in_shards(dw_main, dw_gate, width, nh)
        if l > 0:
            mix_in_above = big["w_mix_in"]
            dh = _mm(d_proj, w_main, tb=True, out_dtype=F32, name="mm_dh1_main")
        else:
            dh, (partials["w_mix_in"][l],) = _mm(d_proj, w_main, tb=True, out_dtype=F32, name="mm_dh1_main",
                                                 carry=scatter(big, ("w_mix_in",)))
        dh = _mm(d_ba, w_gate, tb=True, out_dtype=BF16, res=dh, name="mm_dh1_gates")
        dx, dxb, small["mix_norm"][l], _ = _rms_bwd(sv["x0"], dh, mix_norm[l], dx, "rms_bwd_mix")

    small_names = ("mix_norm", "xattn_norm", "mem_norm", "ffn_norm", "gdn_a_log", "gdn_dt_bias", "gdn_out_norm",
                   "gdn_conv", "sc_conv", "ffn_conv")
    small_parts = [jnp.stack(small[n]).reshape(1, -1) for n in small_names]
    small_parts += [g_final.reshape(1, -1), loss_part.reshape(1, 1)]
    small_sizes = [p.shape[1] for p in small_parts]
    small_local = _pack(small_parts, (1,))
    small_sum = _slot_sum(_all_gather([small_local], "all_gather_small")[0][0], "sum_small")
    small_tot = _unpack(small_sum, small_sizes)
    grads = {}
    for n, g in zip(small_names, small_tot[:len(small_names)]):
        if n in conv_names:
            _, kt, cs = wts[n].shape
            g = lax.dynamic_slice_in_dim(g.reshape(depth, kt, N_DEV * cs), me * cs, cs, axis=2)
        grads[n] = g.reshape(wts[n].shape)
    grads["final_norm"] = small_tot[-2].reshape(final_norm.shape)
    loss = small_tot[-1].reshape(())

    delta, new_m, new_v = {}, {}, {}
    for n in names:
        if n in BIG:
            grads[n], delta[n], new_m[n], new_v[n] = _adamw_sharded(wts[n], mom1[n], mom2[n], partials[n], "adamw_" + n)
        else:
            delta[n], new_m[n], new_v[n] = _adamw(wts[n], grads[n], mom1[n], mom2[n], "adamw_" + n)
    return (loss, dx[None], *[grads[n] for n in names], *[delta[n] for n in names],
            *[new_m[n] for n in names], *[new_v[n] for n in names])
```

```python
import functools

import jax
import jax.numpy as jnp
from jax import lax
from jax.experimental import pallas as pl
from jax.experimental.pallas import tpu as pltpu

F32 = jnp.float32
BF16 = jnp.bfloat16
CHUNK = 64
HEAD = 128
XHEADS = 4
GDN_K = 4
EPS = 1e-6
HALO = 16
QKV_COL = 4
N_DEV = 8
VMEM_LIMIT = 56 * 1024 * 1024
ADAM_LR, ADAM_B1, ADAM_B2, ADAM_EPS, ADAM_WD, ADAM_STEP = 0.001, 0.9, 0.999, 1e-08, 0.01, 10
MESH = pl.DeviceIdType.MESH
MM_TILES_MN = (1408, 1024, 512, 256, 128)
MM_TILES_K = (2048, 1408, 1024, 512, 256, 128)
ACT_TILES = (1408, 512, 256, 128)


def _call(body, **kw):
    return pl.pallas_call(body, **kw)


def _pc(body, *, carry=None, **kw):
    if carry is None:
        return _call(body, **kw)
    c_args, c_in, c_shapes, c_out, c_sems = _carry_parts(carry)
    nca = len(c_args)
    grid = kw["grid"]
    single = not isinstance(kw["out_shape"], (list, tuple))
    out_shape = [kw["out_shape"]] if single else list(kw["out_shape"])
    out_specs = [kw["out_specs"]] if single else list(kw["out_specs"])
    n_in, n_out = len(kw["in_specs"]), len(out_shape)

    def wrapped(*refs):
        ins, srcs = refs[:n_in], refs[n_in:n_in + nca]
        outs = refs[n_in + nca:n_in + nca + n_out]
        dsts = refs[n_in + nca + n_out:n_in + 2 * nca + n_out]
        scratch = refs[n_in + 2 * nca + n_out:]
        ids = [pl.program_id(ax) for ax in range(len(grid))]
        first = functools.reduce(lambda p, q: p & q, [i == 0 for i in ids])
        last = functools.reduce(lambda p, q: p & q, [i == g - 1 for i, g in zip(ids, grid)])
        _carry_run(carry, srcs, dsts, scratch[-2], scratch[-1], first, last)
        body(*ins, *outs, *scratch[:-2])

    kw = dict(kw, in_specs=list(kw["in_specs"]) + c_in, out_specs=out_specs + c_out, out_shape=out_shape + c_shapes,
              scratch_shapes=list(kw.get("scratch_shapes", [])) + c_sems,
              compiler_params=_cp(("arbitrary",) * len(grid)))
    call = _call(wrapped, **kw)

    def run(*args):
        res = call(*args, *c_args)
        return (res[0] if single else tuple(res[:n_out])), list(res[n_out:])

    return run


def _cp(sem=None, **kw):
    if sem is not None:
        kw["dimension_semantics"] = sem
    return pltpu.CompilerParams(vmem_limit_bytes=VMEM_LIMIT, **kw)


def _tile(n, cands):
    for c in cands:
        if n % c == 0:
            return c
    return n


def _sigmoid(x):
    return 0.5 * jnp.tanh(0.5 * x) + 0.5

def _softplus(x):
    return jnp.maximum(x, 0.0) + jnp.log(1.0 + jnp.exp(-jnp.abs(x)))


def _mmb(a, b):
    return jnp.dot(a.astype(BF16), b.astype(BF16), preferred_element_type=F32)


def _mmb_nt(a, b):
    return lax.dot_general(a.astype(BF16), b.astype(BF16), (((1,), (1,)), ((), ())), preferred_element_type=F32)


def _mmb_tn(a, b):
    return lax.dot_general(a.astype(BF16), b.astype(BF16), (((0,), (0,)), ((), ())), preferred_element_type=F32)


def _split(x):
    hi = x.astype(BF16)
    return hi, x - hi.astype(F32)


def _mm3(a, b):
    ah, ar = _split(a)
    bh, br = _split(b)
    al, bl = ar.astype(BF16), br.astype(BF16)
    return (jnp.dot(ah, bh, preferred_element_type=F32)
            + (jnp.dot(ah, bl, preferred_element_type=F32) + jnp.dot(al, bh, preferred_element_type=F32)))


def _mm_exact_lhs(a, b):
    ab = a.astype(BF16)
    b1, r1 = _split(b)
    b2, r2 = _split(r1)
    return (jnp.dot(ab, b1, preferred_element_type=F32)
            + (jnp.dot(ab, b2, preferred_element_type=F32) + jnp.dot(ab, r2.astype(BF16), preferred_element_type=F32)))


def _shift_down(cur, prev, s):
    if s == 0:
        return cur
    r = pltpu.roll(cur, s, 0)
    p = pltpu.roll(prev, s, 0)
    rows = lax.broadcasted_iota(jnp.int32, prev.shape, 0)
    first = jnp.where(rows < s, p, r[:HALO])
    return jnp.concatenate([first, r[HALO:]], axis=0)


def _shift_up(ext, s, tm):
    if s == 0:
        return ext[:tm]
    return pltpu.roll(ext, ext.shape[0] - s, 0)[:tm]


def _prev_map(tm, col):
    return lambda i, j: (jnp.maximum(i * (tm // HALO) - 1, 0), col(j))


def _next_map(tm, nrows, col):
    return lambda i, j: (jnp.minimum((i + 1) * (tm // HALO), nrows // HALO - 1), col(j))


def _carry_parts(carry):
    if carry is None:
        return [], [], [], [], []
    _, srcs, _ = carry
    na = len(srcs)
    anyspec = pl.BlockSpec(memory_space=pl.ANY)
    return (list(srcs), [anyspec] * na, [jax.ShapeDtypeStruct((N_DEV,) + s.shape[1:], s.dtype) for s in srcs],
            [anyspec] * na, [pltpu.SemaphoreType.DMA((na, N_DEV)), pltpu.SemaphoreType.DMA((na, N_DEV))])


def _carry_run(carry, src_refs, dst_refs, send_sems, recv_sems, first, last):
    if carry is None:
        return
    kind, _, layer = carry
    mx, my, mc = lax.axis_index("x"), lax.axis_index("y"), lax.axis_index("c")
    me = 4 * mx + 2 * my + mc

    def descriptors(with_recvs):
        sends, recvs = [], []
        for a in range(len(dst_refs)):
            for k in range(N_DEV):
                px, py, pc = mx ^ (k >> 2), my ^ ((k >> 1) & 1), mc ^ (k & 1)
                peer = 4 * px + 2 * py + pc
                src = src_refs[a].at[peer] if kind == "scatter" else src_refs[a].at[layer]
                if k == 0:
                    sends.append(pltpu.make_async_copy(src, dst_refs[a].at[me], send_sems.at[a, 0]))
                    continue
                sends.append(pltpu.make_async_remote_copy(
                    src_ref=src, dst_ref=dst_refs[a].at[me], send_sem=send_sems.at[a, k], recv_sem=recv_sems.at[a, k],
                    device_id=(px, py, pc), device_id_type=MESH))
                if with_recvs:
                    recvs.append(pltpu.make_async_remote_copy(
                        src_ref=src, dst_ref=dst_refs[a].at[peer], send_sem=send_sems.at[a, k],
                        recv_sem=recv_sems.at[a, k], device_id=(mx, my, mc), device_id_type=MESH))
        return sends, recvs

    @pl.when(first)
    def _():
        for cp in descriptors(False)[0]:
            cp.start()

    @pl.when(last)
    def _():
        sends, recvs = descriptors(True)
        for cp in recvs:
            cp.wait_recv()
        for i, cp in enumerate(sends):
            if i % N_DEV == 0:
                cp.wait()
            else:
                cp.wait_send()


def _mm(a, b, *, ta=False, tb=False, out_dtype=F32, res=None, name, a_halves=False, b_shards=False,
        b_halves=False, out_shards=False, carry=None):
    if a_halves:
        m, k = a.shape[1], 2 * a.shape[2]
    else:
        m, k = (a.shape[1], a.shape[0]) if ta else a.shape
    if b_shards:
        cs = b.shape[2]
        n = b.shape[1] if tb else N_DEV * cs
    elif b_halves:
        n = 2 * b.shape[2]
        cs = n // N_DEV
    else:
        n = b.shape[0] if tb else b.shape[1]
        cs = None
    tm = _tile(m, MM_TILES_MN)
    tn = cs if (cs is not None and not tb) else _tile(n, MM_TILES_MN)
    tk = cs if (b_shards and tb) else _tile(k, MM_TILES_K)
    nk = k // tk
    dn = (((0 if ta else 1,), (1 if tb else 0,)), ((), ()))

    def body(a_ref, b_ref, *rest):
        if res is None:
            o_ref, acc = rest
        else:
            r_ref, o_ref, acc = rest
        kk = pl.program_id(2)
        part = lax.dot_general(a_ref[...].astype(BF16), b_ref[...].astype(BF16), dn, preferred_element_type=F32)

        def finish(total):
            if res is not None:
                total = total + r_ref[...].astype(F32)
            o_ref[...] = total.astype(out_dtype)

        if nk == 1:
            finish(part)
            return

        @pl.when(kk == 0)
        def _():
            acc[...] = part

        @pl.when((kk > 0) & (kk < nk - 1))
        def _():
            acc[...] += part

        @pl.when(kk == nk - 1)
        def _():
            finish(acc[...] + part)

    if a_halves:
        per = (k // 2) // tk
        a_spec = pl.BlockSpec((None, tm, tk), lambda i, j, kk: (kk // per, i, kk % per))
    elif ta:
        a_spec = pl.BlockSpec((tk, tm), lambda i, j, kk: (kk, i))
    else:
        a_spec = pl.BlockSpec((tm, tk), lambda i, j, kk: (i, kk))
    if b_shards and tb:
        b_spec = pl.BlockSpec((None, tn, tk), lambda i, j, kk: (kk, j, 0))
    elif b_shards:
        b_spec = pl.BlockSpec((None, tk, tn), lambda i, j, kk: (j, kk, 0))
    elif b_halves:
        perb = (n // 2) // tn
        b_spec = pl.BlockSpec((None, tk, tn), lambda i, j, kk: (j // perb, kk, j % perb))
    elif tb:
        b_spec = pl.BlockSpec((tn, tk), lambda i, j, kk: (j, kk))
    else:
        b_spec = pl.BlockSpec((tk, tn), lambda i, j, kk: (kk, j))
    if out_shards:
        o_spec = pl.BlockSpec((None, tm, tn), lambda i, j, kk: (j, i, 0))
        o_shape = jax.ShapeDtypeStruct((N_DEV, m, tn), out_dtype)
    else:
        o_spec = pl.BlockSpec((tm, tn), lambda i, j, kk: (i, j))
        o_shape = jax.ShapeDtypeStruct((m, n), out_dtype)
    in_specs = [a_spec, b_spec] + ([o_spec] if res is not None else [])
    args = (a, b) + ((res,) if res is not None else ())
    return _pc(body, carry=carry, name=name, grid=(m // tm, n // tn, nk), in_specs=in_specs, out_specs=o_spec,
               out_shape=o_shape, scratch_shapes=[pltpu.VMEM((tm, tn), F32)],
               compiler_params=_cp(("parallel", "parallel", "arbitrary")))(*args)


def _rms_fwd(x, w, name):
    s, d = x.shape
    tm = _tile(s, (512, 256, 128, 64))

    def body(x_ref, w_ref, o_ref):
        xv = x_ref[...]
        r = lax.rsqrt(jnp.mean(xv * xv, axis=-1, keepdims=True) + EPS)
        o_ref[...] = (xv * r * w_ref[...]).astype(BF16)

    return _pc(body, name=name, grid=(s // tm,),
               in_specs=[pl.BlockSpec((tm, d), lambda i: (i, 0)), pl.BlockSpec((1, d), lambda i: (0, 0))],
               out_specs=pl.BlockSpec((tm, d), lambda i: (i, 0)), out_shape=jax.ShapeDtypeStruct((s, d), BF16),
               compiler_params=_cp(("parallel",)))(x, w.reshape(1, d))


def _rms_bwd(x, dh, w, dx_in, name, carry=None):
    s, d = x.shape
    tm = _tile(s, (256, 128, 64))

    def body(x_ref, dh_ref, w_ref, dxi_ref, dx_ref, dxb_ref, dg_ref):
        @pl.when(pl.program_id(0) == 0)
        def _():
            dg_ref[...] = jnp.zeros_like(dg_ref)

        xv = x_ref[...]
        dy = dh_ref[...].astype(F32)
        r = lax.rsqrt(jnp.mean(xv * xv, axis=-1, keepdims=True) + EPS)
        xh = xv * r
        dxh = dy * w_ref[...]
        dx = dxi_ref[...] + r * (dxh - xh * jnp.mean(dxh * xh, axis=-1, keepdims=True))
        dx_ref[...] = dx
        dxb_ref[...] = dx.astype(BF16)
        dg_ref[0:1, :] += jnp.sum(dy * xh, axis=0, keepdims=True)

    row = pl.BlockSpec((tm, d), lambda i: (i, 0))
    outs = _pc(body, carry=carry, name=name, grid=(s // tm,),
               in_specs=[row, row, pl.BlockSpec((1, d), lambda i: (0, 0)), row],
               out_specs=[row, row, pl.BlockSpec((8, d), lambda i: (0, 0))],
               out_shape=[jax.ShapeDtypeStruct((s, d), F32), jax.ShapeDtypeStruct((s, d), BF16),
                          jax.ShapeDtypeStruct((8, d), F32)],
               compiler_params=_cp(("arbitrary",)))(x, dh, w.reshape(1, d), dx_in)
    (dx, dxb, dg), got = outs if carry is not None else (outs, [])
    return dx, dxb, dg[0], got


def _loss_head(x, w, target):
    s, d = x.shape
    tm = _tile(s, (256, 128, 64))

    def body(x_ref, w_ref, t_ref, dx_ref, dxb_ref, dg_ref, l_ref):
        @pl.when(pl.program_id(0) == 0)
        def _():
            dg_ref[...] = jnp.zeros_like(dg_ref)
            l_ref[...] = jnp.zeros_like(l_ref)

        xv = x_ref[...]
        r = lax.rsqrt(jnp.mean(xv * xv, axis=-1, keepdims=True) + EPS)
        xh = xv * r
        err = xh * w_ref[...] - t_ref[...]
        l_ref[...] += 0.5 * jnp.sum(jnp.mean(err * err, axis=-1, keepdims=True), axis=0, keepdims=True)
        dy = err * (1.0 / d)
        dxh = dy * w_ref[...]
        dx = r * (dxh - xh * jnp.mean(dxh * xh, axis=-1, keepdims=True))
        dx_ref[...] = dx
        dxb_ref[...] = dx.astype(BF16)
        dg_ref[0:1, :] += jnp.sum(dy * xh, axis=0, keepdims=True)

    row = pl.BlockSpec((tm, d), lambda i: (i, 0))
    dx, dxb, dg, ls = _pc(body, name="loss_head", grid=(s // tm,),
                          in_specs=[row, pl.BlockSpec((1, d), lambda i: (0, 0)), row],
                          out_specs=[row, row, pl.BlockSpec((8, d), lambda i: (0, 0)),
                                     pl.BlockSpec((8, 128), lambda i: (0, 0))],
                          out_shape=[jax.ShapeDtypeStruct((s, d), F32), jax.ShapeDtypeStruct((s, d), BF16),
                                     jax.ShapeDtypeStruct((8, d), F32), jax.ShapeDtypeStruct((8, 128), F32)],
                          compiler_params=_cp(("arbitrary",)))(x, w.reshape(1, d), target)
    return ls[0, 0], dx, dxb, dg[0]


def _gdn_prep_fwd(proj, conv_w, width):
    s = proj.shape[0]
    tm = _tile(s, (256, 128, 64))
    nh = width // HEAD

    def body(c_ref, p_ref, w_ref, o_ref):
        i, seg = pl.program_id(0), pl.program_id(1)
        cur = c_ref[...].astype(F32)
        prev = jnp.where(i > 0, p_ref[...].astype(F32), 0.0)
        pre = cur * w_ref[GDN_K - 1:GDN_K, :]
        for j in range(GDN_K - 1):
            pre = pre + _shift_down(cur, prev, GDN_K - 1 - j) * w_ref[j:j + 1, :]
        act = pre * _sigmoid(pre)
        scale = jnp.where(seg == 0, HEAD ** -0.5, 1.0)
        for h in range(nh):
            a = act[:, h * HEAD:(h + 1) * HEAD]
            rs = lax.rsqrt(jnp.sum(a * a, axis=-1, keepdims=True) + EPS) * scale
            o_ref[:, h * HEAD:(h + 1) * HEAD] = a * jnp.where(seg < 2, rs, 1.0)

    return _pc(body, name="gdn_prep_fwd", grid=(s // tm, 3),
               in_specs=[pl.BlockSpec((tm, width), lambda i, j: (i, j + QKV_COL)),
                         pl.BlockSpec((HALO, width), _prev_map(tm, lambda j: j + QKV_COL)),
                         pl.BlockSpec((GDN_K, width), lambda i, j: (0, j))],
               out_specs=pl.BlockSpec((tm, width), lambda i, j: (i, j)),
               out_shape=jax.ShapeDtypeStruct((s, 3 * width), F32),
               compiler_params=_cp(("parallel", "parallel")))(proj, proj, conv_w)


def _gdn_prep_bwd(dqkv, proj, conv_w, dproj_in, width):
    s = proj.shape[0]
    tm = _tile(s, (256, 128, 64))
    nh = width // HEAD
    nt = s // tm

    def body(c_ref, p_ref, n_ref, d_ref, dn_ref, w_ref, _, o_ref, dw_ref):
        seg, i = pl.program_id(0), pl.program_id(1)

        @pl.when(i == 0)
        def _():
            dw_ref[...] = jnp.zeros_like(dw_ref)

        ext = jnp.concatenate([c_ref[...].astype(F32), n_ref[...].astype(F32)], axis=0)
        prev = jnp.where(i > 0, p_ref[...].astype(F32), 0.0)
        sh = [_shift_down(ext, prev, GDN_K - 1 - j) for j in range(GDN_K)]
        pre = sh[0] * w_ref[0:1, :]
        for j in range(1, GDN_K):
            pre = pre + sh[j] * w_ref[j:j + 1, :]
        sg = _sigmoid(pre)
        act = pre * sg
        dout = jnp.concatenate([d_ref[...], dn_ref[...]], axis=0)
        rows = lax.broadcasted_iota(jnp.int32, (tm + HALO, 1), 0)
        dout = jnp.where((rows < tm) | (i < nt - 1), dout, 0.0)
        scale = jnp.where(seg == 0, HEAD ** -0.5, 1.0)
        parts = []
        for h in range(nh):
            a = act[:, h * HEAD:(h + 1) * HEAD]
            dq = dout[:, h * HEAD:(h + 1) * HEAD]
            rs = lax.rsqrt(jnp.sum(a * a, axis=-1, keepdims=True) + EPS)
            nrm = a * rs
            dn = dq * scale
            da_norm = rs * (dn - nrm * jnp.sum(dn * nrm, axis=-1, keepdims=True))
            parts.append(jnp.where(seg < 2, da_norm, dq))
        dact = jnp.concatenate(parts, axis=1)
        dpre = dact * (sg * (1.0 + pre * (1.0 - sg)))
        dp = _shift_up(dpre, 0, tm) * w_ref[GDN_K - 1:GDN_K, :]
        for j in range(GDN_K - 1):
            dp = dp + _shift_up(dpre, GDN_K - 1 - j, tm) * w_ref[j:j + 1, :]
        o_ref[...] = dp.astype(BF16)
        for j in range(GDN_K):
            dw_ref[j:j + 1, :] += jnp.sum(dpre[:tm] * sh[j][:tm], axis=0, keepdims=True)

    dproj, dw = _pc(body, name="gdn_prep_bwd", grid=(3, nt),
                    in_specs=[pl.BlockSpec((tm, width), lambda j, i: (i, j + QKV_COL)),
                              pl.BlockSpec((HALO, width), lambda j, i: _prev_map(tm, lambda c: c + QKV_COL)(i, j)),
                              pl.BlockSpec((HALO, width), lambda j, i: _next_map(tm, s, lambda c: c + QKV_COL)(i, j)),
                              pl.BlockSpec((tm, width), lambda j, i: (i, j)),
                              pl.BlockSpec((HALO, width), lambda j, i: _next_map(tm, s, lambda c: c)(i, j)),
                              pl.BlockSpec((GDN_K, width), lambda j, i: (0, j)),
                              pl.BlockSpec(memory_space=pl.ANY)],
                    out_specs=[pl.BlockSpec((tm, width), lambda j, i: (i, j + QKV_COL)),
                               pl.BlockSpec((8, width), lambda j, i: (0, j))],
                    out_shape=[jax.ShapeDtypeStruct(dproj_in.shape, BF16), jax.ShapeDtypeStruct((8, 3 * width), F32)],
                    input_output_aliases={6: 0},
                    compiler_params=_cp(("parallel", "arbitrary")))(proj, proj, proj, dqkv, dqkv, conv_w, dproj_in)
    return dproj, dw[:GDN_K]


def _chunk_common(bav, gp_ref, nh):
    g_full = -jnp.exp(gp_ref[0:1, :]) * _softplus(bav + gp_ref[1:2, :])
    beta_full = _sigmoid(bav)
    ri = lax.broadcasted_iota(jnp.int32, (CHUNK, CHUNK), 0)
    ci = lax.broadcasted_iota(jnp.int32, (CHUNK, CHUNK), 1)
    gc_full = _mm_exact_lhs(ri >= ci, g_full)
    gc_t = gc_full.T
    return beta_full, gc_full, gc_t, ri, ci


def _head_gates(h, nh, beta_full, gc_full, gc_t, ri, ci):
    bcol = beta_full[:, h:h + 1]
    gcol = gc_full[:, nh + h:nh + h + 1]
    grow = gc_t[nh + h:nh + h + 1, :]
    dec = jnp.exp(jnp.where(ri >= ci, gcol - grow, -1e30))
    ecol = jnp.exp(gcol)
    gl = gcol[CHUNK - 1:CHUNK, :]
    return bcol, gcol, dec, ecol, gl


def _gdn_fwd(qkv, ba, gp, width, carry=None):
    s = qkv.shape[0]
    nh = width // HEAD
    nc = s // CHUNK
    heads = range(nh)

    def body(q_ref, k_ref, v_ref, ba_ref, gp_ref, o_ref, st_ref, t_ref, state):
        @pl.when(pl.program_id(0) == 0)
        def _():
            state[...] = jnp.zeros_like(state)

        beta_full, gc_full, gc_t, ri, ci = _chunk_common(ba_ref[...], gp_ref, nh)
        eye = (ri == ci).astype(F32)
        sls = [slice(h * HEAD, (h + 1) * HEAD) for h in heads]
        q = [q_ref[:, sl] for sl in sls]
        k = [k_ref[:, sl] for sl in sls]
        v = [v_ref[:, sl] for sl in sls]
        st = [state[h] for h in heads]
        gates = [_head_gates(h, nh, beta_full, gc_full, gc_t, ri, ci) for h in heads]
        bcol, gcol, dec, ecol, gl = (list(z) for z in zip(*gates))
        kb = [k[h] * bcol[h] for h in heads]
        a = [jnp.where(ri > ci, _mmb_nt(kb[h], k[h]) * dec[h], 0.0) for h in heads]
        attn = [jnp.where(ri >= ci, _mmb_nt(q[h], k[h]) * dec[h], 0.0) for h in heads]
        t = [eye - a[h] for h in heads]
        pw = [_mm3(a[h], a[h]) for h in heads]
        for _ in range(4):
            both = [_mm3(jnp.concatenate([t[h], pw[h]], axis=0), pw[h]) for h in heads]
            t = [t[h] + both[h][:CHUNK] for h in heads]
            pw = [both[h][CHUNK:] for h in heads]
        t = [t[h] + _mm3(t[h], pw[h]) for h in heads]
        uw = [_mmb(t[h], jnp.concatenate([v[h] * bcol[h], kb[h] * ecol[h]], axis=1)) for h in heads]
        vn = [uw[h][:, :HEAD] - _mmb(uw[h][:, HEAD:], st[h]) for h in heads]
        out = [_mmb(q[h] * ecol[h], st[h]) + _mmb(attn[h], vn[h]) for h in heads]
        new = [st[h] * jnp.exp(gl[h]) + _mmb_tn(k[h] * jnp.exp(gl[h] - gcol[h]), vn[h]) for h in heads]
        for h in heads:
            o_ref[:, sls[h]] = out[h]
            st_ref[0, h] = st[h]
            t_ref[0, h] = t[h]
            state[h] = new[h]

    blk = lambda c: pl.BlockSpec((CHUNK, width), lambda n, c=c: (n, c))
    outs = _pc(body, carry=carry, name="gdn_fwd", grid=(nc,),
               in_specs=[blk(0), blk(1), blk(2), pl.BlockSpec((CHUNK, 128), lambda n: (n, 0)),
                         pl.BlockSpec((8, 128), lambda n: (0, 0))],
               out_specs=[blk(0), pl.BlockSpec((1, nh, HEAD, HEAD), lambda n: (n, 0, 0, 0)),
                          pl.BlockSpec((1, nh, CHUNK, CHUNK), lambda n: (n, 0, 0, 0))],
               out_shape=[jax.ShapeDtypeStruct((s, width), F32), jax.ShapeDtypeStruct((nc, nh, HEAD, HEAD), F32),
                          jax.ShapeDtypeStruct((nc, nh, CHUNK, CHUNK), F32)],
               scratch_shapes=[pltpu.VMEM((nh, HEAD, HEAD), F32)],
               compiler_params=_cp(("arbitrary",)))(qkv, qkv, qkv, ba, gp)
    (o, st, t), got = outs if carry is not None else (outs, [])
    return o, st, t, got


def _gdn_bwd(qkv, ba, gp, do, states, tinv, width, carry=None):
    s = qkv.shape[0]
    nh = width // HEAD
    nc = s // CHUNK
    heads = range(nh)

    def body(q_ref, k_ref, v_ref, ba_ref, gp_ref, do_ref, st_ref, t_ref, dqkv_ref, dgb_ref, dstate):
        @pl.when(pl.program_id(0) == 0)
        def _():
            dstate[...] = jnp.zeros_like(dstate)

        beta_full, gc_full, gc_t, ri, ci = _chunk_common(ba_ref[...], gp_ref, nh)
        lane = lax.broadcasted_iota(jnp.int32, (CHUNK, 128), 1)
        rowi = lax.broadcasted_iota(jnp.int32, (CHUNK, 1), 0)
        low, strict = ri >= ci, ri > ci
        each = lambda fn: [fn(h) for h in heads]
        rowsum = lambda x: jnp.sum(x, axis=1, keepdims=True)
        sls = each(lambda h: slice(h * HEAD, (h + 1) * HEAD))
        q, k, v = each(lambda h: q_ref[:, sls[h]]), each(lambda h: k_ref[:, sls[h]]), each(lambda h: v_ref[:, sls[h]])
        dout = each(lambda h: do_ref[:, sls[h]])
        st, t, dsp = each(lambda h: st_ref[0, h]), each(lambda h: t_ref[0, h]), each(lambda h: dstate[h])
        gates = each(lambda h: _head_gates(h, nh, beta_full, gc_full, gc_t, ri, ci))
        bcol, gcol, dec, ecol, gl = (list(z) for z in zip(*gates))
        el = each(lambda h: jnp.exp(gl[h]))
        kdsc = each(lambda h: jnp.exp(gl[h] - gcol[h]))
        kb = each(lambda h: k[h] * bcol[h])
        a = each(lambda h: jnp.where(strict, _mmb_nt(kb[h], k[h]) * dec[h], 0.0))
        attn = each(lambda h: jnp.where(low, _mmb_nt(q[h], k[h]) * dec[h], 0.0))
        uw = each(lambda h: _mmb(t[h], jnp.concatenate([v[h] * bcol[h], kb[h] * ecol[h]], axis=1)))
        w = each(lambda h: uw[h][:, HEAD:])
        kd = each(lambda h: k[h] * kdsc[h])
        vn = each(lambda h: uw[h][:, :HEAD] - _mmb(w[h], st[h]))
        d_attn = each(lambda h: jnp.where(low, _mmb_nt(dout[h], vn[h]), 0.0))
        d_vn = each(lambda h: _mmb_tn(attn[h], dout[h]) + _mmb(kd[h], dsp[h]))
        d_qd = each(lambda h: _mmb_nt(dout[h], st[h]))
        d_kd = each(lambda h: _mmb_nt(vn[h], dsp[h]))
        d_el = each(lambda h: jnp.sum(rowsum(st[h] * dsp[h]), axis=0, keepdims=True))
        dst_new = each(lambda h: _mmb_tn(q[h] * ecol[h], dout[h]) + el[h] * dsp[h] - _mmb_tn(w[h], d_vn[h]))
        d_w = each(lambda h: -_mmb_nt(d_vn[h], st[h]))
        dr = each(lambda h: _mmb_tn(t[h], jnp.concatenate([d_vn[h], d_w[h]], axis=1)))
        dru, drw = each(lambda h: dr[h][:, :HEAD]), each(lambda h: dr[h][:, HEAD:])
        d_a = each(lambda h: -jnp.where(strict, _mmb_nt(dr[h], uw[h]), 0.0))
        d_kk = each(lambda h: d_a[h] * dec[h])
        d_qk = each(lambda h: d_attn[h] * dec[h])
        d_kb = each(lambda h: _mmb(d_kk[h], k[h]) + drw[h] * ecol[h])
        dk = each(lambda h: _mmb_tn(d_kk[h], kb[h]) + _mmb_tn(d_qk[h], q[h]) + d_kb[h] * bcol[h] + d_kd[h] * kdsc[h])
        dq = each(lambda h: _mmb(d_qk[h], k[h]) + d_qd[h] * ecol[h])
        dbeta = each(lambda h: rowsum(dru[h] * v[h] + d_kb[h] * k[h]))
        de = each(lambda h: rowsum(drw[h] * kb[h] + d_qd[h] * q[h]))
        r = each(lambda h: rowsum(d_kd[h] * k[h]) * kdsc[h])
        mm = each(lambda h: d_a[h] * a[h] + d_attn[h] * attn[h])
        d_gl = each(lambda h: jnp.sum(r[h], axis=0, keepdims=True) + d_el[h] * el[h])
        d_gc = each(lambda h: de[h] * ecol[h] - r[h] + rowsum(mm[h]) - rowsum(mm[h].T)
                    + jnp.where(rowi == CHUNK - 1, d_gl[h], 0.0))
        dbeta_full = jnp.zeros((CHUNK, 128), F32)
        dgc_full = jnp.zeros((CHUNK, 128), F32)
        for h in heads:
            dqkv_ref[:, sls[h]] = dq[h]
            dqkv_ref[:, width + h * HEAD:width + (h + 1) * HEAD] = dk[h]
            dqkv_ref[:, 2 * width + h * HEAD:2 * width + (h + 1) * HEAD] = dru[h] * bcol[h]
            dstate[h] = dst_new[h]
            dbeta_full = dbeta_full + jnp.where(lane == h, dbeta[h], 0.0)
            dgc_full = dgc_full + jnp.where(lane == nh + h, d_gc[h], 0.0)
        dgb_ref[...] = dbeta_full + _mm_exact_lhs(ri <= ci, dgc_full)

    rev = lambda c: pl.BlockSpec((CHUNK, width), lambda n, c=c: (nc - 1 - n, c))
    outs = _pc(body, carry=carry, name="gdn_bwd", grid=(nc,),
               in_specs=[rev(0), rev(1), rev(2), pl.BlockSpec((CHUNK, 128), lambda n: (nc - 1 - n, 0)),
                         pl.BlockSpec((8, 128), lambda n: (0, 0)), rev(0),
                         pl.BlockSpec((1, nh, HEAD, HEAD), lambda n: (nc - 1 - n, 0, 0, 0)),
                         pl.BlockSpec((1, nh, CHUNK, CHUNK), lambda n: (nc - 1 - n, 0, 0, 0))],
               out_specs=[pl.BlockSpec((CHUNK, 3 * width), lambda n: (nc - 1 - n, 0)),
                          pl.BlockSpec((CHUNK, 128), lambda n: (nc - 1 - n, 0))],
               out_shape=[jax.ShapeDtypeStruct((s, 3 * width), F32), jax.ShapeDtypeStruct((s, 128), F32)],
               scratch_shapes=[pltpu.VMEM((nh, HEAD, HEAD), F32)],
               compiler_params=_cp(("arbitrary",)))(qkv, qkv, qkv, ba, gp, do, states, tinv)
    (dqkv, dgb), got = outs if carry is not None else (outs, [])
    return dqkv, dgb, got


def _gates_bwd(ba, dgb, gp, nh):
    s = ba.shape[0]
    tm = _tile(s, (512, 256, 128, 64))

    def body(ba_ref, d_ref, gp_ref, o_ref, dp_ref):
        @pl.when(pl.program_id(0) == 0)
        def _():
            dp_ref[...] = jnp.zeros_like(dp_ref)

        bav, dv = ba_ref[...], d_ref[...]
        lane = lax.broadcasted_iota(jnp.int32, bav.shape, 1)
        beta = _sigmoid(bav)
        amp = jnp.exp(gp_ref[0:1, :])
        z = bav + gp_ref[1:2, :]
        d_a = dv * (-amp) * _sigmoid(z)
        d_b = dv * beta * (1.0 - beta)
        is_a = (lane >= nh) & (lane < 2 * nh)
        o_ref[...] = jnp.where(lane < nh, d_b, jnp.where(is_a, d_a, 0.0))
        dp_ref[0:1, :] += jnp.sum(jnp.where(is_a, dv * (-amp) * _softplus(z), 0.0), axis=0, keepdims=True)
        dp_ref[1:2, :] += jnp.sum(jnp.where(is_a, d_a, 0.0), axis=0, keepdims=True)

    row = pl.BlockSpec((tm, 128), lambda i: (i, 0))
    par = pl.BlockSpec((8, 128), lambda i: (0, 0))
    return _pc(body, name="gates_bwd", grid=(s // tm,), in_specs=[row, row, par], out_specs=[row, par],
               out_shape=[jax.ShapeDtypeStruct((s, 128), F32), jax.ShapeDtypeStruct((8, 128), F32)],
               compiler_params=_cp(("arbitrary",)))(ba, dgb, gp)


def _mix_post_fwd(o_raw, proj, gain, sc_w, width):
    s = o_raw.shape[0]
    tm = _tile(s, (256, 128, 64))
    nh = width // HEAD
    ksc = sc_w.shape[0]

    def body(o_ref, z_ref, b_ref, c_ref, h_ref, cp_ref, hp_ref, g_ref, w_ref, y_ref):
        i = pl.program_id(0)
        z = z_ref[...].astype(F32)
        sz = z * _sigmoid(z)
        for h in range(nh):
            sl = slice(h * HEAD, (h + 1) * HEAD)
            o = o_ref[:, sl]
            r = lax.rsqrt(jnp.mean(o * o, axis=-1, keepdims=True) + EPS)
            y_ref[:, sl] = (o * r * g_ref[...] * sz[:, sl]).astype(BF16)
        prod = c_ref[...].astype(F32) * h_ref[...].astype(F32)
        pprev = jnp.where(i > 0, cp_ref[...].astype(F32) * hp_ref[...].astype(F32), 0.0)
        cv = prod * w_ref[ksc - 1:ksc, :]
        for j in range(ksc - 1):
            cv = cv + _shift_down(prod, pprev, ksc - 1 - j) * w_ref[j:j + 1, :]
        y_ref[:, width:] = (b_ref[...].astype(F32) * cv).astype(BF16)

    col = lambda c: pl.BlockSpec((tm, width), lambda i, c=c: (i, c))
    prv = lambda c: pl.BlockSpec((HALO, width), lambda i, c=c: (jnp.maximum(i * (tm // HALO) - 1, 0), c))
    return _pc(body, name="mix_post_fwd", grid=(s // tm,),
               in_specs=[col(0), col(0), col(1), col(2), col(3), prv(2), prv(3),
                         pl.BlockSpec((1, HEAD), lambda i: (0, 0)), pl.BlockSpec((ksc, width), lambda i: (0, 0))],
               out_specs=pl.BlockSpec((tm, 2 * width), lambda i: (i, 0)),
               out_shape=jax.ShapeDtypeStruct((s, 2 * width), BF16),
               compiler_params=_cp(("parallel",)))(o_raw, proj, proj, proj, proj, proj, proj,
                                                    gain.reshape(1, HEAD), sc_w)


def _mix_post_bwd(dy, o_raw, proj, gain, sc_w, width, carry=None):
    s = o_raw.shape[0]
    tm = _tile(s, (256, 128, 64))
    nt = s // tm
    nh = width // HEAD
    ksc = sc_w.shape[0]

    def body(dyg_ref, dys_ref, dysn_ref, o_ref, z_ref, b_ref, bn_ref, c_ref, h_ref, cp_ref, hp_ref, g_ref, w_ref,
             do_ref, dp_ref, dg_ref, dw_ref):
        i = pl.program_id(0)

        @pl.when(i == 0)
        def _():
            dg_ref[...] = jnp.zeros_like(dg_ref)
            dw_ref[...] = jnp.zeros_like(dw_ref)

        z = z_ref[...].astype(F32)
        sg = _sigmoid(z)
        sz = z * sg
        dsz = sg * (1.0 + z * (1.0 - sg))
        dyg = dyg_ref[...].astype(F32)
        dgain = jnp.zeros((1, HEAD), F32)
        for h in range(nh):
            sl = slice(h * HEAD, (h + 1) * HEAD)
            o = o_ref[:, sl]
            r = lax.rsqrt(jnp.mean(o * o, axis=-1, keepdims=True) + EPS)
            oh = o * r
            d_yn = dyg[:, sl] * sz[:, sl]
            dp_ref[:, sl] = (dyg[:, sl] * oh * g_ref[...] * dsz[:, sl]).astype(BF16)
            dgain = dgain + jnp.sum(d_yn * oh, axis=0, keepdims=True)
            doh = d_yn * g_ref[...]
            do_ref[:, sl] = r * (doh - oh * jnp.mean(doh * oh, axis=-1, keepdims=True))
        dg_ref[0:1, :] += dgain
        cc, hh = c_ref[...].astype(F32), h_ref[...].astype(F32)
        prod = cc * hh
        pprev = jnp.where(i > 0, cp_ref[...].astype(F32) * hp_ref[...].astype(F32), 0.0)
        sh = [_shift_down(prod, pprev, ksc - 1 - j) for j in range(ksc)]
        cv = sh[0] * w_ref[0:1, :]
        for j in range(1, ksc):
            cv = cv + sh[j] * w_ref[j:j + 1, :]
        dys = dys_ref[...].astype(F32)
        dp_ref[:, width:2 * width] = (dys * cv).astype(BF16)
        dcv_n = jnp.where(i < nt - 1, dysn_ref[...].astype(F32) * bn_ref[...].astype(F32), 0.0)
        dcv = jnp.concatenate([dys * b_ref[...].astype(F32), dcv_n], axis=0)
        dprod = dcv[:tm] * w_ref[ksc - 1:ksc, :]
        for j in range(ksc - 1):
            dprod = dprod + _shift_up(dcv, ksc - 1 - j, tm) * w_ref[j:j + 1, :]
        dp_ref[:, 2 * width:3 * width] = (dprod * hh).astype(BF16)
        dp_ref[:, 3 * width:] = (dprod * cc).astype(BF16)
        for j in range(ksc):
            dw_ref[j:j + 1, :] += jnp.sum(dcv[:tm] * sh[j], axis=0, keepdims=True)

    col = lambda c: pl.BlockSpec((tm, width), lambda i, c=c: (i, c))
    prv = lambda c: pl.BlockSpec((HALO, width), lambda i, c=c: (jnp.maximum(i * (tm // HALO) - 1, 0), c))
    nxt = lambda c: pl.BlockSpec((HALO, width), lambda i, c=c: (jnp.minimum((i + 1) * (tm // HALO), s // HALO - 1), c))
    outs = _pc(
        body, carry=carry, name="mix_post_bwd", grid=(nt,),
        in_specs=[col(0), col(1), nxt(1), col(0), col(0), col(1), nxt(1), col(2), col(3), prv(2), prv(3),
                  pl.BlockSpec((1, HEAD), lambda i: (0, 0)), pl.BlockSpec((ksc, width), lambda i: (0, 0))],
        out_specs=[col(0), pl.BlockSpec((tm, 4 * width), lambda i: (i, 0)),
                   pl.BlockSpec((8, HEAD), lambda i: (0, 0)), pl.BlockSpec((8, width), lambda i: (0, 0))],
        out_shape=[jax.ShapeDtypeStruct((s, width), F32), jax.ShapeDtypeStruct((s, 7 * width), BF16),
                   jax.ShapeDtypeStruct((8, HEAD), F32), jax.ShapeDtypeStruct((8, width), F32)],
        compiler_params=_cp(("arbitrary",)))(dy, dy, dy, o_raw, proj, proj, proj, proj, proj, proj, proj,
                                             gain.reshape(1, HEAD), sc_w)
    (do, dp, dg, dw), got = outs if carry is not None else (outs, [])
    return do, dp, dg[0], dw[:ksc], got


def _xattn_fwd(q, k, v):
    s, d = q.shape
    nm = k.shape[0]
    dh = d // XHEADS
    tm = _tile(s, (512, 256, 128, 64))

    def body(q_ref, k_ref, v_ref, o_ref):
        sc = _mmb_nt(q_ref[...], k_ref[...]) * (dh ** -0.5)
        p = jnp.exp(sc - jnp.max(sc, axis=-1, keepdims=True))
        p = p / jnp.sum(p, axis=-1, keepdims=True)
        o_ref[...] = _mmb(p, v_ref[...]).astype(BF16)

    return _pc(body, name="xattn_fwd", grid=(s // tm, XHEADS),
               in_specs=[pl.BlockSpec((tm, dh), lambda i, h: (i, h)), pl.BlockSpec((nm, dh), lambda i, h: (0, h)),
                         pl.BlockSpec((nm, dh), lambda i, h: (0, h))],
               out_specs=pl.BlockSpec((tm, dh), lambda i, h: (i, h)), out_shape=jax.ShapeDtypeStruct((s, d), BF16),
               compiler_params=_cp(("parallel", "parallel")))(q, k, v)


def _xattn_bwd(q, k, v, do, carry=None):
    s, d = q.shape
    nm = k.shape[0]
    dh = d // XHEADS
    tm = _tile(s, (512, 256, 128, 64))

    def body(q_ref, k_ref, v_ref, do_ref, dq_ref, dk_ref, dv_ref):
        @pl.when(pl.program_id(1) == 0)
        def _():
            dk_ref[...] = jnp.zeros_like(dk_ref)
            dv_ref[...] = jnp.zeros_like(dv_ref)

        scale = dh ** -0.5
        sc = _mmb_nt(q_ref[...], k_ref[...]) * scale
        p = jnp.exp(sc - jnp.max(sc, axis=-1, keepdims=True))
        p = p / jnp.sum(p, axis=-1, keepdims=True)
        dp = _mmb_nt(do_ref[...], v_ref[...])
        ds = p * (dp - jnp.sum(dp * p, axis=-1, keepdims=True)) * scale
        dq_ref[...] = _mmb(ds, k_ref[...]).astype(BF16)
        dk_ref[...] += _mmb_tn(ds, q_ref[...])
        dv_ref[...] += _mmb_tn(p, do_ref[...])

    rowb = pl.BlockSpec((tm, dh), lambda h, i: (i, h))
    memb = pl.BlockSpec((nm, dh), lambda h, i: (0, h))
    outs = _pc(body, carry=carry, name="xattn_bwd", grid=(XHEADS, s // tm), in_specs=[rowb, memb, memb, rowb],
               out_specs=[rowb, memb, memb],
               out_shape=[jax.ShapeDtypeStruct((s, d), BF16), jax.ShapeDtypeStruct((nm, d), F32),
                          jax.ShapeDtypeStruct((nm, d), F32)],
               compiler_params=_cp(("parallel", "arbitrary")))(q, k, v, do)
    (dq, dk, dv), got = outs if carry is not None else (outs, [])
    return dq, dk, dv, got


def _ffn_act_fwd(u_pre, conv_w):
    s, f2 = u_pre.shape
    f = f2 // 2
    tm = _tile(s, (256, 128, 64))
    cb = _tile(f, ACT_TILES)
    nf = f // cb
    kf = conv_w.shape[0]

    def body(g_ref, u_ref, gp_ref, up_ref, wg_ref, wu_ref, a_ref):
        i = pl.program_id(0)

        def conv(c_ref, p_ref, w_ref):
            cur = c_ref[...].astype(F32)
            prev = jnp.where(i > 0, p_ref[...].astype(F32), 0.0)
            out = cur * w_ref[kf - 1:kf, :]
            for j in range(kf - 1):
                out = out + _shift_down(cur, prev, kf - 1 - j) * w_ref[j:j + 1, :]
            return out

        gate, up = conv(g_ref, gp_ref, wg_ref), conv(u_ref, up_ref, wu_ref)
        a_ref[...] = (gate * _sigmoid(gate) * up).astype(BF16)

    return _pc(body, name="ffn_act_fwd", grid=(s // tm, nf),
               in_specs=[pl.BlockSpec((tm, cb), lambda i, j: (i, j)), pl.BlockSpec((tm, cb), lambda i, j: (i, j + nf)),
                         pl.BlockSpec((HALO, cb), _prev_map(tm, lambda j: j)),
                         pl.BlockSpec((HALO, cb), _prev_map(tm, lambda j: j + nf)),
                         pl.BlockSpec((kf, cb), lambda i, j: (0, j)), pl.BlockSpec((kf, cb), lambda i, j: (0, j + nf))],
               out_specs=pl.BlockSpec((tm, cb), lambda i, j: (i, j)), out_shape=jax.ShapeDtypeStruct((s, f), BF16),
               compiler_params=_cp(("parallel", "parallel")))(u_pre, u_pre, u_pre, u_pre, conv_w, conv_w)


def _ffn_act_bwd(da, u_pre, conv_w, carry=None):
    s, f2 = u_pre.shape
    f = f2 // 2
    tm = _tile(s, (256, 128, 64))
    nt = s // tm
    cb = _tile(f, (512, 256, 128))
    nf = f // cb
    kf = conv_w.shape[0]

    def body(da_ref, dan_ref, g_ref, gp_ref, gn_ref, u_ref, up_ref, un_ref, wg_ref, wu_ref, d_ref, dw_ref):
        i = pl.program_id(1)

        @pl.when(i == 0)
        def _():
            dw_ref[...] = jnp.zeros_like(dw_ref)

        def conv(c_ref, p_ref, n_ref, w_ref):
            ext = jnp.concatenate([c_ref[...].astype(F32), n_ref[...].astype(F32)], axis=0)
            prev = jnp.where(i > 0, p_ref[...].astype(F32), 0.0)
            sh = [_shift_down(ext, prev, kf - 1 - j) for j in range(kf)]
            out = sh[0] * w_ref[0:1, :]
            for j in range(1, kf):
                out = out + sh[j] * w_ref[j:j + 1, :]
            return out, sh

        gate, gsh = conv(g_ref, gp_ref, gn_ref, wg_ref)
        up, ush = conv(u_ref, up_ref, un_ref, wu_ref)
        dav = jnp.concatenate([da_ref[...].astype(F32), dan_ref[...].astype(F32)], axis=0)
        rows = lax.broadcasted_iota(jnp.int32, (tm + HALO, 1), 0)
        dav = jnp.where((rows < tm) | (i < nt - 1), dav, 0.0)
        sg = _sigmoid(gate)
        dgate = dav * up * (sg * (1.0 + gate * (1.0 - sg)))
        dup = dav * (gate * sg)

        def conv_t(dv, w_ref):
            out = dv[:tm] * w_ref[kf - 1:kf, :]
            for j in range(kf - 1):
                out = out + _shift_up(dv, kf - 1 - j, tm) * w_ref[j:j + 1, :]
            return out

        d_ref[0] = conv_t(dgate, wg_ref).astype(BF16)
        d_ref[1] = conv_t(dup, wu_ref).astype(BF16)
        for j in range(kf):
            dw_ref[0, j:j + 1, :] += jnp.sum(dgate[:tm] * gsh[j][:tm], axis=0, keepdims=True)
            dw_ref[1, j:j + 1, :] += jnp.sum(dup[:tm] * ush[j][:tm], axis=0, keepdims=True)

    pm = lambda off: (lambda j, i: _prev_map(tm, lambda c: c + off)(i, j))
    nm = lambda off: (lambda j, i: _next_map(tm, s, lambda c: c + off)(i, j))
    outs = _pc(
        body, carry=carry, name="ffn_act_bwd", grid=(nf, nt),
        in_specs=[pl.BlockSpec((tm, cb), lambda j, i: (i, j)), pl.BlockSpec((HALO, cb), nm(0)),
                  pl.BlockSpec((tm, cb), lambda j, i: (i, j)), pl.BlockSpec((HALO, cb), pm(0)),
                  pl.BlockSpec((HALO, cb), nm(0)),
                  pl.BlockSpec((tm, cb), lambda j, i: (i, j + nf)), pl.BlockSpec((HALO, cb), pm(nf)),
                  pl.BlockSpec((HALO, cb), nm(nf)),
                  pl.BlockSpec((kf, cb), lambda j, i: (0, j)), pl.BlockSpec((kf, cb), lambda j, i: (0, j + nf))],
        out_specs=[pl.BlockSpec((2, tm, cb), lambda j, i: (0, i, j)), pl.BlockSpec((2, 8, cb), lambda j, i: (0, 0, j))],
        out_shape=[jax.ShapeDtypeStruct((2, s, f), BF16), jax.ShapeDtypeStruct((2, 8, f), F32)],
        compiler_params=_cp(("parallel", "arbitrary")))(da, da, u_pre, u_pre, u_pre, u_pre, u_pre, u_pre, conv_w, conv_w)
    (du, dw), got = outs if carry is not None else (outs, [])
    return du, dw[:, :kf], got


def _adamw(w, g, m, v, name):
    shape = w.shape
    c = shape[-1]
    r = w.size // c
    tr = r if r * c <= 262144 else _tile(r, tuple(t for t in (512, 256, 128, 64, 32, 16, 8) if t * c <= 262144))
    bc1 = 1.0 - ADAM_B1 ** ADAM_STEP
    bc2 = 1.0 - ADAM_B2 ** ADAM_STEP

    def body(w_ref, g_ref, m_ref, v_ref, d_ref, nm_ref, nv_ref):
        gv = g_ref[...]
        mn = ADAM_B1 * m_ref[...] + (1.0 - ADAM_B1) * gv
        vn = ADAM_B2 * v_ref[...] + (1.0 - ADAM_B2) * (gv * gv)
        nm_ref[...] = mn
        nv_ref[...] = vn
        d_ref[...] = -ADAM_LR * ((mn / bc1) / (jnp.sqrt(vn / bc2) + ADAM_EPS) + ADAM_WD * w_ref[...])

    blk = pl.BlockSpec((tr, c), lambda i: (i, 0))
    outs = _pc(body, name=name, grid=(r // tr,), in_specs=[blk] * 4, out_specs=[blk] * 3,
               out_shape=[jax.ShapeDtypeStruct((r, c), F32)] * 3,
               compiler_params=_cp(("parallel",)))(*(t.reshape(r, c) for t in (w, g, m, v)))
    return tuple(o.reshape(shape) for o in outs)


def _adamw_sharded(w, m, v, partials, name):
    nl, r, c = w.shape
    tr = _tile(r, tuple(t for t in (256, 128, 64, 32, 16, 8) if t * c <= 131072))
    bc1 = 1.0 - ADAM_B1 ** ADAM_STEP
    bc2 = 1.0 - ADAM_B2 ** ADAM_STEP

    def body(w_ref, m_ref, v_ref, *rest):
        p_refs, (g_ref, d_ref, nm_ref, nv_ref) = rest[:nl], rest[nl:]
        layer = pl.program_id(0)
        for l in range(nl):
            @pl.when(layer == l)
            def _(p_ref=p_refs[l]):
                gv = p_ref[0].astype(F32)
                for dev in range(1, N_DEV):
                    gv = gv + p_ref[dev].astype(F32)
                mn = ADAM_B1 * m_ref[...] + (1.0 - ADAM_B1) * gv
                vn = ADAM_B2 * v_ref[...] + (1.0 - ADAM_B2) * (gv * gv)
                g_ref[...] = gv
                nm_ref[...] = mn
                nv_ref[...] = vn
                d_ref[...] = -ADAM_LR * ((mn / bc1) / (jnp.sqrt(vn / bc2) + ADAM_EPS) + ADAM_WD * w_ref[...])

    blk = pl.BlockSpec((None, tr, c), lambda l, i: (l, i, 0))
    p_specs = [pl.BlockSpec((N_DEV, tr, c), lambda l, i, k=k: (0, jnp.where(l == k, i, 0), 0)) for k in range(nl)]
    return _pc(body, name=name, grid=(nl, r // tr), in_specs=[blk] * 3 + p_specs, out_specs=[blk] * 4,
               out_shape=[jax.ShapeDtypeStruct((nl, r, c), F32)] * 4,
               compiler_params=_cp(("arbitrary", "arbitrary")))(w, m, v, *partials)


def _slot_sum(x, name):
    _, r, c = x.shape
    tr = _tile(r, (512, 256, 128, 64, 32, 16, 8))

    def body(x_ref, o_ref):
        acc = x_ref[0].astype(F32)
        for d in range(1, N_DEV):
            acc = acc + x_ref[d].astype(F32)
        o_ref[...] = acc

    return _pc(body, name=name, grid=(r // tr,), in_specs=[pl.BlockSpec((N_DEV, tr, c), lambda i: (0, i, 0))],
               out_specs=pl.BlockSpec((tr, c), lambda i: (i, 0)), out_shape=jax.ShapeDtypeStruct((r, c), F32),
               compiler_params=_cp(("parallel",)))(x)


def _all_gather(xs, name, layers=None):
    na = len(xs)
    layers = tuple(range(xs[0].shape[0])) if layers is None else layers
    nl = len(layers) * na

    def body(*refs):
        x_refs, out_refs = refs[:na], refs[na:na + nl]
        send_sems, recv_sems, local_sems = refs[na + nl:]
        mx, my, mc = lax.axis_index("x"), lax.axis_index("y"), lax.axis_index("c")
        me, sibling = (mx, my, mc), (mx, my, 1 - mc)
        chips = [(1 - mx, my), (mx, 1 - my), (1 - mx, 1 - my)]

        def local(l):
            return x_refs[l % na].at[layers[l // na]]

        def slot(l, dev):
            return out_refs[l].at[4 * dev[0] + 2 * dev[1] + dev[2]]

        def copy(l, k, block, to, src=None):
            return pltpu.make_async_remote_copy(
                src_ref=slot(l, block) if src is None else src, dst_ref=slot(l, block),
                send_sem=send_sems.at[l, k], recv_sem=recv_sems.at[l, k], device_id=to, device_id_type=MESH)

        mine = [pltpu.make_async_copy(local(l), slot(l, me), local_sems.at[l]) for l in range(nl)]
        first = []
        for l in range(nl):
            mine[l].start()
            first.append(copy(l, 0, me, sibling, src=local(l)))
            first += [copy(l, 1 + j, me, (*chip, mc), src=local(l)) for j, chip in enumerate(chips)]
        for cp in first:
            cp.start()
        passed = []
        for l in range(nl):
            for j, chip in enumerate(chips):
                copy(l, 1 + j, (*chip, mc), me).wait_recv()
                fw = copy(l, 4 + j, (*chip, mc), sibling)
                fw.start()
                passed.append(fw)
        for l in range(nl):
            copy(l, 0, sibling, me).wait_recv()
            for j, chip in enumerate(chips):
                copy(l, 4 + j, (*chip, 1 - mc), me).wait_recv()
        for cp in first + passed:
            cp.wait_send()
        for l in range(nl):
            mine[l].wait()

    anyspec = pl.BlockSpec(memory_space=pl.ANY)
    outs = _pc(body, name=name, in_specs=[anyspec] * na, out_specs=[anyspec] * nl,
               out_shape=[jax.ShapeDtypeStruct((N_DEV,) + xs[l % na].shape[1:], xs[l % na].dtype) for l in range(nl)],
               scratch_shapes=[pltpu.SemaphoreType.DMA((nl, 7)), pltpu.SemaphoreType.DMA((nl, 7)),
                               pltpu.SemaphoreType.DMA((nl,))],
               compiler_params=_cp())(*xs)
    return [outs[l * na:(l + 1) * na] for l in range(nl // na)]


def _mix_in_assemble(shards, width, nh):
    _, d, cs = shards.shape
    tr = _tile(d, (128, 64, 32, 16))
    w4 = 4 * width

    def body(s_ref, main_ref, gate_ref):
        full = jnp.concatenate([s_ref[j].astype(F32) for j in range(N_DEV)], axis=1)
        main_ref[:, :width] = full[:, 3 * width:w4].astype(BF16)
        main_ref[:, width:w4] = full[:, w4 + 2 * nh:].astype(BF16)
        main_ref[:, w4:] = full[:, :3 * width].astype(BF16)
        gate_ref[...] = jnp.concatenate([full[:, w4:w4 + 2 * nh], jnp.zeros((tr, 128 - 2 * nh), F32)],
                                        axis=1).astype(BF16)

    return _pc(body, name="mix_in_assemble", grid=(d // tr,),
               in_specs=[pl.BlockSpec((N_DEV, tr, cs), lambda i: (0, i, 0))],
               out_specs=[pl.BlockSpec((tr, 7 * width), lambda i: (i, 0)), pl.BlockSpec((tr, 128), lambda i: (i, 0))],
               out_shape=[jax.ShapeDtypeStruct((d, 7 * width), BF16), jax.ShapeDtypeStruct((d, 128), BF16)],
               compiler_params=_cp(("parallel",)))(shards)


def _mix_in_shards(dmain, dgate, width, nh):
    d = dmain.shape[0]
    cs = (7 * width + 2 * nh) // N_DEV
    tr = _tile(d, (128, 64, 32, 16))
    w4 = 4 * width

    def body(main_ref, gate_ref, o_ref):
        full = jnp.concatenate([main_ref[:, w4:], main_ref[:, :width], gate_ref[:, :2 * nh], main_ref[:, width:w4]],
                               axis=1)
        for j in range(N_DEV):
            o_ref[j] = full[:, j * cs:(j + 1) * cs].astype(BF16)

    return _pc(body, name="mix_in_shards", grid=(d // tr,),
               in_specs=[pl.BlockSpec((tr, 7 * width), lambda i: (i, 0)), pl.BlockSpec((tr, 128), lambda i: (i, 0))],
               out_specs=pl.BlockSpec((N_DEV, tr, cs), lambda i: (0, i, 0)),
               out_shape=jax.ShapeDtypeStruct((N_DEV, d, cs), BF16),
               compiler_params=_cp(("parallel",)))(dmain, dgate)


PACK_COLS = 1024
BIG = ("w_mix_in", "w_mix_out", "w_xq", "w_xk", "w_xv", "w_xo", "w_ffn_up", "w_ffn_down")


def _pack(blocks, lead):
    flat = jnp.concatenate(blocks, axis=-1)
    n = flat.shape[-1]
    per = 16 * PACK_COLS
    pad = (-n) % per
    flat = jnp.pad(flat, [(0, 0)] * len(lead) + [(0, pad)])
    return flat.reshape(*lead, (n + pad) // PACK_COLS, PACK_COLS)


def _unpack(packed, sizes):
    flat = packed.reshape(*packed.shape[:-2], -1)
    out, off = [], 0
    for n in sizes:
        out.append(flat[..., off:off + n])
        off += n
    return out


def kernel(x, mem, mix_norm, w_mix_in, gdn_conv, gdn_a_log, gdn_dt_bias, gdn_out_norm, sc_conv, w_mix_out, xattn_norm, mem_norm, w_xq, w_xk, w_xv, w_xo, ffn_norm, w_ffn_up, ffn_conv, w_ffn_down, final_norm, loss_target, m_mix_norm, m_w_mix_in, m_gdn_conv, m_gdn_a_log, m_gdn_dt_bias, m_gdn_out_norm, m_sc_conv, m_w_mix_out, m_xattn_norm, m_mem_norm, m_w_xq, m_w_xk, m_w_xv, m_w_xo, m_ffn_norm, m_w_ffn_up, m_ffn_conv, m_w_ffn_down, m_final_norm, v_mix_norm, v_w_mix_in, v_gdn_conv, v_gdn_a_log, v_gdn_dt_bias, v_gdn_out_norm, v_sc_conv, v_w_mix_out, v_xattn_norm, v_mem_norm, v_w_xq, v_w_xk, v_w_xv, v_w_xo, v_ffn_norm, v_w_ffn_up, v_ffn_conv, v_w_ffn_down, v_final_norm):
    names = ["mix_norm", "w_mix_in", "gdn_conv", "gdn_a_log", "gdn_dt_bias", "gdn_out_norm", "sc_conv", "w_mix_out",
             "xattn_norm", "mem_norm", "w_xq", "w_xk", "w_xv", "w_xo", "ffn_norm", "w_ffn_up", "ffn_conv",
             "w_ffn_down", "final_norm"]
    wts = dict(zip(names, (mix_norm, w_mix_in, gdn_conv, gdn_a_log, gdn_dt_bias, gdn_out_norm, sc_conv, w_mix_out,
                           xattn_norm, mem_norm, w_xq, w_xk, w_xv, w_xo, ffn_norm, w_ffn_up, ffn_conv, w_ffn_down,
                           final_norm)))
    mom1 = dict(zip(names, (m_mix_norm, m_w_mix_in, m_gdn_conv, m_gdn_a_log, m_gdn_dt_bias, m_gdn_out_norm, m_sc_conv,
                            m_w_mix_out, m_xattn_norm, m_mem_norm, m_w_xq, m_w_xk, m_w_xv, m_w_xo, m_ffn_norm,
                            m_w_ffn_up, m_ffn_conv, m_w_ffn_down, m_final_norm)))
    mom2 = dict(zip(names, (v_mix_norm, v_w_mix_in, v_gdn_conv, v_gdn_a_log, v_gdn_dt_bias, v_gdn_out_norm, v_sc_conv,
                            v_w_mix_out, v_xattn_norm, v_mem_norm, v_w_xq, v_w_xk, v_w_xv, v_w_xo, v_ffn_norm,
                            v_w_ffn_up, v_ffn_conv, v_w_ffn_down, v_final_norm)))

    x0 = x[0]
    memv = mem[0]
    target = loss_target[0]
    s, d = x0.shape
    depth = mix_norm.shape[0]
    width = d // 2
    nh = width // HEAD
    me = 4 * lax.axis_index("x") + 2 * lax.axis_index("y") + lax.axis_index("c")

    local_bf16 = {n: wts[n].astype(BF16) for n in BIG}
    gathered = [None] * depth
    gathered[0] = dict(zip(BIG, _all_gather([local_bf16[n] for n in BIG], "all_gather_weights", layers=(0,))[0]))

    def gather_next(l, group):
        return None if l + 1 >= depth else ("gather", [local_bf16[n] for n in group], l + 1)

    def mm_gathering(l, group, *args, **kw):
        carry = gather_next(l, group)
        if carry is None:
            return _mm(*args, **kw)
        out, got = _mm(*args, carry=carry, **kw)
        gathered[l + 1].update(zip(group, got))
        return out

    conv_names = ("gdn_conv", "sc_conv", "ffn_conv")
    conv_local = _pack([wts[n].reshape(1, -1) for n in conv_names], (1,))
    conv_all = _all_gather([conv_local], "all_gather_conv")[0][0]
    conv_parts = _unpack(conv_all, [wts[n].size for n in conv_names])
    conv_full = {}
    for n, part in zip(conv_names, conv_parts):
        _, kt, cs = wts[n].shape
        conv_full[n] = jnp.moveaxis(part.reshape(N_DEV, depth, kt, cs), 0, 2).reshape(depth, kt, N_DEV * cs)

    def layer_weights(l):
        wl = {n: gathered[l][n].reshape(-1, gathered[l][n].shape[-1]) for n in BIG if n not in ("w_mix_in", "w_ffn_up")}
        wl["w_ffn_up"] = gathered[l]["w_ffn_up"]
        return wl

    def gate_params(l):
        rows = jnp.stack([gdn_a_log[l], gdn_dt_bias[l]])
        return jnp.pad(rows, ((0, 6), (nh, 128 - 2 * nh)))

    saved = []
    xc = x0
    for l in range(depth):
        wl = layer_weights(l)
        w_main, w_gate = _mix_in_assemble(gathered[l]["w_mix_in"], width, nh)
        gp = gate_params(l)
        if l + 1 < depth:
            gathered[l + 1] = {}
        h1 = _rms_fwd(xc, mix_norm[l], "rms_mix")
        proj = mm_gathering(l, ("w_mix_in",), h1, w_main, out_dtype=BF16, name="mm_mix_in")
        ba = _mm(h1, w_gate, out_dtype=F32, name="mm_mix_gates")
        qkv = _gdn_prep_fwd(proj, conv_full["gdn_conv"][l], width)
        with_gdn = ("w_mix_out", "w_xq")
        o_raw, states, tinv, got = _gdn_fwd(qkv, ba, gp, width, carry=gather_next(l, with_gdn))
        if got:
            gathered[l + 1].update(zip(with_gdn, got))
        y = _mix_post_fwd(o_raw, proj, gdn_out_norm[l], conv_full["sc_conv"][l], width)
        x1 = mm_gathering(l, ("w_xk",), y, wl["w_mix_out"], res=xc, name="mm_mix_out")
        h2 = _rms_fwd(x1, xattn_norm[l], "rms_xattn")
        mem_n = _rms_fwd(memv, mem_norm[l], "rms_mem")
        qx = mm_gathering(l, ("w_xv",), h2, wl["w_xq"], out_dtype=BF16, name="mm_xq")
        kx = _mm(mem_n, wl["w_xk"], out_dtype=BF16, name="mm_xk")
        vx = _mm(mem_n, wl["w_xv"], out_dtype=BF16, name="mm_xv")
        ox = _xattn_fwd(qx, kx, vx)
        x2 = mm_gathering(l, ("w_xo",), ox, wl["w_xo"], res=x1, name="mm_xo")
        h3 = _rms_fwd(x2, ffn_norm[l], "rms_ffn")
        u_pre = mm_gathering(l, ("w_ffn_up",), h3, wl["w_ffn_up"], b_shards=True, out_dtype=BF16, name="mm_ffn_up")
        act = _ffn_act_fwd(u_pre, conv_full["ffn_conv"][l])
        x3 = mm_gathering(l, ("w_ffn_down",), act, wl["w_ffn_down"], res=x2, name="mm_ffn_down")
        saved.append(dict(x0=xc, x1=x1, x2=x2, h1=h1, h2=h2, h3=h3, proj=proj, ba=ba, qkv=qkv, o_raw=o_raw,
                          states=states, tinv=tinv, y=y, mem_n=mem_n, qx=qx, kx=kx, vx=vx, ox=ox, u_pre=u_pre, act=act,
                          w_main=w_main, w_gate=w_gate, gp=gp, wl=wl))
        xc = x3

    loss_part, dx, dxb, g_final = _loss_head(xc, final_norm, target)

    small = {n: [None] * depth for n in ("mix_norm", "xattn_norm", "mem_norm", "ffn_norm", "gdn_a_log", "gdn_dt_bias",
                                          "gdn_out_norm", "gdn_conv", "sc_conv", "ffn_conv")}
    partials = {n: [None] * depth for n in BIG}

    def scatter(big, group):
        return ("scatter", [big[n].reshape((N_DEV,) + wts[n].shape[1:]) for n in group], None)

    for l in reversed(range(depth)):
        sv = saved[l]
        wl, w_main, w_gate, gp = sv["wl"], sv["w_main"], sv["w_gate"], sv["gp"]
        big = {}
        d_act = _mm(dxb, wl["w_ffn_down"], tb=True, out_dtype=BF16, name="mm_d_act")
        big["w_ffn_down"] = _mm(sv["act"], dxb, ta=True, out_dtype=BF16, name="mm_dw_ffn_down")
        du, dcw, got = _ffn_act_bwd(d_act, sv["u_pre"], conv_full["ffn_conv"][l],
                                    carry=None if l + 1 >= depth else ("scatter", [ffn_up_above], None))
        if got:
            partials["w_ffn_up"][l + 1] = got[0]
        small["ffn_conv"][l] = jnp.concatenate([dcw[0], dcw[1]], axis=1)
        big["w_ffn_up"], (partials["w_ffn_down"][l],) = _mm(
            sv["h3"], du, ta=True, b_halves=True, out_shards=True, out_dtype=BF16, name="mm_dw_ffn_up",
            carry=scatter(big, ("w_ffn_down",)))
        if l > 0:
            ffn_up_above = big["w_ffn_up"]
            dh = _mm(du, wl["w_ffn_up"], tb=True, a_halves=True, b_shards=True, out_dtype=BF16, name="mm_dh3")
        else:
            dh, (partials["w_ffn_up"][l],) = _mm(du, wl["w_ffn_up"], tb=True, a_halves=True, b_shards=True,
                                                 out_dtype=BF16, name="mm_dh3", carry=scatter(big, ("w_ffn_up",)))
        dx, dxb, small["ffn_norm"][l], _ = _rms_bwd(sv["x2"], dh, ffn_norm[l], dx, "rms_bwd_ffn")
        d_ox = _mm(dxb, wl["w_xo"], tb=True, out_dtype=BF16, name="mm_d_ox")
        big["w_xo"] = _mm(sv["ox"], dxb, ta=True, out_dtype=BF16, name="mm_dw_xo")
        d_qx, d_kx, d_vx, (partials["w_xo"][l],) = _xattn_bwd(sv["qx"], sv["kx"], sv["vx"], d_ox,
                                                               carry=scatter(big, ("w_xo",)))
        big["w_xq"] = _mm(sv["h2"], d_qx, ta=True, out_dtype=BF16, name="mm_dw_xq")
        big["w_xk"] = _mm(sv["mem_n"], d_kx, ta=True, out_dtype=BF16, name="mm_dw_xk")
        big["w_xv"] = _mm(sv["mem_n"], d_vx, ta=True, out_dtype=BF16, name="mm_dw_xv")
        dh, (partials["w_xq"][l],) = _mm(d_qx, wl["w_xq"], tb=True, out_dtype=BF16, name="mm_dh2",
                                         carry=scatter(big, ("w_xq",)))
        d_mem = _mm(d_kx, wl["w_xk"], tb=True, out_dtype=F32, name="mm_dmem_k")
        d_mem = _mm(d_vx, wl["w_xv"], tb=True, out_dtype=F32, res=d_mem, name="mm_dmem_v")
        _, _, small["mem_norm"][l], _ = _rms_bwd(memv, d_mem, mem_norm[l], jnp.zeros_like(memv), "rms_bwd_mem")
        dx, dxb, small["xattn_norm"][l], (partials["w_xk"][l],) = _rms_bwd(
            sv["x1"], dh, xattn_norm[l], dx, "rms_bwd_xattn", carry=scatter(big, ("w_xk",)))
        d_y, (partials["w_xv"][l],) = _mm(dxb, wl["w_mix_out"], tb=True, out_dtype=BF16, name="mm_d_y",
                                          carry=scatter(big, ("w_xv",)))
        big["w_mix_out"] = _mm(sv["y"], dxb, ta=True, out_dtype=BF16, name="mm_dw_mix_out")
        d_o, d_proj, small["gdn_out_norm"][l], small["sc_conv"][l], (partials["w_mix_out"][l],) = _mix_post_bwd(
            d_y, sv["o_raw"], sv["proj"], gdn_out_norm[l], conv_full["sc_conv"][l], width,
            carry=scatter(big, ("w_mix_out",)))
        d_qkv, d_gb, got = _gdn_bwd(sv["qkv"], sv["ba"], gp, d_o, sv["states"], sv["tinv"], width,
                                    carry=None if l + 1 >= depth else ("scatter", [mix_in_above], None))
        if got:
            partials["w_mix_in"][l + 1] = got[0]
        d_ba, d_gp = _gates_bwd(sv["ba"], d_gb, gp, nh)
        small["gdn_a_log"][l] = d_gp[0, nh:2 * nh]
        small["gdn_dt_bias"][l] = d_gp[1, nh:2 * nh]
        d_proj, small["gdn_conv"][l] = _gdn_prep_bwd(d_qkv, sv["proj"], conv_full["gdn_conv"][l], d_proj, width)
        dw_main = _mm(sv["h1"], d_proj, ta=True, name="mm_dw_mix_in")
        dw_gate = _mm(sv["h1"], d_ba, ta=True, name="mm_dw_mix_gates")
        big["w_mix_in"] = _mix_in_shards(dw_main, dw_gate, width, nh)
        if l > 0:
            mix_in_above = big["w_mix_in"]
            dh = _mm(d_proj, w_main, tb=True, out_dtype=F32, name="mm_dh1_main")
        else:
            dh, (partials["w_mix_in"][l],) = _mm(d_proj, w_main, tb=True, out_dtype=F32, name="mm_dh1_main",
                                                 carry=scatter(big, ("w_mix_in",)))
        dh = _mm(d_ba, w_gate, tb=True, out_dtype=BF16, res=dh, name="mm_dh1_gates")
        dx, dxb, small["mix_norm"][l], _ = _rms_bwd(sv["x0"], dh, mix_norm[l], dx, "rms_bwd_mix")

    small_names = ("mix_norm", "xattn_norm", "mem_norm", "ffn_norm", "gdn_a_log", "gdn_dt_bias", "gdn_out_norm",
                   "gdn_conv", "sc_conv", "ffn_conv")
    small_parts = [jnp.stack(small[n]).reshape(1, -1) for n in small_names]
    small_parts += [g_final.reshape(1, -1), loss_part.reshape(1, 1)]
    small_sizes = [p.shape[1] for p in small_parts]
    small_local = _pack(small_parts, (1,))
    small_sum = _slot_sum(_all_gather([small_local], "all_gather_small")[0][0], "sum_small")
    small_tot = _unpack(small_sum, small_sizes)
    grads = {}
    for n, g in zip(small_names, small_tot[:len(small_names)]):
        if n in conv_names:
            _, kt, cs = wts[n].shape
            g = lax.dynamic_slice_in_dim(g.reshape(depth, kt, N_DEV * cs), me * cs, cs, axis=2)
        grads[n] = g.reshape(wts[n].shape)
    grads["final_norm"] = small_tot[-2].reshape(final_norm.shape)
    loss = small_tot[-1].reshape(())

    delta, new_m, new_v = {}, {}, {}
    for n in names:
        if n in BIG:
            grads[n], delta[n], new_m[n], new_v[n] = _adamw_sharded(wts[n], mom1[n], mom2[n], partials[n], "adamw_" + n)
        else:
            delta[n], new_m[n], new_v[n] = _adamw(wts[n], grads[n], mom1[n], mom2[n], "adamw_" + n)
    return (loss, dx[None], *[grads[n] for n in names], *[delta[n] for n in names],
            *[new_m[n] for n in names], *[new_v[n] for n in names])
```

```python
import functools

import jax
import jax.numpy as jnp
from jax import lax
from jax.experimental import pallas as pl
from jax.experimental.pallas import tpu as pltpu

F32 = jnp.float32
BF16 = jnp.bfloat16
CHUNK = 64
HEAD = 128
XHEADS = 4
GDN_K = 4
EPS = 1e-6
HALO = 16
QKV_COL = 4
N_DEV = 8
VMEM_LIMIT = 56 * 1024 * 1024
ADAM_LR, ADAM_B1, ADAM_B2, ADAM_EPS, ADAM_WD, ADAM_STEP = 0.001, 0.9, 0.999, 1e-08, 0.01, 10
MESH = pl.DeviceIdType.MESH
MM_TILES_MN = (1408, 1024, 512, 256, 128)
MM_TILES_K = (2048, 1408, 1024, 512, 256, 128)
ACT_TILES = (1408, 512, 256, 128)


def _call(body, **kw):
    return pl.pallas_call(body, **kw)


def _pc(body, *, carry=None, **kw):
    if carry is None:
        return _call(body, **kw)
    c_args, c_in, c_shapes, c_out, c_sems = _carry_parts(carry)
    nca = len(c_args)
    grid = kw["grid"]
    single = not isinstance(kw["out_shape"], (list, tuple))
    out_shape = [kw["out_shape"]] if single else list(kw["out_shape"])
    out_specs = [kw["out_specs"]] if single else list(kw["out_specs"])
    n_in, n_out = len(kw["in_specs"]), len(out_shape)

    def wrapped(*refs):
        ins, srcs = refs[:n_in], refs[n_in:n_in + nca]
        outs = refs[n_in + nca:n_in + nca + n_out]
        dsts = refs[n_in + nca + n_out:n_in + 2 * nca + n_out]
        scratch = refs[n_in + 2 * nca + n_out:]
        ids = [pl.program_id(ax) for ax in range(len(grid))]
        first = functools.reduce(lambda p, q: p & q, [i == 0 for i in ids])
        last = functools.reduce(lambda p, q: p & q, [i == g - 1 for i, g in zip(ids, grid)])
        _carry_run(carry, srcs, dsts, scratch[-2], scratch[-1], first, last)
        body(*ins, *outs, *scratch[:-2])

    kw = dict(kw, in_specs=list(kw["in_specs"]) + c_in, out_specs=out_specs + c_out, out_shape=out_shape + c_shapes,
              scratch_shapes=list(kw.get("scratch_shapes", [])) + c_sems,
              compiler_params=_cp(("arbitrary",) * len(grid)))
    call = _call(wrapped, **kw)

    def run(*args):
        res = call(*args, *c_args)
        return (res[0] if single else tuple(res[:n_out])), list(res[n_out:])

    return run


def _cp(sem=None, **kw):
    if sem is not None:
        kw["dimension_semantics"] = sem
    return pltpu.CompilerParams(vmem_limit_bytes=VMEM_LIMIT, **kw)


def _tile(n, cands):
    for c in cands:
        if n % c == 0:
            return c
    return n


def _sigmoid(x):
    return 0.5 * jnp.tanh(0.5 * x) + 0.5

def _softplus(x):
    return jnp.maximum(x, 0.0) + jnp.log(1.0 + jnp.exp(-jnp.abs(x)))


def _mmb(a, b):
    return jnp.dot(a.astype(BF16), b.astype(BF16), preferred_element_type=F32)


def _mmb_nt(a, b):
    return lax.dot_general(a.astype(BF16), b.astype(BF16), (((1,), (1,)), ((), ())), preferred_element_type=F32)


def _mmb_tn(a, b):
    return lax.dot_general(a.astype(BF16), b.astype(BF16), (((0,), (0,)), ((), ())), preferred_element_type=F32)


def _split(x):
    hi = x.astype(BF16)
    return hi, x - hi.astype(F32)


def _mm3(a, b):
    ah, ar = _split(a)
    bh, br = _split(b)
    al, bl = ar.astype(BF16), br.astype(BF16)
    return (jnp.dot(ah, bh, preferred_element_type=F32)
            + (jnp.dot(ah, bl, preferred_element_type=F32) + jnp.dot(al, bh, preferred_element_type=F32)))


def _mm_exact_lhs(a, b):
    ab = a.astype(BF16)
    b1, r1 = _split(b)
    b2, r2 = _split(r1)
    return (jnp.dot(ab, b1, preferred_element_type=F32)
            + (jnp.dot(ab, b2, preferred_element_type=F32) + jnp.dot(ab, r2.astype(BF16), preferred_element_type=F32)))


def _shift_down(cur, prev, s):
    if s == 0:
        return cur
    r = pltpu.roll(cur, s, 0)
    p = pltpu.roll(prev, s, 0)
    rows = lax.broadcasted_iota(jnp.int32, prev.shape, 0)
    first = jnp.where(rows < s, p, r[:HALO])
    return jnp.concatenate([first, r[HALO:]], axis=0)


def _shift_up(ext, s, tm):
    if s == 0:
        return ext[:tm]
    return pltpu.roll(ext, ext.shape[0] - s, 0)[:tm]


def _prev_map(tm, col):
    return lambda i, j: (jnp.maximum(i * (tm // HALO) - 1, 0), col(j))


def _next_map(tm, nrows, col):
    return lambda i, j: (jnp.minimum((i + 1) * (tm // HALO), nrows // HALO - 1), col(j))


def _carry_parts(carry):
    if carry is None:
        return [], [], [], [], []
    _, srcs, _ = carry
    na = len(srcs)
    anyspec = pl.BlockSpec(memory_space=pl.ANY)
    return (list(srcs), [anyspec] * na, [jax.ShapeDtypeStruct((N_DEV,) + s.shape[1:], s.dtype) for s in srcs],
            [anyspec] * na, [pltpu.SemaphoreType.DMA((na, N_DEV)), pltpu.SemaphoreType.DMA((na, N_DEV))])


def _carry_run(carry, src_refs, dst_refs, send_sems, recv_sems, first, last):
    if carry is None:
        return
    kind, _, layer = carry
    mx, my, mc = lax.axis_index("x"), lax.axis_index("y"), lax.axis_index("c")
    me = 4 * mx + 2 * my + mc

    def descriptors(with_recvs):
        sends, recvs = [], []
        for a in range(len(dst_refs)):
            for k in range(N_DEV):
                px, py, pc = mx ^ (k >> 2), my ^ ((k >> 1) & 1), mc ^ (k & 1)
                peer = 4 * px + 2 * py + pc
                src = src_refs[a].at[peer] if kind == "scatter" else src_refs[a].at[layer]
                if k == 0:
                    sends.append(pltpu.make_async_copy(src, dst_refs[a].at[me], send_sems.at[a, 0]))
                    continue
                sends.append(pltpu.make_async_remote_copy(
                    src_ref=src, dst_ref=dst_refs[a].at[me], send_sem=send_sems.at[a, k], recv_sem=recv_sems.at[a, k],
                    device_id=(px, py, pc), device_id_type=MESH))
                if with_recvs:
                    recvs.append(pltpu.make_async_remote_copy(
                        src_ref=src, dst_ref=dst_refs[a].at[peer], send_sem=send_sems.at[a, k],
                        recv_sem=recv_sems.at[a, k], device_id=(mx, my, mc), device_id_type=MESH))
        return sends, recvs

    @pl.when(first)
    def _():
        for cp in descriptors(False)[0]:
            cp.start()

    @pl.when(last)
    def _():
        sends, recvs = descriptors(True)
        for cp in recvs:
            cp.wait_recv()
        for i, cp in enumerate(sends):
            if i % N_DEV == 0:
                cp.wait()
            else:
                cp.wait_send()


def _mm(a, b, *, ta=False, tb=False, out_dtype=F32, res=None, name, a_halves=False, b_shards=False,
        b_halves=False, out_shards=False, carry=None):
    if a_halves:
        m, k = a.shape[1], 2 * a.shape[2]
    else:
        m, k = (a.shape[1], a.shape[0]) if ta else a.shape
    if b_shards:
        cs = b.shape[2]
        n = b.shape[1] if tb else N_DEV * cs
    elif b_halves:
        n = 2 * b.shape[2]
        cs = n // N_DEV
    else:
        n = b.shape[0] if tb else b.shape[1]
        cs = None
    tm = _tile(m, MM_TILES_MN)
    tn = cs if (cs is not None and not tb) else _tile(n, MM_TILES_MN)
    tk = cs if (b_shards and tb) else _tile(k, MM_TILES_K)
    nk = k // tk
    dn = (((0 if ta else 1,), (1 if tb else 0,)), ((), ()))

    def body(a_ref, b_ref, *rest):
        if res is None:
            o_ref, acc = rest
        else:
            r_ref, o_ref, acc = rest
        kk = pl.program_id(2)
        part = lax.dot_general(a_ref[...].astype(BF16), b_ref[...].astype(BF16), dn, preferred_element_type=F32)

        def finish(total):
            if res is not None:
                total = total + r_ref[...].astype(F32)
            o_ref[...] = total.astype(out_dtype)

        if nk == 1:
            finish(part)
            return

        @pl.when(kk == 0)
        def _():
            acc[...] = part

        @pl.when((kk > 0) & (kk < nk - 1))
        def _():
            acc[...] += part

        @pl.when(kk == nk - 1)
        def _():
            finish(acc[...] + part)

    if a_halves:
        per = (k // 2) // tk
        a_spec = pl.BlockSpec((None, tm, tk), lambda i, j, kk: (kk // per, i, kk % per))
    elif ta:
        a_spec = pl.BlockSpec((tk, tm), lambda i, j, kk: (kk, i))
    else:
        a_spec = pl.BlockSpec((tm, tk), lambda i, j, kk: (i, kk))
    if b_shards and tb:
        b_spec = pl.BlockSpec((None, tn, tk), lambda i, j, kk: (kk, j, 0))
    elif b_shards:
        b_spec = pl.BlockSpec((None, tk, tn), lambda i, j, kk: (j, kk, 0))
    elif b_halves:
        perb = (n // 2) // tn
        b_spec = pl.BlockSpec((None, tk, tn), lambda i, j, kk: (j // perb, kk, j % perb))
    elif tb:
        b_spec = pl.BlockSpec((tn, tk), lambda i, j, kk: (j, kk))
    else:
        b_spec = pl.BlockSpec((tk, tn), lambda i, j, kk: (kk, j))
    if out_shards:
        o_spec = pl.BlockSpec((None, tm, tn), lambda i, j, kk: (j, i, 0))
        o_shape = jax.ShapeDtypeStruct((N_DEV, m, tn), out_dtype)
    else:
        o_spec = pl.BlockSpec((tm, tn), lambda i, j, kk: (i, j))
        o_shape = jax.ShapeDtypeStruct((m, n), out_dtype)
    in_specs = [a_spec, b_spec] + ([o_spec] if res is not None else [])
    args = (a, b) + ((res,) if res is not None else ())
    return _pc(body, carry=carry, name=name, grid=(m // tm, n // tn, nk), in_specs=in_specs, out_specs=o_spec,
               out_shape=o_shape, scratch_shapes=[pltpu.VMEM((tm, tn), F32)],
               compiler_params=_cp(("parallel", "parallel", "arbitrary")))(*args)


def _rms_fwd(x, w, name):
    s, d = x.shape
    tm = _tile(s, (512, 256, 128, 64))

    def body(x_ref, w_ref, o_ref):
        xv = x_ref[...]
        r = lax.rsqrt(jnp.mean(xv * xv, axis=-1, keepdims=True) + EPS)
        o_ref[...] = (xv * r * w_ref[...]).astype(BF16)

    return _pc(body, name=name, grid=(s // tm,),
               in_specs=[pl.BlockSpec((tm, d), lambda i: (i, 0)), pl.BlockSpec((1, d), lambda i: (0, 0))],
               out_specs=pl.BlockSpec((tm, d), lambda i: (i, 0)), out_shape=jax.ShapeDtypeStruct((s, d), BF16),
               compiler_params=_cp(("parallel",)))(x, w.reshape(1, d))


def _rms_bwd(x, dh, w, dx_in, name, carry=None):
    s, d = x.shape
    tm = _tile(s, (256, 128, 64))

    def body(x_ref, dh_ref, w_ref, dxi_ref, dx_ref, dxb_ref, dg_ref):
        @pl.when(pl.program_id(0) == 0)
        def _():
            dg_ref[...] = jnp.zeros_like(dg_ref)

        xv = x_ref[...]
        dy = dh_ref[...].astype(F32)
        r = lax.rsqrt(jnp.mean(xv * xv, axis=-1, keepdims=True) + EPS)
        xh = xv * r
        dxh = dy * w_ref[...]
        dx = dxi_ref[...] + r * (dxh - xh * jnp.mean(dxh * xh, axis=-1, keepdims=True))
        dx_ref[...] = dx
        dxb_ref[...] = dx.astype(BF16)
        dg_ref[0:1, :] += jnp.sum(dy * xh, axis=0, keepdims=True)

    row = pl.BlockSpec((tm, d), lambda i: (i, 0))
    outs = _pc(body, carry=carry, name=name, grid=(s // tm,),
               in_specs=[row, row, pl.BlockSpec((1, d), lambda i: (0, 0)), row],
               out_specs=[row, row, pl.BlockSpec((8, d), lambda i: (0, 0))],
               out_shape=[jax.ShapeDtypeStruct((s, d), F32), jax.ShapeDtypeStruct((s, d), BF16),
                          jax.ShapeDtypeStruct((8, d), F32)],
               compiler_params=_cp(("arbitrary",)))(x, dh, w.reshape(1, d), dx_in)
    (dx, dxb, dg), got = outs if carry is not None else (outs, [])
    return dx, dxb, dg[0], got


def _loss_head(x, w, target):
    s, d = x.shape
    tm = _tile(s, (256, 128, 64))

    def body(x_ref, w_ref, t_ref, dx_ref, dxb_ref, dg_ref, l_ref):
        @pl.when(pl.program_id(0) == 0)
        def _():
            dg_ref[...] = jnp.zeros_like(dg_ref)
            l_ref[...] = jnp.zeros_like(l_ref)

        xv = x_ref[...]
        r = lax.rsqrt(jnp.mean(xv * xv, axis=-1, keepdims=True) + EPS)
        xh = xv * r
        err = xh * w_ref[...] - t_ref[...]
        l_ref[...] += 0.5 * jnp.sum(jnp.mean(err * err, axis=-1, keepdims=True), axis=0, keepdims=True)
        dy = err * (1.0 / d)
        dxh = dy * w_ref[...]
        dx = r * (dxh - xh * jnp.mean(dxh * xh, axis=-1, keepdims=True))
        dx_ref[...] = dx
        dxb_ref[...] = dx.astype(BF16)
        dg_ref[0:1, :] += jnp.sum(dy * xh, axis=0, keepdims=True)

    row = pl.BlockSpec((tm, d), lambda i: (i, 0))
    dx, dxb, dg, ls = _pc(body, name="loss_head", grid=(s // tm,),
                          in_specs=[row, pl.BlockSpec((1, d), lambda i: (0, 0)), row],
                          out_specs=[row, row, pl.BlockSpec((8, d), lambda i: (0, 0)),
                                     pl.BlockSpec((8, 128), lambda i: (0, 0))],
                          out_shape=[jax.ShapeDtypeStruct((s, d), F32), jax.ShapeDtypeStruct((s, d), BF16),
                                     jax.ShapeDtypeStruct((8, d), F32), jax.ShapeDtypeStruct((8, 128), F32)],
                          compiler_params=_cp(("arbitrary",)))(x, w.reshape(1, d), target)
    return ls[0, 0], dx, dxb, dg[0]


def _gdn_prep_fwd(proj, conv_w, width):
    s = proj.shape[0]
    tm = _tile(s, (256, 128, 64))
    nh = width // HEAD

    def body(c_ref, p_ref, w_ref, o_ref):
        i, seg = pl.program_id(0), pl.program_id(1)
        cur = c_ref[...].astype(F32)
        prev = jnp.where(i > 0, p_ref[...].astype(F32), 0.0)
        pre = cur * w_ref[GDN_K - 1:GDN_K, :]
        for j in range(GDN_K - 1):
            pre = pre + _shift_down(cur, prev, GDN_K - 1 - j) * w_ref[j:j + 1, :]
        act = pre * _sigmoid(pre)
        scale = jnp.where(seg == 0, HEAD ** -0.5, 1.0)
        for h in range(nh):
            a = act[:, h * HEAD:(h + 1) * HEAD]
            rs = lax.rsqrt(jnp.sum(a * a, axis=-1, keepdims=True) + EPS) * scale
            o_ref[:, h * HEAD:(h + 1) * HEAD] = a * jnp.where(seg < 2, rs, 1.0)

    return _pc(body, name="gdn_prep_fwd", grid=(s // tm, 3),
               in_specs=[pl.BlockSpec((tm, width), lambda i, j: (i, j + QKV_COL)),
                         pl.BlockSpec((HALO, width), _prev_map(tm, lambda j: j + QKV_COL)),
                         pl.BlockSpec((GDN_K, width), lambda i, j: (0, j))],
               out_specs=pl.BlockSpec((tm, width), lambda i, j: (i, j)),
               out_shape=jax.ShapeDtypeStruct((s, 3 * width), F32),
               compiler_params=_cp(("parallel", "parallel")))(proj, proj, conv_w)


def _gdn_prep_bwd(dqkv, proj, conv_w, dproj_in, width):
    s = proj.shape[0]
    tm = _tile(s, (256, 128, 64))
    nh = width // HEAD
    nt = s // tm

    def body(c_ref, p_ref, n_ref, d_ref, dn_ref, w_ref, _, o_ref, dw_ref):
        seg, i = pl.program_id(0), pl.program_id(1)

        @pl.when(i == 0)
        def _():
            dw_ref[...] = jnp.zeros_like(dw_ref)

        ext = jnp.concatenate([c_ref[...].astype(F32), n_ref[...].astype(F32)], axis=0)
        prev = jnp.where(i > 0, p_ref[...].astype(F32), 0.0)
        sh = [_shift_down(ext, prev, GDN_K - 1 - j) for j in range(GDN_K)]
        pre = sh[0] * w_ref[0:1, :]
        for j in range(1, GDN_K):
            pre = pre + sh[j] * w_ref[j:j + 1, :]
        sg = _sigmoid(pre)
        act = pre * sg
        dout = jnp.concatenate([d_ref[...], dn_ref[...]], axis=0)
        rows = lax.broadcasted_iota(jnp.int32, (tm + HALO, 1), 0)
        dout = jnp.where((rows < tm) | (i < nt - 1), dout, 0.0)
        scale = jnp.where(seg == 0, HEAD ** -0.5, 1.0)
        parts = []
        for h in range(nh):
            a = act[:, h * HEAD:(h + 1) * HEAD]
            dq = dout[:, h * HEAD:(h + 1) * HEAD]
            rs = lax.rsqrt(jnp.sum(a * a, axis=-1, keepdims=True) + EPS)
            nrm = a * rs
            dn = dq * scale
            da_norm = rs * (dn - nrm * jnp.sum(dn * nrm, axis=-1, keepdims=True))
            parts.append(jnp.where(seg < 2, da_norm, dq))
        dact = jnp.concatenate(parts, axis=1)
        dpre = dact * (sg * (1.0 + pre * (1.0 - sg)))
        dp = _shift_up(dpre, 0, tm) * w_ref[GDN_K - 1:GDN_K, :]
        for j in range(GDN_K - 1):
            dp = dp + _shift_up(dpre, GDN_K - 1 - j, tm) * w_ref[j:j + 1, :]
        o_ref[...] = dp.astype(BF16)
        for j in range(GDN_K):
            dw_ref[j:j + 1, :] += jnp.sum(dpre[:tm] * sh[j][:tm], axis=0, keepdims=True)

    dproj, dw = _pc(body, name="gdn_prep_bwd", grid=(3, nt),
                    in_specs=[pl.BlockSpec((tm, width), lambda j, i: (i, j + QKV_COL)),
                              pl.BlockSpec((HALO, width), lambda j, i: _prev_map(tm, lambda c: c + QKV_COL)(i, j)),
                              pl.BlockSpec((HALO, width), lambda j, i: _next_map(tm, s, lambda c: c + QKV_COL)(i, j)),
                              pl.BlockSpec((tm, width), lambda j, i: (i, j)),
                              pl.BlockSpec((HALO, width), lambda j, i: _next_map(tm, s, lambda c: c)(i, j)),
                              pl.BlockSpec((GDN_K, width), lambda j, i: (0, j)),
                              pl.BlockSpec(memory_space=pl.ANY)],
                    out_specs=[pl.BlockSpec((tm, width), lambda j, i: (i, j + QKV_COL)),
                               pl.BlockSpec((8, width), lambda j, i: (0, j))],
                    out_shape=[jax.ShapeDtypeStruct(dproj_in.shape, BF16), jax.ShapeDtypeStruct((8, 3 * width), F32)],
                    input_output_aliases={6: 0},
                    compiler_params=_cp(("parallel", "arbitrary")))(proj, proj, proj, dqkv, dqkv, conv_w, dproj_in)
    return dproj, dw[:GDN_K]


def _chunk_common(bav, gp_ref, nh):
    g_full = -jnp.exp(gp_ref[0:1, :]) * _softplus(bav + gp_ref[1:2, :])
    beta_full = _sigmoid(bav)
    ri = lax.broadcasted_iota(jnp.int32, (CHUNK, CHUNK), 0)
    ci = lax.broadcasted_iota(jnp.int32, (CHUNK, CHUNK), 1)
    gc_full = _mm_exact_lhs(ri >= ci, g_full)
    gc_t = gc_full.T
    return beta_full, gc_full, gc_t, ri, ci


def _head_gates(h, nh, beta_full, gc_full, gc_t, ri, ci):
    bcol = beta_full[:, h:h + 1]
    gcol = gc_full[:, nh + h:nh + h + 1]
    grow = gc_t[nh + h:nh + h + 1, :]
    dec = jnp.exp(jnp.where(ri >= ci, gcol - grow, -1e30))
    ecol = jnp.exp(gcol)
    gl = gcol[CHUNK - 1:CHUNK, :]
    return bcol, gcol, dec, ecol, gl


def _gdn_fwd(qkv, ba, gp, width, carry=None):
    s = qkv.shape[0]
    nh = width // HEAD
    nc = s // CHUNK
    heads = range(nh)

    def body(q_ref, k_ref, v_ref, ba_ref, gp_ref, o_ref, st_ref, t_ref, state):
        @pl.when(pl.program_id(0) == 0)
        def _():
            state[...] = jnp.zeros_like(state)

        beta_full, gc_full, gc_t, ri, ci = _chunk_common(ba_ref[...], gp_ref, nh)
        eye = (ri == ci).astype(F32)
        sls = [slice(h * HEAD, (h + 1) * HEAD) for h in heads]
        q = [q_ref[:, sl] for sl in sls]
        k = [k_ref[:, sl] for sl in sls]
        v = [v_ref[:, sl] for sl in sls]
        st = [state[h] for h in heads]
        gates = [_head_gates(h, nh, beta_full, gc_full, gc_t, ri, ci) for h in heads]
        bcol, gcol, dec, ecol, gl = (list(z) for z in zip(*gates))
        kb = [k[h] * bcol[h] for h in heads]
        a = [jnp.where(ri > ci, _mmb_nt(kb[h], k[h]) * dec[h], 0.0) for h in heads]
        attn = [jnp.where(ri >= ci, _mmb_nt(q[h], k[h]) * dec[h], 0.0) for h in heads]
        t = [eye - a[h] for h in heads]
        pw = [_mm3(a[h], a[h]) for h in heads]
        for _ in range(4):
            both = [_mm3(jnp.concatenate([t[h], pw[h]], axis=0), pw[h]) for h in heads]
            t = [t[h] + both[h][:CHUNK] for h in heads]
            pw = [both[h][CHUNK:] for h in heads]
        t = [t[h] + _mm3(t[h], pw[h]) for h in heads]
        uw = [_mmb(t[h], jnp.concatenate([v[h] * bcol[h], kb[h] * ecol[h]], axis=1)) for h in heads]
        vn = [uw[h][:, :HEAD] - _mmb(uw[h][:, HEAD:], st[h]) for h in heads]
        out = [_mmb(q[h] * ecol[h], st[h]) + _mmb(attn[h], vn[h]) for h in heads]
        new = [st[h] * jnp.exp(gl[h]) + _mmb_tn(k[h] * jnp.exp(gl[h] - gcol[h]), vn[h]) for h in heads]
        for h in heads:
            o_ref[:, sls[h]] = out[h]
            st_ref[0, h] = st[h]
            t_ref[0, h] = t[h]
            state[h] = new[h]

    blk = lambda c: pl.BlockSpec((CHUNK, width), lambda n, c=c: (n, c))
    outs = _pc(body, carry=carry, name="gdn_fwd", grid=(nc,),
               in_specs=[blk(0), blk(1), blk(2), pl.BlockSpec((CHUNK, 128), lambda n: (n, 0)),
                         pl.BlockSpec((8, 128), lambda n: (0, 0))],
               out_specs=[blk(0), pl.BlockSpec((1, nh, HEAD, HEAD), lambda n: (n, 0, 0, 0)),
                          pl.BlockSpec((1, nh, CHUNK, CHUNK), lambda n: (n, 0, 0, 0))],
               out_shape=[jax.ShapeDtypeStruct((s, width), F32), jax.ShapeDtypeStruct((nc, nh, HEAD, HEAD), F32),
                          jax.ShapeDtypeStruct((nc, nh, CHUNK, CHUNK), F32)],
               scratch_shapes=[pltpu.VMEM((nh, HEAD, HEAD), F32)],
               compiler_params=_cp(("arbitrary",)))(qkv, qkv, qkv, ba, gp)
    (o, st, t), got = outs if carry is not None else (outs, [])
    return o, st, t, got


def _gdn_bwd(qkv, ba, gp, do, states, tinv, width, carry=None):
    s = qkv.shape[0]
    nh = width // HEAD
    nc = s // CHUNK
    heads = range(nh)

    def body(q_ref, k_ref, v_ref, ba_ref, gp_ref, do_ref, st_ref, t_ref, dqkv_ref, dgb_ref, dstate):
        @pl.when(pl.program_id(0) == 0)
        def _():
            dstate[...] = jnp.zeros_like(dstate)

        beta_full, gc_full, gc_t, ri, ci = _chunk_common(ba_ref[...], gp_ref, nh)
        lane = lax.broadcasted_iota(jnp.int32, (CHUNK, 128), 1)
        rowi = lax.broadcasted_iota(jnp.int32, (CHUNK, 1), 0)
        low, strict = ri >= ci, ri > ci
        each = lambda fn: [fn(h) for h in heads]
        rowsum = lambda x: jnp.sum(x, axis=1, keepdims=True)
        sls = each(lambda h: slice(h * HEAD, (h + 1) * HEAD))
        q, k, v = each(lambda h: q_ref[:, sls[h]]), each(lambda h: k_ref[:, sls[h]]), each(lambda h: v_ref[:, sls[h]])
        dout = each(lambda h: do_ref[:, sls[h]])
        st, t, dsp = each(lambda h: st_ref[0, h]), each(lambda h: t_ref[0, h]), each(lambda h: dstate[h])
        gates = each(lambda h: _head_gates(h, nh, beta_full, gc_full, gc_t, ri, ci))
        bcol, gcol, dec, ecol, gl = (list(z) for z in zip(*gates))
        el = each(lambda h: jnp.exp(gl[h]))
        kdsc = each(lambda h: jnp.exp(gl[h] - gcol[h]))
        kb = each(lambda h: k[h] * bcol[h])
        a = each(lambda h: jnp.where(strict, _mmb_nt(kb[h], k[h]) * dec[h], 0.0))
        attn = each(lambda h: jnp.where(low, _mmb_nt(q[h], k[h]) * dec[h], 0.0))
        uw = each(lambda h: _mmb(t[h], jnp.concatenate([v[h] * bcol[h], kb[h] * ecol[h]], axis=1)))
        w = each(lambda h: uw[h][:, HEAD:])
        kd = each(lambda h: k[h] * kdsc[h])
        vn = each(lambda h: uw[h][:, :HEAD] - _mmb(w[h], st[h]))
        d_attn = each(lambda h: jnp.where(low, _mmb_nt(dout[h], vn[h]), 0.0))
        d_vn = each(lambda h: _mmb_tn(attn[h], dout[h]) + _mmb(kd[h], dsp[h]))
        d_qd = each(lambda h: _mmb_nt(dout[h], st[h]))
        d_kd = each(lambda h: _mmb_nt(vn[h], dsp[h]))
        d_el = each(lambda h: jnp.sum(rowsum(st[h] * dsp[h]), axis=0, keepdims=True))
        dst_new = each(lambda h: _mmb_tn(q[h] * ecol[h], dout[h]) + el[h] * dsp[h] - _mmb_tn(w[h], d_vn[h]))
        d_w = each(lambda h: -_mmb_nt(d_vn[h], st[h]))
        dr = each(lambda h: _mmb_tn(t[h], jnp.concatenate([d_vn[h], d_w[h]], axis=1)))
        dru, drw = each(lambda h: dr[h][:, :HEAD]), each(lambda h: dr[h][:, HEAD:])
        d_a = each(lambda h: -jnp.where(strict, _mmb_nt(dr[h], uw[h]), 0.0))
        d_kk = each(lambda h: d_a[h] * dec[h])
        d_qk = each(lambda h: d_attn[h] * dec[h])
        d_kb = each(lambda h: _mmb(d_kk[h], k[h]) + drw[h] * ecol[h])
        dk = each(lambda h: _mmb_tn(d_kk[h], kb[h]) + _mmb_tn(d_qk[h], q[h]) + d_kb[h] * bcol[h] + d_kd[h] * kdsc[h])
        dq = each(lambda h: _mmb(d_qk[h], k[h]) + d_qd[h] * ecol[h])
        dbeta = each(lambda h: rowsum(dru[h] * v[h] + d_kb[h] * k[h]))
        de = each(lambda h: rowsum(drw[h] * kb[h] + d_qd[h] * q[h]))
        r = each(lambda h: rowsum(d_kd[h] * k[h]) * kdsc[h])
        mm = each(lambda h: d_a[h] * a[h] + d_attn[h] * attn[h])
        d_gl = each(lambda h: jnp.sum(r[h], axis=0, keepdims=True) + d_el[h] * el[h])
        d_gc = each(lambda h: de[h] * ecol[h] - r[h] + rowsum(mm[h]) - rowsum(mm[h].T)
                    + jnp.where(rowi == CHUNK - 1, d_gl[h], 0.0))
        dbeta_full = jnp.zeros((CHUNK, 128), F32)
        dgc_full = jnp.zeros((CHUNK, 128), F32)
        for h in heads:
            dqkv_ref[:, sls[h]] = dq[h]
            dqkv_ref[:, width + h * HEAD:width + (h + 1) * HEAD] = dk[h]
            dqkv_ref[:, 2 * width + h * HEAD:2 * width + (h + 1) * HEAD] = dru[h] * bcol[h]
            dstate[h] = dst_new[h]
            dbeta_full = dbeta_full + jnp.where(lane == h, dbeta[h], 0.0)
            dgc_full = dgc_full + jnp.where(lane == nh + h, d_gc[h], 0.0)
        dgb_ref[...] = dbeta_full + _mm_exact_lhs(ri <= ci, dgc_full)

    rev = lambda c: pl.BlockSpec((CHUNK, width), lambda n, c=c: (nc - 1 - n, c))
    outs = _pc(body, carry=carry, name="gdn_bwd", grid=(nc,),
               in_specs=[rev(0), rev(1), rev(2), pl.BlockSpec((CHUNK, 128), lambda n: (nc - 1 - n, 0)),
                         pl.BlockSpec((8, 128), lambda n: (0, 0)), rev(0),
                         pl.BlockSpec((1, nh, HEAD, HEAD), lambda n: (nc - 1 - n, 0, 0, 0)),
                         pl.BlockSpec((1, nh, CHUNK, CHUNK), lambda n: (nc - 1 - n, 0, 0, 0))],
               out_specs=[pl.BlockSpec((CHUNK, 3 * width), lambda n: (nc - 1 - n, 0)),
                          pl.BlockSpec((CHUNK, 128), lambda n: (nc - 1 - n, 0))],
               out_shape=[jax.ShapeDtypeStruct((s, 3 * width), F32), jax.ShapeDtypeStruct((s, 128), F32)],
               scratch_shapes=[pltpu.VMEM((nh, HEAD, HEAD), F32)],
               compiler_params=_cp(("arbitrary",)))(qkv, qkv, qkv, ba, gp, do, states, tinv)
    (dqkv, dgb), got = outs if carry is not None else (outs, [])
    return dqkv, dgb, got


def _gates_bwd(ba, dgb, gp, nh):
    s = ba.shape[0]
    tm = _tile(s, (512, 256, 128, 64))

    def body(ba_ref, d_ref, gp_ref, o_ref, dp_ref):
        @pl.when(pl.program_id(0) == 0)
        def _():
            dp_ref[...] = jnp.zeros_like(dp_ref)

        bav, dv = ba_ref[...], d_ref[...]
        lane = lax.broadcasted_iota(jnp.int32, bav.shape, 1)
        beta = _sigmoid(bav)
        amp = jnp.exp(gp_ref[0:1, :])
        z = bav + gp_ref[1:2, :]
        d_a = dv * (-amp) * _sigmoid(z)
        d_b = dv * beta * (1.0 - beta)
        is_a = (lane >= nh) & (lane < 2 * nh)
        o_ref[...] = jnp.where(lane < nh, d_b, jnp.where(is_a, d_a, 0.0))
        dp_ref[0:1, :] += jnp.sum(jnp.where(is_a, dv * (-amp) * _softplus(z), 0.0), axis=0, keepdims=True)
        dp_ref[1:2, :] += jnp.sum(jnp.where(is_a, d_a, 0.0), axis=0, keepdims=True)

    row = pl.BlockSpec((tm, 128), lambda i: (i, 0))
    par = pl.BlockSpec((8, 128), lambda i: (0, 0))
    return _pc(body, name="gates_bwd", grid=(s // tm,), in_specs=[row, row, par], out_specs=[row, par],
               out_shape=[jax.ShapeDtypeStruct((s, 128), F32), jax.ShapeDtypeStruct((8, 128), F32)],
               compiler_params=_cp(("arbitrary",)))(ba, dgb, gp)


def _mix_post_fwd(o_raw, proj, gain, sc_w, width):
    s = o_raw.shape[0]
    tm = _tile(s, (256, 128, 64))
    nh = width // HEAD
    ksc = sc_w.shape[0]

    def body(o_ref, z_ref, b_ref, c_ref, h_ref, cp_ref, hp_ref, g_ref, w_ref, y_ref):
        i = pl.program_id(0)
        z = z_ref[...].astype(F32)
        sz = z * _sigmoid(z)
        for h in range(nh):
            sl = slice(h * HEAD, (h + 1) * HEAD)
            o = o_ref[:, sl]
            r = lax.rsqrt(jnp.mean(o * o, axis=-1, keepdims=True) + EPS)
            y_ref[:, sl] = (o * r * g_ref[...] * sz[:, sl]).astype(BF16)
        prod = c_ref[...].astype(F32) * h_ref[...].astype(F32)
        pprev = jnp.where(i > 0, cp_ref[...].astype(F32) * hp_ref[...].astype(F32), 0.0)
        cv = prod * w_ref[ksc - 1:ksc, :]
        for j in range(ksc - 1):
            cv = cv + _shift_down(prod, pprev, ksc - 1 - j) * w_ref[j:j + 1, :]
        y_ref[:, width:] = (b_ref[...].astype(F32) * cv).astype(BF16)

    col = lambda c: pl.BlockSpec((tm, width), lambda i, c=c: (i, c))
    prv = lambda c: pl.BlockSpec((HALO, width), lambda i, c=c: (jnp.maximum(i * (tm // HALO) - 1, 0), c))
    return _pc(body, name="mix_post_fwd", grid=(s // tm,),
               in_specs=[col(0), col(0), col(1), col(2), col(3), prv(2), prv(3),
                         pl.BlockSpec((1, HEAD), lambda i: (0, 0)), pl.BlockSpec((ksc, width), lambda i: (0, 0))],
               out_specs=pl.BlockSpec((tm, 2 * width), lambda i: (i, 0)),
               out_shape=jax.ShapeDtypeStruct((s, 2 * width), BF16),
               compiler_params=_cp(("parallel",)))(o_raw, proj, proj, proj, proj, proj, proj,
                                                    gain.reshape(1, HEAD), sc_w)


def _mix_post_bwd(dy, o_raw, proj, gain, sc_w, width, carry=None):
    s = o_raw.shape[0]
    tm = _tile(s, (256, 128, 64))
    nt = s // tm
    nh = width // HEAD
    ksc = sc_w.shape[0]

    def body(dyg_ref, dys_ref, dysn_ref, o_ref, z_ref, b_ref, bn_ref, c_ref, h_ref, cp_ref, hp_ref, g_ref, w_ref,
             do_ref, dp_ref, dg_ref, dw_ref):
        i = pl.program_id(0)

        @pl.when(i == 0)
        def _():
            dg_ref[...] = jnp.zeros_like(dg_ref)
            dw_ref[...] = jnp.zeros_like(dw_ref)

        z = z_ref[...].astype(F32)
        sg = _sigmoid(z)
        sz = z * sg
        dsz = sg * (1.0 + z * (1.0 - sg))
        dyg = dyg_ref[...].astype(F32)
        dgain = jnp.zeros((1, HEAD), F32)
        for h in range(nh):
            sl = slice(h * HEAD, (h + 1) * HEAD)
            o = o_ref[:, sl]
            r = lax.rsqrt(jnp.mean(o * o, axis=-1, keepdims=True) + EPS)
            oh = o * r
            d_yn = dyg[:, sl] * sz[:, sl]
            dp_ref[:, sl] = (dyg[:, sl] * oh * g_ref[...] * dsz[:, sl]).astype(BF16)
            dgain = dgain + jnp.sum(d_yn * oh, axis=0, keepdims=True)
            doh = d_yn * g_ref[...]
            do_ref[:, sl] = r * (doh - oh * jnp.mean(doh * oh, axis=-1, keepdims=True))
        dg_ref[0:1, :] += dgain
        cc, hh = c_ref[...].astype(F32), h_ref[...].astype(F32)
        prod = cc * hh
        pprev = jnp.where(i > 0, cp_ref[...].astype(F32) * hp_ref[...].astype(F32), 0.0)
        sh = [_shift_down(prod, pprev, ksc - 1 - j) for j in range(ksc)]
        cv = sh[0] * w_ref[0:1, :]
        for j in range(1, ksc):
            cv = cv + sh[j] * w_ref[j:j + 1, :]
        dys = dys_ref[...].astype(F32)
        dp_ref[:, width:2 * width] = (dys * cv).astype(BF16)
        dcv_n = jnp.where(i < nt - 1, dysn_ref[...].astype(F32) * bn_ref[...].astype(F32), 0.0)
        dcv = jnp.concatenate([dys * b_ref[...].astype(F32), dcv_n], axis=0)
        dprod = dcv[:tm] * w_ref[ksc - 1:ksc, :]
        for j in range(ksc - 1):
            dprod = dprod + _shift_up(dcv, ksc - 1 - j, tm) * w_ref[j:j + 1, :]
        dp_ref[:, 2 * width:3 * width] = (dprod * hh).astype(BF16)
        dp_ref[:, 3 * width:] = (dprod * cc).astype(BF16)
        for j in range(ksc):
            dw_ref[j:j + 1, :] += jnp.sum(dcv[:tm] * sh[j], axis=0, keepdims=True)

    col = lambda c: pl.BlockSpec((tm, width), lambda i, c=c: (i, c))
    prv = lambda c: pl.BlockSpec((HALO, width), lambda i, c=c: (jnp.maximum(i * (tm // HALO) - 1, 0), c))
    nxt = lambda c: pl.BlockSpec((HALO, width), lambda i, c=c: (jnp.minimum((i + 1) * (tm // HALO), s // HALO - 1), c))
    outs = _pc(
        body, carry=carry, name="mix_post_bwd", grid=(nt,),
        in_specs=[col(0), col(1), nxt(1), col(0), col(0), col(1), nxt(1), col(2), col(3), prv(2), prv(3),
                  pl.BlockSpec((1, HEAD), lambda i: (0, 0)), pl.BlockSpec((ksc, width), lambda i: (0, 0))],
        out_specs=[col(0), pl.BlockSpec((tm, 4 * width), lambda i: (i, 0)),
                   pl.BlockSpec((8, HEAD), lambda i: (0, 0)), pl.BlockSpec((8, width), lambda i: (0, 0))],
        out_shape=[jax.ShapeDtypeStruct((s, width), F32), jax.ShapeDtypeStruct((s, 7 * width), BF16),
                   jax.ShapeDtypeStruct((8, HEAD), F32), jax.ShapeDtypeStruct((8, width), F32)],
        compiler_params=_cp(("arbitrary",)))(dy, dy, dy, o_raw, proj, proj, proj, proj, proj, proj, proj,
                                             gain.reshape(1, HEAD), sc_w)
    (do, dp, dg, dw), got = outs if carry is not None else (outs, [])
    return do, dp, dg[0], dw[:ksc], got


def _xattn_fwd(q, k, v):
    s, d = q.shape
    nm = k.shape[0]
    dh = d // XHEADS
    tm = _tile(s, (512, 256, 128, 64))

    def body(q_ref, k_ref, v_ref, o_ref):
        sc = _mmb_nt(q_ref[...], k_ref[...]) * (dh ** -0.5)
        p = jnp.exp(sc - jnp.max(sc, axis=-1, keepdims=True))
        p = p / jnp.sum(p, axis=-1, keepdims=True)
        o_ref[...] = _mmb(p, v_ref[...]).astype(BF16)

    return _pc(body, name="xattn_fwd", grid=(s // tm, XHEADS),
               in_specs=[pl.BlockSpec((tm, dh), lambda i, h: (i, h)), pl.BlockSpec((nm, dh), lambda i, h: (0, h)),
                         pl.BlockSpec((nm, dh), lambda i, h: (0, h))],
               out_specs=pl.BlockSpec((tm, dh), lambda i, h: (i, h)), out_shape=jax.ShapeDtypeStruct((s, d), BF16),
               compiler_params=_cp(("parallel", "parallel")))(q, k, v)


def _xattn_bwd(q, k, v, do, carry=None):
    s, d = q.shape
    nm = k.shape[0]
    dh = d // XHEADS
    tm = _tile(s, (512, 256, 128, 64))

    def body(q_ref, k_ref, v_ref, do_ref, dq_ref, dk_ref, dv_ref):
        @pl.when(pl.program_id(1) == 0)
        def _():
            dk_ref[...] = jnp.zeros_like(dk_ref)
            dv_ref[...] = jnp.zeros_like(dv_ref)

        scale = dh ** -0.5
        sc = _mmb_nt(q_ref[...], k_ref[...]) * scale
        p = jnp.exp(sc - jnp.max(sc, axis=-1, keepdims=True))
        p = p / jnp.sum(p, axis=-1, keepdims=True)
        dp = _mmb_nt(do_ref[...], v_ref[...])
        ds = p * (dp - jnp.sum(dp * p, axis=-1, keepdims=True)) * scale
        dq_ref[...] = _mmb(ds, k_ref[...]).astype(BF16)
        dk_ref[...] += _mmb_tn(ds, q_ref[...])
        dv_ref[...] += _mmb_tn(p, do_ref[...])

    rowb = pl.BlockSpec((tm, dh), lambda h, i: (i, h))
    memb = pl.BlockSpec((nm, dh), lambda h, i: (0, h))
    outs = _pc(body, carry=carry, name="xattn_bwd", grid=(XHEADS, s // tm), in_specs=[rowb, memb, memb, rowb],
               out_specs=[rowb, memb, memb],
               out_shape=[jax.ShapeDtypeStruct((s, d), BF16), jax.ShapeDtypeStruct((nm, d), F32),
                          jax.ShapeDtypeStruct((nm, d), F32)],
               compiler_params=_cp(("parallel", "arbitrary")))(q, k, v, do)
    (dq, dk, dv), got = outs if carry is not None else (outs, [])
    return dq, dk, dv, got


def _ffn_act_fwd(u_pre, conv_w):
    s, f2 = u_pre.shape
    f = f2 // 2
    tm = _tile(s, (256, 128, 64))
    cb = _tile(f, ACT_TILES)
    nf = f // cb
    kf = conv_w.shape[0]

    def body(g_ref, u_ref, gp_ref, up_ref, wg_ref, wu_ref, a_ref):
        i = pl.program_id(0)

        def conv(c_ref, p_ref, w_ref):
            cur = c_ref[...].astype(F32)
            prev = jnp.where(i > 0, p_ref[...].astype(F32), 0.0)
            out = cur * w_ref[kf - 1:kf, :]
            for j in range(kf - 1):
                out = out + _shift_down(cur, prev, kf - 1 - j) * w_ref[j:j + 1, :]
            return out

        gate, up = conv(g_ref, gp_ref, wg_ref), conv(u_ref, up_ref, wu_ref)
        a_ref[...] = (gate * _sigmoid(gate) * up).astype(BF16)

    return _pc(body, name="ffn_act_fwd", grid=(s // tm, nf),
               in_specs=[pl.BlockSpec((tm, cb), lambda i, j: (i, j)), pl.BlockSpec((tm, cb), lambda i, j: (i, j + nf)),
                         pl.BlockSpec((HALO, cb), _prev_map(tm, lambda j: j)),
                         pl.BlockSpec((HALO, cb), _prev_map(tm, lambda j: j + nf)),
                         pl.BlockSpec((kf, cb), lambda i, j: (0, j)), pl.BlockSpec((kf, cb), lambda i, j: (0, j + nf))],
               out_specs=pl.BlockSpec((tm, cb), lambda i, j: (i, j)), out_shape=jax.ShapeDtypeStruct((s, f), BF16),
               compiler_params=_cp(("parallel", "parallel")))(u_pre, u_pre, u_pre, u_pre, conv_w, conv_w)


def _ffn_act_bwd(da, u_pre, conv_w, carry=None):
    s, f2 = u_pre.shape
    f = f2 // 2
    tm = _tile(s, (256, 128, 64))
    nt = s // tm
    cb = _tile(f, ACT_TILES)
    nf = f // cb
    kf = conv_w.shape[0]

    def body(da_ref, dan_ref, g_ref, gp_ref, gn_ref, u_ref, up_ref, un_ref, wg_ref, wu_ref, d_ref, dw_ref):
        i = pl.program_id(1)

        @pl.when(i == 0)
        def _():
            dw_ref[...] = jnp.zeros_like(dw_ref)

        def conv(c_ref, p_ref, n_ref, w_ref):
            ext = jnp.concatenate([c_ref[...].astype(F32), n_ref[...].astype(F32)], axis=0)
            prev = jnp.where(i > 0, p_ref[...].astype(F32), 0.0)
            sh = [_shift_down(ext, prev, kf - 1 - j) for j in range(kf)]
            out = sh[0] * w_ref[0:1, :]
            for j in range(1, kf):
                out = out + sh[j] * w_ref[j:j + 1, :]
            return out, sh

        gate, gsh = conv(g_ref, gp_ref, gn_ref, wg_ref)
        up, ush = conv(u_ref, up_ref, un_ref, wu_ref)
        dav = jnp.concatenate([da_ref[...].astype(F32), dan_ref[...].astype(F32)], axis=0)
        rows = lax.broadcasted_iota(jnp.int32, (tm + HALO, 1), 0)
        dav = jnp.where((rows < tm) | (i < nt - 1), dav, 0.0)
        sg = _sigmoid(gate)
        dgate = dav * up * (sg * (1.0 + gate * (1.0 - sg)))
        dup = dav * (gate * sg)

        def conv_t(dv, w_ref):
            out = dv[:tm] * w_ref[kf - 1:kf, :]
            for j in range(kf - 1):
                out = out + _shift_up(dv, kf - 1 - j, tm) * w_ref[j:j + 1, :]
            return out

        d_ref[0] = conv_t(dgate, wg_ref).astype(BF16)
        d_ref[1] = conv_t(dup, wu_ref).astype(BF16)
        for j in range(kf):
            dw_ref[0, j:j + 1, :] += jnp.sum(dgate[:tm] * gsh[j][:tm], axis=0, keepdims=True)
            dw_ref[1, j:j + 1, :] += jnp.sum(dup[:tm] * ush[j][:tm], axis=0, keepdims=True)

    pm = lambda off: (lambda j, i: _prev_map(tm, lambda c: c + off)(i, j))
    nm = lambda off: (lambda j, i: _next_map(tm, s, lambda c: c + off)(i, j))
    outs = _pc(
        body, carry=carry, name="ffn_act_bwd", grid=(nf, nt),
        in_specs=[pl.BlockSpec((tm, cb), lambda j, i: (i, j)), pl.BlockSpec((HALO, cb), nm(0)),
                  pl.BlockSpec((tm, cb), lambda j, i: (i, j)), pl.BlockSpec((HALO, cb), pm(0)),
                  pl.BlockSpec((HALO, cb), nm(0)),
                  pl.BlockSpec((tm, cb), lambda j, i: (i, j + nf)), pl.BlockSpec((HALO, cb), pm(nf)),
                  pl.BlockSpec((HALO, cb), nm(nf)),
                  pl.BlockSpec((kf, cb), lambda j, i: (0, j)), pl.BlockSpec((kf, cb), lambda j, i: (0, j + nf))],
        out_specs=[pl.BlockSpec((2, tm, cb), lambda j, i: (0, i, j)), pl.BlockSpec((2, 8, cb), lambda j, i: (0, 0, j))],
        out_shape=[jax.ShapeDtypeStruct((2, s, f), BF16), jax.ShapeDtypeStruct((2, 8, f), F32)],
        compiler_params=_cp(("parallel", "arbitrary")))(da, da, u_pre, u_pre, u_pre, u_pre, u_pre, u_pre, conv_w, conv_w)
    (du, dw), got = outs if carry is not None else (outs, [])
    return du, dw[:, :kf], got


def _adamw(w, g, m, v, name):
    shape = w.shape
    c = shape[-1]
    r = w.size // c
    tr = r if r * c <= 262144 else _tile(r, tuple(t for t in (512, 256, 128, 64, 32, 16, 8) if t * c <= 262144))
    bc1 = 1.0 - ADAM_B1 ** ADAM_STEP
    bc2 = 1.0 - ADAM_B2 ** ADAM_STEP

    def body(w_ref, g_ref, m_ref, v_ref, d_ref, nm_ref, nv_ref):
        gv = g_ref[...]
        mn = ADAM_B1 * m_ref[...] + (1.0 - ADAM_B1) * gv
        vn = ADAM_B2 * v_ref[...] + (1.0 - ADAM_B2) * (gv * gv)
        nm_ref[...] = mn
        nv_ref[...] = vn
        d_ref[...] = -ADAM_LR * ((mn / bc1) / (jnp.sqrt(vn / bc2) + ADAM_EPS) + ADAM_WD * w_ref[...])

    blk = pl.BlockSpec((tr, c), lambda i: (i, 0))
    outs = _pc(body, name=name, grid=(r // tr,), in_specs=[blk] * 4, out_specs=[blk] * 3,
               out_shape=[jax.ShapeDtypeStruct((r, c), F32)] * 3,
               compiler_params=_cp(("parallel",)))(*(t.reshape(r, c) for t in (w, g, m, v)))
    return tuple(o.reshape(shape) for o in outs)


def _adamw_sharded(w, m, v, partials, name):
    nl, r, c = w.shape
    tr = _tile(r, tuple(t for t in (256, 128, 64, 32, 16, 8) if t * c <= 131072))
    bc1 = 1.0 - ADAM_B1 ** ADAM_STEP
    bc2 = 1.0 - ADAM_B2 ** ADAM_STEP

    def body(w_ref, m_ref, v_ref, *rest):
        p_refs, (g_ref, d_ref, nm_ref, nv_ref) = rest[:nl], rest[nl:]
        layer = pl.program_id(0)
        for l in range(nl):
            @pl.when(layer == l)
            def _(p_ref=p_refs[l]):
                gv = p_ref[0].astype(F32)
                for dev in range(1, N_DEV):
                    gv = gv + p_ref[dev].astype(F32)
                mn = ADAM_B1 * m_ref[...] + (1.0 - ADAM_B1) * gv
                vn = ADAM_B2 * v_ref[...] + (1.0 - ADAM_B2) * (gv * gv)
                g_ref[...] = gv
                nm_ref[...] = mn
                nv_ref[...] = vn
                d_ref[...] = -ADAM_LR * ((mn / bc1) / (jnp.sqrt(vn / bc2) + ADAM_EPS) + ADAM_WD * w_ref[...])

    blk = pl.BlockSpec((None, tr, c), lambda l, i: (l, i, 0))
    p_specs = [pl.BlockSpec((N_DEV, tr, c), lambda l, i, k=k: (0, jnp.where(l == k, i, 0), 0)) for k in range(nl)]
    return _pc(body, name=name, grid=(nl, r // tr), in_specs=[blk] * 3 + p_specs, out_specs=[blk] * 4,
               out_shape=[jax.ShapeDtypeStruct((nl, r, c), F32)] * 4,
               compiler_params=_cp(("arbitrary", "arbitrary")))(w, m, v, *partials)


def _slot_sum(x, name):
    _, r, c = x.shape
    tr = _tile(r, (512, 256, 128, 64, 32, 16, 8))

    def body(x_ref, o_ref):
        acc = x_ref[0].astype(F32)
        for d in range(1, N_DEV):
            acc = acc + x_ref[d].astype(F32)
        o_ref[...] = acc

    return _pc(body, name=name, grid=(r // tr,), in_specs=[pl.BlockSpec((N_DEV, tr, c), lambda i: (0, i, 0))],
               out_specs=pl.BlockSpec((tr, c), lambda i: (i, 0)), out_shape=jax.ShapeDtypeStruct((r, c), F32),
               compiler_params=_cp(("parallel",)))(x)


def _all_gather(xs, name, layers=None):
    na = len(xs)
    layers = tuple(range(xs[0].shape[0])) if layers is None else layers
    nl = len(layers) * na

    def body(*refs):
        x_refs, out_refs = refs[:na], refs[na:na + nl]
        send_sems, recv_sems, local_sems = refs[na + nl:]
        mx, my, mc = lax.axis_index("x"), lax.axis_index("y"), lax.axis_index("c")
        me, sibling = (mx, my, mc), (mx, my, 1 - mc)
        chips = [(1 - mx, my), (mx, 1 - my), (1 - mx, 1 - my)]

        def local(l):
            return x_refs[l % na].at[layers[l // na]]

        def slot(l, dev):
            return out_refs[l].at[4 * dev[0] + 2 * dev[1] + dev[2]]

        def copy(l, k, block, to, src=None):
            return pltpu.make_async_remote_copy(
                src_ref=slot(l, block) if src is None else src, dst_ref=slot(l, block),
                send_sem=send_sems.at[l, k], recv_sem=recv_sems.at[l, k], device_id=to, device_id_type=MESH)

        mine = [pltpu.make_async_copy(local(l), slot(l, me), local_sems.at[l]) for l in range(nl)]
        first = []
        for l in range(nl):
            mine[l].start()
            first.append(copy(l, 0, me, sibling, src=local(l)))
            first += [copy(l, 1 + j, me, (*chip, mc), src=local(l)) for j, chip in enumerate(chips)]
        for cp in first:
            cp.start()
        passed = []
        for l in range(nl):
            for j, chip in enumerate(chips):
                copy(l, 1 + j, (*chip, mc), me).wait_recv()
                fw = copy(l, 4 + j, (*chip, mc), sibling)
                fw.start()
                passed.append(fw)
        for l in range(nl):
            copy(l, 0, sibling, me).wait_recv()
            for j, chip in enumerate(chips):
                copy(l, 4 + j, (*chip, 1 - mc), me).wait_recv()
        for cp in first + passed:
            cp.wait_send()
        for l in range(nl):
            mine[l].wait()

    anyspec = pl.BlockSpec(memory_space=pl.ANY)
    outs = _pc(body, name=name, in_specs=[anyspec] * na, out_specs=[anyspec] * nl,
               out_shape=[jax.ShapeDtypeStruct((N_DEV,) + xs[l % na].shape[1:], xs[l % na].dtype) for l in range(nl)],
               scratch_shapes=[pltpu.SemaphoreType.DMA((nl, 7)), pltpu.SemaphoreType.DMA((nl, 7)),
                               pltpu.SemaphoreType.DMA((nl,))],
               compiler_params=_cp())(*xs)
    return [outs[l * na:(l + 1) * na] for l in range(nl // na)]


def _mix_in_assemble(shards, width, nh):
    _, d, cs = shards.shape
    tr = _tile(d, (128, 64, 32, 16))
    w4 = 4 * width

    def body(s_ref, main_ref, gate_ref):
        full = jnp.concatenate([s_ref[j].astype(F32) for j in range(N_DEV)], axis=1)
        main_ref[:, :width] = full[:, 3 * width:w4].astype(BF16)
        main_ref[:, width:w4] = full[:, w4 + 2 * nh:].astype(BF16)
        main_ref[:, w4:] = full[:, :3 * width].astype(BF16)
        gate_ref[...] = jnp.concatenate([full[:, w4:w4 + 2 * nh], jnp.zeros((tr, 128 - 2 * nh), F32)],
                                        axis=1).astype(BF16)

    return _pc(body, name="mix_in_assemble", grid=(d // tr,),
               in_specs=[pl.BlockSpec((N_DEV, tr, cs), lambda i: (0, i, 0))],
               out_specs=[pl.BlockSpec((tr, 7 * width), lambda i: (i, 0)), pl.BlockSpec((tr, 128), lambda i: (i, 0))],
               out_shape=[jax.ShapeDtypeStruct((d, 7 * width), BF16), jax.ShapeDtypeStruct((d, 128), BF16)],
               compiler_params=_cp(("parallel",)))(shards)


def _mix_in_shards(dmain, dgate, width, nh):
    d = dmain.shape[0]
    cs = (7 * width + 2 * nh) // N_DEV
    tr = _tile(d, (128, 64, 32, 16))
    w4 = 4 * width

    def body(main_ref, gate_ref, o_ref):
        full = jnp.concatenate([main_ref[:, w4:], main_ref[:, :width], gate_ref[:, :2 * nh], main_ref[:, width:w4]],
                               axis=1)
        for j in range(N_DEV):
            o_ref[j] = full[:, j * cs:(j + 1) * cs].astype(BF16)

    return _pc(body, name="mix_in_shards", grid=(d // tr,),
               in_specs=[pl.BlockSpec((tr, 7 * width), lambda i: (i, 0)), pl.BlockSpec((tr, 128), lambda i: (i, 0))],
               out_specs=pl.BlockSpec((N_DEV, tr, cs), lambda i: (0, i, 0)),
               out_shape=jax.ShapeDtypeStruct((N_DEV, d, cs), BF16),
               compiler_params=_cp(("parallel",)))(dmain, dgate)


PACK_COLS = 1024
BIG = ("w_mix_in", "w_mix_out", "w_xq", "w_xk", "w_xv", "w_xo", "w_ffn_up", "w_ffn_down")


def _pack(blocks, lead):
    flat = jnp.concatenate(blocks, axis=-1)
    n = flat.shape[-1]
    per = 16 * PACK_COLS
    pad = (-n) % per
    flat = jnp.pad(flat, [(0, 0)] * len(lead) + [(0, pad)])
    return flat.reshape(*lead, (n + pad) // PACK_COLS, PACK_COLS)


def _unpack(packed, sizes):
    flat = packed.reshape(*packed.shape[:-2], -1)
    out, off = [], 0
    for n in sizes:
        out.append(flat[..., off:off + n])
        off += n
    return out


def kernel(x, mem, mix_norm, w_mix_in, gdn_conv, gdn_a_log, gdn_dt_bias, gdn_out_norm, sc_conv, w_mix_out, xattn_norm, mem_norm, w_xq, w_xk, w_xv, w_xo, ffn_norm, w_ffn_up, ffn_conv, w_ffn_down, final_norm, loss_target, m_mix_norm, m_w_mix_in, m_gdn_conv, m_gdn_a_log, m_gdn_dt_bias, m_gdn_out_norm, m_sc_conv, m_w_mix_out, m_xattn_norm, m_mem_norm, m_w_xq, m_w_xk, m_w_xv, m_w_xo, m_ffn_norm, m_w_ffn_up, m_ffn_conv, m_w_ffn_down, m_final_norm, v_mix_norm, v_w_mix_in, v_gdn_conv, v_gdn_a_log, v_gdn_dt_bias, v_gdn_out_norm, v_sc_conv, v_w_mix_out, v_xattn_norm, v_mem_norm, v_w_xq, v_w_xk, v_w_xv, v_w_xo, v_ffn_norm, v_w_ffn_up, v_ffn_conv, v_w_ffn_down, v_final_norm):
    names = ["mix_norm", "w_mix_in", "gdn_conv", "gdn_a_log", "gdn_dt_bias", "gdn_out_norm", "sc_conv", "w_mix_out",
             "xattn_norm", "mem_norm", "w_xq", "w_xk", "w_xv", "w_xo", "ffn_norm", "w_ffn_up", "ffn_conv",
             "w_ffn_down", "final_norm"]
    wts = dict(zip(names, (mix_norm, w_mix_in, gdn_conv, gdn_a_log, gdn_dt_bias, gdn_out_norm, sc_conv, w_mix_out,
                           xattn_norm, mem_norm, w_xq, w_xk, w_xv, w_xo, ffn_norm, w_ffn_up, ffn_conv, w_ffn_down,
                           final_norm)))
    mom1 = dict(zip(names, (m_mix_norm, m_w_mix_in, m_gdn_conv, m_gdn_a_log, m_gdn_dt_bias, m_gdn_out_norm, m_sc_conv,
                            m_w_mix_out, m_xattn_norm, m_mem_norm, m_w_xq, m_w_xk, m_w_xv, m_w_xo, m_ffn_norm,
                            m_w_ffn_up, m_ffn_conv, m_w_ffn_down, m_final_norm)))
    mom2 = dict(zip(names, (v_mix_norm, v_w_mix_in, v_gdn_conv, v_gdn_a_log, v_gdn_dt_bias, v_gdn_out_norm, v_sc_conv,
                            v_w_mix_out, v_xattn_norm, v_mem_norm, v_w_xq, v_w_xk, v_w_xv, v_w_xo, v_ffn_norm,
                            v_w_ffn_up, v_ffn_conv, v_w_ffn_down, v_final_norm)))

    x0 = x[0]
    memv = mem[0]
    target = loss_target[0]
    s, d = x0.shape
    depth = mix_norm.shape[0]
    width = d // 2
    nh = width // HEAD
    me = 4 * lax.axis_index("x") + 2 * lax.axis_index("y") + lax.axis_index("c")

    local_bf16 = {n: wts[n].astype(BF16) for n in BIG}
    gathered = [None] * depth
    gathered[0] = dict(zip(BIG, _all_gather([local_bf16[n] for n in BIG], "all_gather_weights", layers=(0,))[0]))

    def gather_next(l, group):
        return None if l + 1 >= depth else ("gather", [local_bf16[n] for n in group], l + 1)

    def mm_gathering(l, group, *args, **kw):
        carry = gather_next(l, group)
        if carry is None:
            return _mm(*args, **kw)
        out, got = _mm(*args, carry=carry, **kw)
        gathered[l + 1].update(zip(group, got))
        return out

    conv_names = ("gdn_conv", "sc_conv", "ffn_conv")
    conv_local = _pack([wts[n].reshape(1, -1) for n in conv_names], (1,))
    conv_all = _all_gather([conv_local], "all_gather_conv")[0][0]
    conv_parts = _unpack(conv_all, [wts[n].size for n in conv_names])
    conv_full = {}
    for n, part in zip(conv_names, conv_parts):
        _, kt, cs = wts[n].shape
        conv_full[n] = jnp.moveaxis(part.reshape(N_DEV, depth, kt, cs), 0, 2).reshape(depth, kt, N_DEV * cs)

    def layer_weights(l):
        wl = {n: gathered[l][n].reshape(-1, gathered[l][n].shape[-1]) for n in BIG if n not in ("w_mix_in", "w_ffn_up")}
        wl["w_ffn_up"] = gathered[l]["w_ffn_up"]
        return wl

    def gate_params(l):
        rows = jnp.stack([gdn_a_log[l], gdn_dt_bias[l]])
        return jnp.pad(rows, ((0, 6), (nh, 128 - 2 * nh)))

    saved = []
    xc = x0
    for l in range(depth):
        wl = layer_weights(l)
        w_main, w_gate = _mix_in_assemble(gathered[l]["w_mix_in"], width, nh)
        gp = gate_params(l)
        if l + 1 < depth:
            gathered[l + 1] = {}
        h1 = _rms_fwd(xc, mix_norm[l], "rms_mix")
        proj = mm_gathering(l, ("w_mix_in",), h1, w_main, out_dtype=BF16, name="mm_mix_in")
        ba = _mm(h1, w_gate, out_dtype=F32, name="mm_mix_gates")
        qkv = _gdn_prep_fwd(proj, conv_full["gdn_conv"][l], width)
        with_gdn = ("w_mix_out", "w_xq")
        o_raw, states, tinv, got = _gdn_fwd(qkv, ba, gp, width, carry=gather_next(l, with_gdn))
        if got:
            gathered[l + 1].update(zip(with_gdn, got))
        y = _mix_post_fwd(o_raw, proj, gdn_out_norm[l], conv_full["sc_conv"][l], width)
        x1 = mm_gathering(l, ("w_xk",), y, wl["w_mix_out"], res=xc, name="mm_mix_out")
        h2 = _rms_fwd(x1, xattn_norm[l], "rms_xattn")
        mem_n = _rms_fwd(memv, mem_norm[l], "rms_mem")
        qx = mm_gathering(l, ("w_xv",), h2, wl["w_xq"], out_dtype=BF16, name="mm_xq")
        kx = _mm(mem_n, wl["w_xk"], out_dtype=BF16, name="mm_xk")
        vx = _mm(mem_n, wl["w_xv"], out_dtype=BF16, name="mm_xv")
        ox = _xattn_fwd(qx, kx, vx)
        x2 = mm_gathering(l, ("w_xo",), ox, wl["w_xo"], res=x1, name="mm_xo")
        h3 = _rms_fwd(x2, ffn_norm[l], "rms_ffn")
        u_pre = mm_gathering(l, ("w_ffn_up",), h3, wl["w_ffn_up"], b_shards=True, out_dtype=BF16, name="mm_ffn_up")
        act = _ffn_act_fwd(u_pre, conv_full["ffn_conv"][l])
        x3 = mm_gathering(l, ("w_ffn_down",), act, wl["w_ffn_down"], res=x2, name="mm_ffn_down")
        saved.append(dict(x0=xc, x1=x1, x2=x2, h1=h1, h2=h2, h3=h3, proj=proj, ba=ba, qkv=qkv, o_raw=o_raw,
                          states=states, tinv=tinv, y=y, mem_n=mem_n, qx=qx, kx=kx, vx=vx, ox=ox, u_pre=u_pre, act=act,
                          w_main=w_main, w_gate=w_gate, gp=gp, wl=wl))
        xc = x3

    loss_part, dx, dxb, g_final = _loss_head(xc, final_norm, target)

    small = {n: [None] * depth for n in ("mix_norm", "xattn_norm", "mem_norm", "ffn_norm", "gdn_a_log", "gdn_dt_bias",
                                          "gdn_out_norm", "gdn_conv", "sc_conv", "ffn_conv")}
    partials = {n: [None] * depth for n in BIG}

    def scatter(big, group):
        return ("scatter", [big[n].reshape((N_DEV,) + wts[n].shape[1:]) for n in group], None)

    for l in reversed(range(depth)):
        sv = saved[l]
        wl, w_main, w_gate, gp = sv["wl"], sv["w_main"], sv["w_gate"], sv["gp"]
        big = {}
        d_act = _mm(dxb, wl["w_ffn_down"], tb=True, out_dtype=BF16, name="mm_d_act")
        big["w_ffn_down"] = _mm(sv["act"], dxb, ta=True, out_dtype=BF16, name="mm_dw_ffn_down")
        du, dcw, got = _ffn_act_bwd(d_act, sv["u_pre"], conv_full["ffn_conv"][l],
                                    carry=None if l + 1 >= depth else ("scatter", [ffn_up_above], None))
        if got:
            partials["w_ffn_up"][l + 1] = got[0]
        small["ffn_conv"][l] = jnp.concatenate([dcw[0], dcw[1]], axis=1)
        big["w_ffn_up"], (partials["w_ffn_down"][l],) = _mm(
            sv["h3"], du, ta=True, b_halves=True, out_shards=True, out_dtype=BF16, name="mm_dw_ffn_up",
            carry=scatter(big, ("w_ffn_down",)))
        if l > 0:
            ffn_up_above = big["w_ffn_up"]
            dh = _mm(du, wl["w_ffn_up"], tb=True, a_halves=True, b_shards=True, out_dtype=BF16, name="mm_dh3")
        else:
            dh, (partials["w_ffn_up"][l],) = _mm(du, wl["w_ffn_up"], tb=True, a_halves=True, b_shards=True,
                                                 out_dtype=BF16, name="mm_dh3", carry=scatter(big, ("w_ffn_up",)))
        dx, dxb, small["ffn_norm"][l], _ = _rms_bwd(sv["x2"], dh, ffn_norm[l], dx, "rms_bwd_ffn")
        d_ox = _mm(dxb, wl["w_xo"], tb=True, out_dtype=BF16, name="mm_d_ox")
        big["w_xo"] = _mm(sv["ox"], dxb, ta=True, out_dtype=BF16, name="mm_dw_xo")
        d_qx, d_kx, d_vx, (partials["w_xo"][l],) = _xattn_bwd(sv["qx"], sv["kx"], sv["vx"], d_ox,
                                                               carry=scatter(big, ("w_xo",)))
        big["w_xq"] = _mm(sv["h2"], d_qx, ta=True, out_dtype=BF16, name="mm_dw_xq")
        big["w_xk"] = _mm(sv["mem_n"], d_kx, ta=True, out_dtype=BF16, name="mm_dw_xk")
        big["w_xv"] = _mm(sv["mem_n"], d_vx, ta=True, out_dtype=BF16, name="mm_dw_xv")
        dh, (partials["w_xq"][l],) = _mm(d_qx, wl["w_xq"], tb=True, out_dtype=BF16, name="mm_dh2",
                                         carry=scatter(big, ("w_xq",)))
        d_mem = _mm(d_kx, wl["w_xk"], tb=True, out_dtype=F32, name="mm_dmem_k")
        d_mem = _mm(d_vx, wl["w_xv"], tb=True, out_dtype=F32, res=d_mem, name="mm_dmem_v")
        _, _, small["mem_norm"][l], _ = _rms_bwd(memv, d_mem, mem_norm[l], jnp.zeros_like(memv), "rms_bwd_mem")
        dx, dxb, small["xattn_norm"][l], (partials["w_xk"][l],) = _rms_bwd(
            sv["x1"], dh, xattn_norm[l], dx, "rms_bwd_xattn", carry=scatter(big, ("w_xk",)))
        d_y, (partials["w_xv"][l],) = _mm(dxb, wl["w_mix_out"], tb=True, out_dtype=BF16, name="mm_d_y",
                                          carry=scatter(big, ("w_xv",)))
        big["w_mix_out"] = _mm(sv["y"], dxb, ta=True, out_dtype=BF16, name="mm_dw_mix_out")
        d_o, d_proj, small["gdn_out_norm"][l], small["sc_conv"][l], (partials["w_mix_out"][l],) = _mix_post_bwd(
            d_y, sv["o_raw"], sv["proj"], gdn_out_norm[l], conv_full["sc_conv"][l], width,
            carry=scatter(big, ("w_mix_out",)))
        d_qkv, d_gb, got = _gdn_bwd(sv["qkv"], sv["ba"], gp, d_o, sv["states"], sv["tinv"], width,
                                    carry=None if l + 1 >= depth else ("scatter", [mix_in_above], None))
        if got:
            partials["w_mix_in"][l + 1] = got[0]
        d_ba, d_gp = _gates_bwd(sv["ba"], d_gb, gp, nh)
        small["gdn_a_log"][l] = d_gp[0, nh:2 * nh]
        small["gdn_dt_bias"][l] = d_gp[1, nh:2 * nh]
        d_proj, small["gdn_conv"][l] = _gdn_prep_bwd(d_qkv, sv["proj"], conv_full["gdn_conv"][l], d_proj, width)
        dw_main = _mm(sv["h1"], d_proj, ta=True, name="mm_dw_mix_in")
        dw_gate = _mm(sv["h1"], d_ba, ta=True, name="mm_dw_mix_gates")
        big["w_mix_in"] = _mix_in_shards(dw_main, dw_gate, width, nh)
        if l > 0:
            mix_in_above = big["w_mix_in"]
            dh = _mm(d_proj, w_main, tb=True, out_dtype=F32, name="mm_dh1_main")
        else:
            dh, (partials["w_mix_in"][l],) = _mm(d_proj, w_main, tb=True, out_dtype=F32, name="mm_dh1_main",
                                                 carry=scatter(big, ("w_mix_in",)))
        dh = _mm(d_ba, w_gate, tb=True, out_dtype=BF16, res=dh, name="mm_dh1_gates")
        dx, dxb, small["mix_norm"][l], _ = _rms_bwd(sv["x0"], dh, mix_norm[l], dx, "rms_bwd_mix")

    small_names = ("mix_norm", "xattn_norm", "mem_norm", "ffn_norm", "gdn_a_log", "gdn_dt_bias", "gdn_out_norm",
                   "gdn_conv", "sc_conv", "ffn_conv")
    small_parts = [jnp.stack(small[n]).reshape(1, -1) for n in small_names]
    small_parts += [g_final.reshape(1, -1), loss_part.reshape(1, 1)]
    small_sizes = [p.shape[1] for p in small_parts]
    small_local = _pack(small_parts, (1,))
    small_sum = _slot_sum(_all_gather([small_local], "all_gather_small")[0][0], "sum_small")
    small_tot = _unpack(small_sum, small_sizes)
    grads = {}
    for n, g in zip(small_names, small_tot[:len(small_names)]):
        if n in conv_names:
            _, kt, cs = wts[n].shape
            g = lax.dynamic_slice_in_dim(g.reshape(depth, kt, N_DEV * cs), me * cs, cs, axis=2)
        grads[n] = g.reshape(wts[n].shape)
    grads["final_norm"] = small_tot[-2].reshape(final_norm.shape)
    loss = small_tot[-1].reshape(())

    delta, new_m, new_v = {}, {}, {}
    for n in names:
        if n in BIG:
            grads[n], delta[n], new_m[n], new_v[n] = _adamw_sharded(wts[n], mom1[n], mom2[n], partials[n], "adamw_" + n)
        else:
            delta[n], new_m[n], new_v[n] = _adamw(wts[n], grads[n], mom1[n], mom2[n], "adamw_" + n)
    return (loss, dx[None], *[grads[n] for n in names], *[delta[n] for n in names],
            *[new_m[n] for n in names], *[new_v[n] for n in names])
```
